```python
import math
import jax
import jax.numpy as jnp
from jax import lax
import numpy as np

D_MODEL = 4096
BATCH = 8
SEQ = 2048
DEPTH = 2

CHUNK = 64
Q_BLOCK = 128
D_FF = 11008
PLE_DIM = 256
N_BRANCHES = 3
DA_HEADS = 12
DA_QK_DIM = 64
DA_V_DIM = 128
GLA_HEADS = 4
GLA_DK = 128
GLA_DV = 256
GLA_GATE_RANK = 16
GLA_GATE_NORMALIZER = 16.0
MLA_HEADS = 12
MLA_Q_RANK = 768
MLA_KV_RANK = 512
MLA_NOPE_DIM = 128
MLA_ROPE_DIM = 64
MLA_V_DIM = 128
ROPE_THETA = 10000.0
LN_EPS = 1e-5
RMS_EPS = 1e-6
DEEPNORM_ALPHA = (2 * DEPTH) ** 0.25
DEEPNORM_BETA = (8 * DEPTH) ** -0.25

A_QK_W = DA_HEADS * 2 * DA_QK_DIM
A_V_W = DA_HEADS * DA_V_DIM
B_K_W = GLA_HEADS * GLA_DK
B_V_W = GLA_HEADS * GLA_DV
C_KVA_W = MLA_KV_RANK + MLA_ROPE_DIM
C_QB_W = MLA_HEADS * (MLA_NOPE_DIM + MLA_ROPE_DIM)
C_KVB_W = MLA_HEADS * (MLA_NOPE_DIM + MLA_V_DIM)
C_V_W = MLA_HEADS * MLA_V_DIM
IN_SPLITS = (A_QK_W, A_QK_W, A_V_W, B_K_W, B_K_W, B_V_W, GLA_GATE_RANK, B_V_W, MLA_Q_RANK, C_KVA_W)
IN_WIDTH = 2 * A_QK_W + A_V_W + 2 * B_K_W + 2 * B_V_W + GLA_GATE_RANK + MLA_Q_RANK + C_KVA_W

kernel_name = 'hybrid_chunk_causal_encoder'


def layer_norm(x, g, b):
    xf = x.astype(jnp.float32)
    mu = jnp.mean(xf, -1, keepdims=True)
    var = jnp.mean(jnp.square(xf - mu), -1, keepdims=True)
    return ((xf - mu) * lax.rsqrt(var + LN_EPS) * g + b).astype(x.dtype)


def rms_norm(x, g):
    xf = x.astype(jnp.float32)
    return (xf * lax.rsqrt(jnp.mean(jnp.square(xf), -1, keepdims=True) + RMS_EPS) * g).astype(x.dtype)


def alibi_slopes(n):
    def pow2_slopes(m):
        start = 2.0 ** (-8.0 / m)
        return [start ** (i + 1) for i in range(m)]
    c = 2 ** int(math.floor(math.log2(n)))
    s = pow2_slopes(c)
    if c < n:
        s = s + pow2_slopes(2 * c)[0::2][: n - c]
    return jnp.asarray(np.array(s, dtype=np.float32))


def chunk_mask(q_start, q_end):
    qi = jnp.arange(q_start, q_end)
    ki = jnp.arange(q_end)
    return (ki[None, :] // CHUNK) <= (qi[:, None] // CHUNK)


def block_sweep(block_fn, seq_len):
    return jnp.concatenate([block_fn(s, s + Q_BLOCK) for s in range(0, seq_len, Q_BLOCK)], axis=1)


def rope(x, pos):
    half = x.shape[-1] // 2
    inv = ROPE_THETA ** (-jnp.arange(half, dtype=jnp.float32) / half)
    ang = pos.astype(jnp.float32)[..., None] * inv
    cos = jnp.cos(ang)[:, :, None, :]
    sin = jnp.sin(ang)[:, :, None, :]
    xf = x.astype(jnp.float32)
    x1, x2 = xf[..., :half], xf[..., half:]
    return jnp.concatenate([x1 * cos - x2 * sin, x2 * cos + x1 * sin], axis=-1)


def swiglu(h, w_gu, w_down):
    gate, up = jnp.split(h @ w_gu, 2, axis=-1)
    return (jax.nn.silu(gate) * up) @ w_down


def diff_attention(q, k, v, lam, slopes):
    scale = DA_QK_DIM ** -0.5
    qf = q.astype(jnp.float32) * scale
    kf = k.astype(jnp.float32)
    vf = v.astype(jnp.float32)

    def block(s, e):
        scores = jnp.einsum('bqhmd,bkhmd->bmhqk', qf[:, s:e], kf[:, :e])
        dist = jnp.abs(jnp.arange(s, e)[:, None] - jnp.arange(e)[None, :]).astype(jnp.float32)
        scores = scores - slopes[:, None, None] * dist
        scores = jnp.where(chunk_mask(s, e), scores, -jnp.inf)
        probs = jax.nn.softmax(scores, axis=-1)
        w = probs[:, 0] - lam * probs[:, 1]
        return jnp.einsum('bhqk,bkhd->bqhd', w, vf[:, :e])

    return block_sweep(block, q.shape[1])


def gla_chunked(q, k, v, log_a):
    B, T, H, dk = q.shape
    dv = v.shape[-1]
    n = T // CHUNK

    def chunks(t):
        return t.astype(jnp.float32).reshape(B, n, CHUNK, H, t.shape[-1])

    qc = chunks(q) * dk ** -0.5
    kc = chunks(k)
    vc = chunks(v)
    b = jnp.cumsum(chunks(log_a), axis=2)
    b_mid = b[:, :, CHUNK // 2:CHUNK // 2 + 1]
    b_last = b[:, :, -1]
    att = jnp.einsum('bnihk,bnjhk->bnhij', qc * jnp.exp(b - b_mid), kc * jnp.exp(b_mid - b))
    causal = jnp.tril(jnp.ones((CHUNK, CHUNK), dtype=bool))
    o_intra = jnp.einsum('bnhij,bnjhv->bnihv', jnp.where(causal, att, 0.0), vc)
    u = jnp.einsum('bnjhk,bnjhv->bnhkv', kc * jnp.exp(b_last[:, :, None] - b), vc)
    decay = jnp.exp(b_last)

    def step(state, inp):
        d, uc = inp
        return d[..., None] * state + uc, state

    _, s_prev = lax.scan(step, jnp.zeros((B, H, dk, dv), jnp.float32),
                         (jnp.moveaxis(decay, 1, 0), jnp.moveaxis(u, 1, 0)))
    o_inter = jnp.einsum('bnihk,nbhkv->bnihv', qc * jnp.exp(b), s_prev)
    return (o_intra + o_inter).reshape(B, T, H, dv)


def mla_attention(q_nope, q_rope, k_nope, k_rope, v):
    scale = (MLA_NOPE_DIM + MLA_ROPE_DIM) ** -0.5

    def block(s, e):
        scores = (jnp.einsum('bqhd,bkhd->bhqk', q_nope[:, s:e], k_nope[:, :e])
                  + jnp.einsum('bqhd,bkd->bhqk', q_rope[:, s:e], k_rope[:, :e])) * scale
        scores = jnp.where(chunk_mask(s, e), scores, -jnp.inf)
        probs = jax.nn.softmax(scores, axis=-1)
        return jnp.einsum('bhqk,bkhd->bqhd', probs, v[:, :e])

    return block_sweep(block, q_nope.shape[1])


def token_mixer(h, positions, layer_idx, w_in, w_branch_gate, da_lambda_q1, da_lambda_k1, da_lambda_q2,
                da_lambda_k2, da_subln_g, gla_w_gate2, gla_b_gate, gla_norm_g, mla_q_norm_g, mla_w_qb,
                mla_kv_norm_g, mla_w_kvb, w_o_a, w_o_b, w_o_c, w_out):
    B, T, _ = h.shape
    f32 = jnp.float32
    offs = np.cumsum(IN_SPLITS)[:-1].tolist()
    a_q, a_k, a_v, b_q, b_k, b_v, b_glr, b_r, c_qa, c_kva = jnp.split(h @ w_in, offs, axis=-1)

    lam_init = 0.8 - 0.6 * math.exp(-0.3 * layer_idx)
    lam = (jnp.exp(jnp.sum(da_lambda_q1.astype(f32) * da_lambda_k1.astype(f32)))
           - jnp.exp(jnp.sum(da_lambda_q2.astype(f32) * da_lambda_k2.astype(f32))) + lam_init)
    o_a = diff_attention(a_q.reshape(B, T, DA_HEADS, 2, DA_QK_DIM), a_k.reshape(B, T, DA_HEADS, 2, DA_QK_DIM),
                         a_v.reshape(B, T, DA_HEADS, DA_V_DIM), lam, alibi_slopes(DA_HEADS))
    o_a = (rms_norm(o_a, da_subln_g) * (1.0 - lam_init)).reshape(B, T, A_V_W).astype(h.dtype)

    log_a = jax.nn.log_sigmoid((b_glr @ gla_w_gate2 + gla_b_gate).astype(f32)) / GLA_GATE_NORMALIZER
    o_b = gla_chunked(b_q.reshape(B, T, GLA_HEADS, GLA_DK), b_k.reshape(B, T, GLA_HEADS, GLA_DK),
                      b_v.reshape(B, T, GLA_HEADS, GLA_DV), log_a.reshape(B, T, GLA_HEADS, GLA_DK))
    o_b = (rms_norm(o_b, gla_norm_g).reshape(B, T, B_V_W) * jax.nn.silu(b_r.astype(f32))).astype(h.dtype)

    q = (rms_norm(c_qa, mla_q_norm_g) @ mla_w_qb).reshape(B, T, MLA_HEADS, MLA_NOPE_DIM + MLA_ROPE_DIM)
    q_nope = q[..., :MLA_NOPE_DIM].astype(f32)
    q_rope = rope(q[..., MLA_NOPE_DIM:], positions)
    c_kv, k_pe = c_kva[..., :MLA_KV_RANK], c_kva[..., MLA_KV_RANK:]
    kv = (rms_norm(c_kv, mla_kv_norm_g) @ mla_w_kvb).reshape(B, T, MLA_HEADS, MLA_NOPE_DIM + MLA_V_DIM)
    k_nope = kv[..., :MLA_NOPE_DIM].astype(f32)
    v_c = kv[..., MLA_NOPE_DIM:].astype(f32)
    k_rope = rope(k_pe[:, :, None, :], positions)[:, :, 0, :]
    o_c = mla_attention(q_nope, q_rope, k_nope, k_rope, v_c).reshape(B, T, C_V_W).astype(h.dtype)

    gates = jax.nn.sigmoid((h @ w_branch_gate).astype(f32)).reshape(B, T, N_BRANCHES, D_MODEL)
    merged = (gates[:, :, 0] * (o_a @ w_o_a) + gates[:, :, 1] * (o_b @ w_o_b)
              + gates[:, :, 2] * (o_c @ w_o_c))
    return merged.astype(h.dtype) @ w_out


def setup_inputs(seed: int = 0) -> dict:
    key = jax.random.key(seed)
    ks = jax.random.split(key, 32)
    f32 = jnp.float32
    D = D_MODEL
    beta = DEEPNORM_BETA

    def nrm(k, shape, scale=1.0):
        return jax.random.normal(k, shape, f32) * scale

    def gain(k, shape):
        return 1.0 + 0.01 * jax.random.normal(k, shape, f32)

    start_chunk = jax.random.randint(ks[2], (BATCH, 1), 0, 64, dtype=jnp.int32)
    positions = start_chunk * CHUNK + jnp.arange(SEQ, dtype=jnp.int32)[None, :]
    return {
        'x': nrm(ks[0], (BATCH, SEQ, D)),
        'p': nrm(ks[1], (DEPTH, BATCH, SEQ, PLE_DIM)),
        'positions': positions,
        'emb_ln_g': gain(ks[3], (D,)),
        'emb_ln_b': nrm(ks[4], (D,), 0.01),
        'w_in': nrm(ks[5], (DEPTH, D, IN_WIDTH), D ** -0.5),
        'w_branch_gate': nrm(ks[6], (DEPTH, D, N_BRANCHES * D), D ** -0.5),
        'da_lambda_q1': nrm(ks[7], (DEPTH, DA_QK_DIM), 0.1),
        'da_lambda_k1': nrm(ks[8], (DEPTH, DA_QK_DIM), 0.1),
        'da_lambda_q2': nrm(ks[9], (DEPTH, DA_QK_DIM), 0.1),
        'da_lambda_k2': nrm(ks[10], (DEPTH, DA_QK_DIM), 0.1),
        'da_subln_g': gain(ks[11], (DEPTH, DA_V_DIM)),
        'gla_w_gate2': nrm(ks[12], (DEPTH, GLA_GATE_RANK, B_K_W), GLA_GATE_RANK ** -0.5),
        'gla_b_gate': nrm(ks[13], (DEPTH, B_K_W), 0.1),
        'gla_norm_g': gain(ks[14], (DEPTH, GLA_DV)),
        'mla_q_norm_g': gain(ks[15], (DEPTH, MLA_Q_RANK)),
        'mla_w_qb': nrm(ks[16], (DEPTH, MLA_Q_RANK, C_QB_W), MLA_Q_RANK ** -0.5),
        'mla_kv_norm_g': gain(ks[17], (DEPTH, MLA_KV_RANK)),
        'mla_w_kvb': nrm(ks[18], (DEPTH, MLA_KV_RANK, C_KVB_W), MLA_KV_RANK ** -0.5),
        'w_o_a': nrm(ks[19], (DEPTH, A_V_W, D), beta * A_V_W ** -0.5),
        'w_o_b': nrm(ks[20], (DEPTH, B_V_W, D), beta * B_V_W ** -0.5),
        'w_o_c': nrm(ks[21], (DEPTH, C_V_W, D), beta * C_V_W ** -0.5),
        'w_out': nrm(ks[22], (DEPTH, D, D), beta * D ** -0.5),
        'ffn_w_gu': nrm(ks[23], (DEPTH, 2, D, 2 * D_FF), D ** -0.5),
        'ffn_w_down': nrm(ks[24], (DEPTH, 2, D_FF, D), beta * D_FF ** -0.5),
        'ln_g': gain(ks[25], (DEPTH, 3, D)),
        'ln_b': nrm(ks[26], (DEPTH, 3, D), 0.01),
        'ple_w_proj': nrm(ks[27], (DEPTH, PLE_DIM, D), PLE_DIM ** -0.5),
        'ple_w_gate': nrm(ks[28], (DEPTH, D, D), D ** -0.5),
    }


def reference(x, p, positions, emb_ln_g, emb_ln_b, w_in, w_branch_gate, da_lambda_q1, da_lambda_k1,
              da_lambda_q2, da_lambda_k2, da_subln_g, gla_w_gate2, gla_b_gate, gla_norm_g, mla_q_norm_g,
              mla_w_qb, mla_kv_norm_g, mla_w_kvb, w_o_a, w_o_b, w_o_c, w_out, ffn_w_gu, ffn_w_down,
              ln_g, ln_b, ple_w_proj, ple_w_gate):
    h = layer_norm(x, emb_ln_g, emb_ln_b)
    for i in range(DEPTH):
        h = layer_norm(DEEPNORM_ALPHA * h + 0.5 * swiglu(h, ffn_w_gu[i, 0], ffn_w_down[i, 0]),
                       ln_g[i, 0], ln_b[i, 0])
        mix = token_mixer(h, positions, i, w_in[i], w_branch_gate[i], da_lambda_q1[i], da_lambda_k1[i],
                          da_lambda_q2[i], da_lambda_k2[i], da_subln_g[i], gla_w_gate2[i], gla_b_gate[i],
                          gla_norm_g[i], mla_q_norm_g[i], mla_w_qb[i], mla_kv_norm_g[i], mla_w_kvb[i],
                          w_o_a[i], w_o_b[i], w_o_c[i], w_out[i])
        h = layer_norm(DEEPNORM_ALPHA * h + mix, ln_g[i, 1], ln_b[i, 1])
        ple = jax.nn.sigmoid(h @ ple_w_gate[i]) * (p[i] @ ple_w_proj[i])
        h = layer_norm(DEEPNORM_ALPHA * h + 0.5 * swiglu(h, ffn_w_gu[i, 1], ffn_w_down[i, 1]) + ple,
                       ln_g[i, 2], ln_b[i, 2])
    return h
```

```python
import functools
import math

import numpy as np
import jax
import jax.numpy as jnp
from jax import lax
from jax.experimental import pallas as pl
from jax.experimental.pallas import tpu as pltpu

F32 = jnp.float32
BF16 = jnp.bfloat16

DEPTH = 2
CHUNK = 64
N_BRANCHES = 3
DA_HEADS = 12
DA_QK_DIM = 64
DA_V_DIM = 128
GLA_HEADS = 4
GLA_DK = 128
GLA_DV = 256
GLA_GATE_RANK = 16
GLA_GATE_NORMALIZER = 16.0
MLA_HEADS = 12
MLA_Q_RANK = 768
MLA_KV_RANK = 512
MLA_NOPE_DIM = 128
MLA_ROPE_DIM = 64
MLA_V_DIM = 128
ROPE_THETA = 10000.0
LN_EPS = 1e-5
RMS_EPS = 1e-6
DEEPNORM_ALPHA = (2 * DEPTH) ** 0.25

A_QK_W = DA_HEADS * 2 * DA_QK_DIM
A_V_W = DA_HEADS * DA_V_DIM
B_K_W = GLA_HEADS * GLA_DK
B_V_W = GLA_HEADS * GLA_DV
C_V_W = MLA_HEADS * MLA_V_DIM
MLA_HEAD_PAD = 256
MLA_QK_W = MLA_HEADS * MLA_HEAD_PAD

ZF_CQA = 0
ZF_KPE = 768
ZF_BQ = 1024
ZF_BK = 1536
ZF_BV = 2048
ZF_BR = 3072
ZF_CKV = 4096
ZF_KPER = 4608
ZF_GLR = 4864
ZF_W = 5120

VMEM_CAP_V7X = 64 * 1024 * 1024
VMEM_LIMIT = VMEM_CAP_V7X - 8 * 1024 * 1024
NEG_BIG = -1e30


def _cparams(*sem):
    return pltpu.CompilerParams(dimension_semantics=sem, vmem_limit_bytes=VMEM_LIMIT)


def _dot(a, b):
    return jnp.dot(a, b, preferred_element_type=F32)


def _dot_nt(a, b):
    return lax.dot_general(a, b, (((1,), (1,)), ((), ())), preferred_element_type=F32)


def _dot_tn(a, b):
    return lax.dot_general(a, b, (((0,), (0,)), ((), ())), preferred_element_type=F32)


def _ln_rows(y, g, b):
    mu = jnp.mean(y, axis=-1, keepdims=True)
    yc = y - mu
    var = jnp.mean(yc * yc, axis=-1, keepdims=True)
    return yc * lax.rsqrt(var + LN_EPS) * g + b


def _rms_rows(y, g):
    return y * lax.rsqrt(jnp.mean(y * y, axis=-1, keepdims=True) + RMS_EPS) * g


def _ln_kernel(x_ref, g_ref, b_ref, o_ref, ob_ref):
    y = _ln_rows(x_ref[...], g_ref[...], b_ref[...])
    o_ref[...] = y
    ob_ref[...] = y.astype(BF16)


def _layer_norm(x, g, b):
    n, d = x.shape
    tr = min(256, n)
    row = pl.BlockSpec((tr, d), lambda i: (i, 0))
    vec = pl.BlockSpec((1, d), lambda i: (0, 0))
    return pl.pallas_call(
        _ln_kernel, grid=(n // tr,), in_specs=[row, vec, vec], out_specs=[row, row],
        out_shape=[jax.ShapeDtypeStruct((n, d), F32), jax.ShapeDtypeStruct((n, d), BF16)],
        compiler_params=_cparams("parallel"), name="ln_emb",
    )(x, g.reshape(1, d), b.reshape(1, d))


LN_SLAB = 16


def _resid_copy(resid_hbm, o_ref, sem):
    tm = o_ref.shape[0]
    start = pl.multiple_of(pl.program_id(0) * tm, tm)
    return pltpu.make_async_copy(resid_hbm.at[pl.ds(start, tm), :], o_ref, sem)


def _ln_epilogue(o_ref, ob_ref, g_ref, b_ref, out_scale):
    g = g_ref[...]
    b = b_ref[...]

    def body(r, carry):
        sl = pl.ds(pl.multiple_of(r * LN_SLAB, LN_SLAB), LN_SLAB)
        y = _ln_rows(o_ref[sl, :] * out_scale, g, b)
        o_ref[sl, :] = y
        ob_ref[sl, :] = y.astype(BF16)
        return carry

    lax.fori_loop(0, o_ref.shape[0] // LN_SLAB, body, 0)


def _ffn_kernel(resid_hbm, hb_ref, wg_ref, wu_ref, wd_ref, g_ref, b_ref, o_ref, ob_ref, sem, *, res_scale):
    f = pl.program_id(1)

    @pl.when(f == 0)
    def _():
        _resid_copy(resid_hbm, o_ref, sem).start()

    x = hb_ref[...]
    gate = _dot(x, wg_ref[...])
    up = _dot(x, wu_ref[...])
    act = (gate * jax.nn.sigmoid(gate) * up).astype(BF16)

    @pl.when(f == 0)
    def _():
        _resid_copy(resid_hbm, o_ref, sem).wait()
        o_ref[...] = o_ref[...] * (2.0 * res_scale)

    o_ref[...] += _dot(act, wd_ref[...])

    @pl.when(f == pl.num_programs(1) - 1)
    def _():
        _ln_epilogue(o_ref, ob_ref, g_ref, b_ref, 0.5)


def _ffn_ln(resid, hb, w_gu, w_down, ln_g, ln_b, res_scale):
    n, d = hb.shape
    ff = w_down.shape[0]
    tm = min(512, n)
    tf = 256
    nf = ff // tf
    vec = pl.BlockSpec((1, d), lambda i, f: (0, 0))
    row = pl.BlockSpec((tm, d), lambda i, f: (i, 0))
    return pl.pallas_call(
        functools.partial(_ffn_kernel, res_scale=res_scale),
        grid=(n // tm, nf),
        in_specs=[
            pl.BlockSpec(memory_space=pl.ANY),
            row,
            pl.BlockSpec((d, tf), lambda i, f: (0, f)),
            pl.BlockSpec((d, tf), lambda i, f: (0, nf + f)),
            pl.BlockSpec((tf, d), lambda i, f: (f, 0)),
            vec, vec,
        ],
        out_specs=[row, row],
        out_shape=[jax.ShapeDtypeStruct((n, d), F32), jax.ShapeDtypeStruct((n, d), BF16)],
        scratch_shapes=[pltpu.SemaphoreType.DMA(())],
        compiler_params=_cparams("parallel", "arbitrary"), name="ffn_ln",
    )(resid, hb, w_gu, w_gu, w_down, ln_g.reshape(1, d), ln_b.reshape(1, d))


def _proj_ln_kernel(resid_hbm, x_ref, w_ref, g_ref, b_ref, o_ref, ob_ref, sem, *, res_scale):
    k = pl.program_id(1)

    @pl.when(k == 0)
    def _():
        cp = _resid_copy(resid_hbm, o_ref, sem)
        cp.start()
        cp.wait()
        o_ref[...] = o_ref[...] * res_scale

    o_ref[...] += _dot(x_ref[...], w_ref[...])

    @pl.when(k == pl.num_programs(1) - 1)
    def _():
        _ln_epilogue(o_ref, ob_ref, g_ref, b_ref, 1.0)


def _proj_ln(resid, x, w, ln_g, ln_b, res_scale):
    n, kdim = x.shape
    d = w.shape[1]
    tm = min(512, n)
    tk = 512
    vec = pl.BlockSpec((1, d), lambda i, k: (0, 0))
    row = pl.BlockSpec((tm, d), lambda i, k: (i, 0))
    return pl.pallas_call(
        functools.partial(_proj_ln_kernel, res_scale=res_scale),
        grid=(n // tm, kdim // tk),
        in_specs=[
            pl.BlockSpec(memory_space=pl.ANY),
            pl.BlockSpec((tm, tk), lambda i, k: (i, k)),
            pl.BlockSpec((tk, d), lambda i, k: (k, 0)),
            vec, vec,
        ],
        out_specs=[row, row],
        out_shape=[jax.ShapeDtypeStruct((n, d), F32), jax.ShapeDtypeStruct((n, d), BF16)],
        scratch_shapes=[pltpu.SemaphoreType.DMA(())],
        compiler_params=_cparams("parallel", "arbitrary"), name="proj_ln",
    )(resid, x, w, ln_g.reshape(1, d), ln_b.reshape(1, d))


def _mm_kernel(x_ref, w_ref, o_ref):
    o_ref[...] = _dot(x_ref[...], w_ref[...]).astype(o_ref.dtype)


def _matmul(x, w, out_dtype, name):
    n, kdim = x.shape
    m = w.shape[1]
    tm = min(1024, n)
    tn = 512
    return pl.pallas_call(
        _mm_kernel, grid=(n // tm, m // tn),
        in_specs=[pl.BlockSpec((tm, kdim), lambda i, j: (i, 0)), pl.BlockSpec((kdim, tn), lambda i, j: (0, j))],
        out_specs=pl.BlockSpec((tm, tn), lambda i, j: (i, j)),
        out_shape=jax.ShapeDtypeStruct((n, m), out_dtype),
        compiler_params=_cparams("parallel", "parallel"), name=name,
    )(x, w)


def _ple_kernel(hb_ref, wg_ref, p_ref, wp_ref, h_ref, o_ref, *, alpha):
    gate = jax.nn.sigmoid(_dot(hb_ref[...], wg_ref[...]))
    proj = _dot(p_ref[...].astype(BF16), wp_ref[...])
    o_ref[...] = alpha * h_ref[...] + gate * proj


def _ple_resid(hb, w_gate, p, w_proj, h, alpha):
    n, d = hb.shape
    pd = p.shape[1]
    tm = min(1024, n)
    tn = 512
    tile = pl.BlockSpec((tm, tn), lambda i, j: (i, j))
    return pl.pallas_call(
        functools.partial(_ple_kernel, alpha=alpha), grid=(n // tm, d // tn),
        in_specs=[
            pl.BlockSpec((tm, d), lambda i, j: (i, 0)),
            pl.BlockSpec((d, tn), lambda i, j: (0, j)),
            pl.BlockSpec((tm, pd), lambda i, j: (i, 0)),
            pl.BlockSpec((pd, tn), lambda i, j: (0, j)),
            tile,
        ],
        out_specs=tile,
        out_shape=jax.ShapeDtypeStruct((n, d), F32),
        compiler_params=_cparams("parallel", "parallel"), name="ple_resid",
    )(hb, w_gate, p, w_proj, h)


def _merge_kernel(hb_ref, oa_ref, ob_ref, oc_ref, g0_ref, g1_ref, g2_ref, wa_ref, wb_ref, wc_ref, o_ref):
    x = hb_ref[...]
    acc = jax.nn.sigmoid(_dot(x, g0_ref[...])) * _dot(oa_ref[...], wa_ref[...])
    acc += jax.nn.sigmoid(_dot(x, g1_ref[...])) * _dot(ob_ref[...], wb_ref[...])
    acc += jax.nn.sigmoid(_dot(x, g2_ref[...])) * _dot(oc_ref[...], wc_ref[...])
    o_ref[...] = acc.astype(o_ref.dtype)


def _merge(hb, o_a, o_b, o_c, w_bg, w_oa, w_ob, w_oc):
    n, d = hb.shape
    tm = min(512, n)
    tn = 256
    nj = d // tn

    def rows(w):
        return pl.BlockSpec((tm, w), lambda i, j: (i, 0))

    def cols(kdim, off):
        return pl.BlockSpec((kdim, tn), lambda i, j: (0, off + j))

    return pl.pallas_call(
        _merge_kernel, grid=(n // tm, nj),
        in_specs=[rows(d), rows(o_a.shape[1]), rows(o_b.shape[1]), rows(o_c.shape[1]),
                  cols(d, 0), cols(d, nj), cols(d, 2 * nj),
                  cols(w_oa.shape[0], 0), cols(w_ob.shape[0], 0), cols(w_oc.shape[0], 0)],
        out_specs=pl.BlockSpec((tm, tn), lambda i, j: (i, j)),
        out_shape=jax.ShapeDtypeStruct((n, d), BF16),
        compiler_params=_cparams("parallel", "parallel"), name="merge",
    )(hb, o_a, o_b, o_c, w_bg, w_bg, w_bg, w_oa, w_ob, w_oc)


def _rope_tab_kernel(pos_ref, inv_ref, c_ref, s_ref):
    ang = pos_ref[...].astype(F32) * inv_ref[...]
    lane = lax.broadcasted_iota(jnp.int32, ang.shape, 1)
    rope = (lane >= MLA_NOPE_DIM) & (lane < MLA_NOPE_DIM + MLA_ROPE_DIM)
    c_ref[...] = jnp.where(lane < MLA_NOPE_DIM, 1.0, jnp.where(rope, jnp.cos(ang), 0.0))
    s_ref[...] = jnp.where(rope, jnp.sin(ang), 0.0)


def _rope_tables(positions):
    n = positions.size
    half = MLA_ROPE_DIM // 2
    inv = ROPE_THETA ** (-np.arange(half, dtype=np.float32) / half)
    inv_row = np.zeros((1, MLA_HEAD_PAD), np.float32)
    inv_row[0, MLA_NOPE_DIM:MLA_NOPE_DIM + MLA_ROPE_DIM] = np.concatenate([inv, inv])
    tr = min(512, n)
    tab = pl.BlockSpec((tr, MLA_HEAD_PAD), lambda i: (i, 0))
    return pl.pallas_call(
        _rope_tab_kernel, grid=(n // tr,),
        in_specs=[pl.BlockSpec((tr, 1), lambda i: (i, 0)), pl.BlockSpec((1, MLA_HEAD_PAD), lambda i: (0, 0))],
        out_specs=[tab, tab],
        out_shape=[jax.ShapeDtypeStruct((n, MLA_HEAD_PAD), F32)] * 2,
        compiler_params=_cparams("parallel"), name="rope_tables",
    )(positions.reshape(n, 1), jnp.asarray(inv_row))


def _mla_proj_kernel(cqa_ref, ckv_ref, kpe_ref, kper_ref, c_ref, s_ref, gq_ref, gkv_ref,
                     wq_ref, wqr_ref, wkn_ref, wv_ref, q_out, k_out, v_out):
    cos = c_ref[...]
    sin = s_ref[...]
    xq = _rms_rows(cqa_ref[...], gq_ref[...]).astype(BF16)
    q = _dot(xq, wq_ref[...])
    q_rot = _dot(xq, wqr_ref[...])
    xkv = _rms_rows(ckv_ref[...], gkv_ref[...]).astype(BF16)
    k_nope = _dot(xkv, wkn_ref[...])
    k_rope = kpe_ref[...] * cos + kper_ref[...] * sin
    for h in range(MLA_HEADS):
        sl = slice(h * MLA_HEAD_PAD, (h + 1) * MLA_HEAD_PAD)
        q_out[:, sl] = (q[:, sl] * cos + q_rot[:, sl] * sin).astype(BF16)
        k_out[:, sl] = (k_nope[:, sl] + k_rope).astype(BF16)
    v_out[...] = _dot(xkv, wv_ref[...]).astype(BF16)


def _mla_proj(z_f, cos_tab, sin_tab, g_q, g_kv, w_q, w_qrot, w_kn, w_v):
    n = z_f.shape[0]
    tm = min(256, n)

    def zcols(width, off):
        return pl.BlockSpec((tm, width), lambda i: (i, off // width))

    def whole(a):
        return pl.BlockSpec(a.shape, lambda i: (0, 0))

    def rows(width):
        return pl.BlockSpec((tm, width), lambda i: (i, 0))

    g_q = g_q.reshape(1, -1)
    g_kv = g_kv.reshape(1, -1)
    return pl.pallas_call(
        _mla_proj_kernel, grid=(n // tm,),
        in_specs=[zcols(MLA_Q_RANK, ZF_CQA), zcols(MLA_KV_RANK, ZF_CKV), zcols(MLA_HEAD_PAD, ZF_KPE),
                  zcols(MLA_HEAD_PAD, ZF_KPER), rows(MLA_HEAD_PAD), rows(MLA_HEAD_PAD),
                  whole(g_q), whole(g_kv), whole(w_q), whole(w_qrot), whole(w_kn), whole(w_v)],
        out_specs=[rows(MLA_QK_W), rows(MLA_QK_W), rows(C_V_W)],
        out_shape=[jax.ShapeDtypeStruct((n, MLA_QK_W), BF16), jax.ShapeDtypeStruct((n, MLA_QK_W), BF16),
                   jax.ShapeDtypeStruct((n, C_V_W), BF16)],
        compiler_params=_cparams("parallel"), name="mla_proj",
    )(z_f, z_f, z_f, z_f, cos_tab, sin_tab, g_q, g_kv, w_q, w_qrot, w_kn, w_v)


ATTN_TILE = 256


def _attn_kernel(slopes_ref, lam_ref, q_ref, k_ref, v_ref, g_ref, o_ref, m_sc, l_sc, acc_sc,
                 *, diff, score_scale, lam_init):
    t = ATTN_TILE
    qi = pl.program_id(2)
    q = q_ref[0]
    if diff:
        lane = lax.broadcasted_iota(jnp.int32, q.shape, 1)
        zero = jnp.zeros_like(q)
        qk_scale = jnp.asarray(DA_QK_DIM ** -0.5, q.dtype)
        qs = [jnp.where(lane < DA_QK_DIM, q * qk_scale, zero), jnp.where(lane >= DA_QK_DIM, q * qk_scale, zero)]
        slope = slopes_ref[pl.program_id(1)]
    else:
        qs = [q]
    nmap = len(qs)
    row = lax.broadcasted_iota(jnp.int32, (t, t), 0)
    col = lax.broadcasted_iota(jnp.int32, (t, t), 1)

    m_sc[...] = jnp.full(m_sc.shape, NEG_BIG, F32)
    l_sc[...] = jnp.zeros(l_sc.shape, F32)
    acc_sc[...] = jnp.zeros(acc_sc.shape, F32)

    def step(kj, masked):
        ks = pl.ds(pl.multiple_of(kj * t, t), t)
        k = k_ref[0, ks, :]
        v = v_ref[0, ks, :]
        if diff:
            bias = slope * jnp.abs((qi - kj) * t + (row - col)).astype(F32)
        if masked:
            allowed = (col // CHUNK) <= (row // CHUNK)
        for i in range(nmap):
            s = _dot_nt(qs[i], k)
            if score_scale != 1.0:
                s = s * score_scale
            if diff:
                s = s - bias
            if masked:
                s = jnp.where(allowed, s, NEG_BIG)
            m_prev = m_sc[i]
            m_new = jnp.maximum(m_prev, jnp.max(s, axis=1, keepdims=True))
            alpha = jnp.exp(m_prev - m_new)
            p = jnp.exp(s - m_new)
            l_sc[i] = alpha * l_sc[i] + jnp.sum(p, axis=1, keepdims=True)
            acc_sc[i] = alpha * acc_sc[i] + _dot(p.astype(BF16), v)
            m_sc[i] = m_new

    def full_step(kj, carry):
        step(kj, False)
        return carry

    lax.fori_loop(0, qi, full_step, 0)
    step(qi, True)

    o = acc_sc[0] / l_sc[0]
    if diff:
        lp = lam_ref[...]
        lam = (jnp.exp(jnp.sum(lp[0:1] * lp[1:2], axis=1, keepdims=True))
               - jnp.exp(jnp.sum(lp[2:3] * lp[3:4], axis=1, keepdims=True)) + lam_init)
        o = o - lam * (acc_sc[1] / l_sc[1])
        o = _rms_rows(o, g_ref[...]) * (1.0 - lam_init)
    o_ref[0] = o.astype(o_ref.dtype)


def _attention(q_arr, k_arr, v_arr, q_off, k_off, v_off, dqk, dv, heads, *, diff, score_scale,
               slopes, lam_params, gain, lam_init, name):
    b, t, _ = q_arr.shape
    tq = ATTN_TILE
    nmap = 2 if diff else 1
    return pl.pallas_call(
        functools.partial(_attn_kernel, diff=diff, score_scale=score_scale, lam_init=lam_init),
        grid=(b, heads, t // tq),
        in_specs=[
            pl.BlockSpec(memory_space=pltpu.SMEM),
            pl.BlockSpec(lam_params.shape, lambda bi, h, i: (0, 0)),
            pl.BlockSpec((1, tq, dqk), lambda bi, h, i: (bi, i, q_off + h)),
            pl.BlockSpec((1, t, dqk), lambda bi, h, i: (bi, 0, k_off + h)),
            pl.BlockSpec((1, t, dv), lambda bi, h, i: (bi, 0, v_off + h)),
            pl.BlockSpec(gain.shape, lambda bi, h, i: (0, 0)),
        ],
        out_specs=pl.BlockSpec((1, tq, dv), lambda bi, h, i: (bi, i, h)),
        out_shape=jax.ShapeDtypeStruct((b, t, heads * dv), BF16),
        scratch_shapes=[pltpu.VMEM((nmap, tq, 1), F32), pltpu.VMEM((nmap, tq, 1), F32),
                        pltpu.VMEM((nmap, tq, dv), F32)],
        compiler_params=_cparams("parallel", "parallel", "arbitrary"), name=name,
    )(slopes, lam_params, q_arr, k_arr, v_arr, gain)


def _alibi_slopes(n):
    def pow2_slopes(m):
        start = 2.0 ** (-8.0 / m)
        return [start ** (i + 1) for i in range(m)]
    c = 2 ** int(math.floor(math.log2(n)))
    s = pow2_slopes(c)
    if c < n:
        s = s + pow2_slopes(2 * c)[0::2][: n - c]
    return np.array(s, dtype=np.float32)


def _gla_kernel(q_ref, k_ref, v_ref, r_ref, glr_ref, w2_ref, bg_ref, ng_ref, o_ref, st_sc):
    c = CHUNK
    scale = GLA_DK ** -0.5
    ri = lax.broadcasted_iota(jnp.int32, (c, c), 0)
    ci = lax.broadcasted_iota(jnp.int32, (c, c), 1)
    causal = ci <= ri
    tri = causal.astype(BF16)
    w2 = w2_ref[...]
    bg = bg_ref[...]
    ng = ng_ref[...]
    st_sc[...] = jnp.zeros(st_sc.shape, F32)

    def body(n, carry):
        sl = pl.ds(pl.multiple_of(n * c, c), c)
        pre = _dot(glr_ref[0, sl, :].astype(BF16), w2) + bg
        log_a = (jnp.minimum(pre, 0.0) - jnp.log1p(jnp.exp(-jnp.abs(pre)))) / GLA_GATE_NORMALIZER
        hi = log_a.astype(BF16)
        rem = log_a - hi.astype(F32)
        mid = rem.astype(BF16)
        lo = (rem - mid.astype(F32)).astype(BF16)
        bcum = _dot(tri, hi) + _dot(tri, mid) + _dot(tri, lo)
        b_mid = bcum[c // 2:c // 2 + 1, :]
        b_last = bcum[c - 1:c, :]
        q = q_ref[0, sl, :] * scale
        k = k_ref[0, sl, :]
        v = v_ref[0, sl, :].astype(BF16)
        att = _dot_nt((q * jnp.exp(bcum - b_mid)).astype(BF16), (k * jnp.exp(b_mid - bcum)).astype(BF16))
        att = jnp.where(causal, att, 0.0)
        state_t = st_sc[...]
        o = _dot(att.astype(BF16), v) + _dot_nt((q * jnp.exp(bcum)).astype(BF16), state_t.astype(BF16))
        upd_t = _dot_tn(v, (k * jnp.exp(b_last - bcum)).astype(BF16))
        st_sc[...] = state_t * jnp.exp(b_last) + upd_t
        r = r_ref[0, sl, :]
        o_ref[0, sl, :] = (_rms_rows(o, ng) * (r * jax.nn.sigmoid(r))).astype(o_ref.dtype)
        return carry

    lax.fori_loop(0, q_ref.shape[1] // c, body, 0)


def _gla(z_f3, w_gate2_pad, b_gate, norm_g):
    b, t, _ = z_f3.shape

    def zcols(width, off):
        return pl.BlockSpec((1, t, width), lambda bi, h: (bi, 0, off // width + h))

    return pl.pallas_call(
        _gla_kernel, grid=(b, GLA_HEADS),
        in_specs=[zcols(GLA_DK, ZF_BQ), zcols(GLA_DK, ZF_BK), zcols(GLA_DV, ZF_BV), zcols(GLA_DV, ZF_BR),
                  pl.BlockSpec((1, t, 128), lambda bi, h: (bi, 0, ZF_GLR // 128)),
                  pl.BlockSpec((128, GLA_DK), lambda bi, h: (0, h)),
                  pl.BlockSpec((1, GLA_DK), lambda bi, h: (0, h)),
                  pl.BlockSpec((1, GLA_DV), lambda bi, h: (0, 0))],
        out_specs=pl.BlockSpec((1, t, GLA_DV), lambda bi, h: (bi, 0, h)),
        out_shape=jax.ShapeDtypeStruct((b, t, B_V_W), BF16),
        scratch_shapes=[pltpu.VMEM((GLA_DV, GLA_DK), F32)],
        compiler_params=_cparams("parallel", "parallel"), name="gla",
    )(z_f3, z_f3, z_f3, z_f3, z_f3, w_gate2_pad, b_gate.reshape(1, -1), norm_g.reshape(1, -1))


def _rot_half_cols(w):
    half = w.shape[-1] // 2
    return jnp.concatenate([-w[..., half:], w[..., :half]], axis=-1)


def _prep_in_proj(w_in):
    d = w_in.shape[0]
    offs = np.cumsum((A_QK_W, A_QK_W, A_V_W, B_K_W, B_K_W, B_V_W, GLA_GATE_RANK, B_V_W, MLA_Q_RANK,
                      MLA_KV_RANK + MLA_ROPE_DIM))
    a_end = offs[2]
    b_q, b_k, b_v, b_glr, b_r, c_qa, c_kva = [w_in[:, offs[i]:offs[i + 1]] for i in range(2, 9)]
    c_kv, k_pe = c_kva[:, :MLA_KV_RANK], c_kva[:, MLA_KV_RANK:]

    def zeros(width):
        return jnp.zeros((d, width), w_in.dtype)

    def in_rope_lanes(w):
        return jnp.concatenate([zeros(MLA_NOPE_DIM), w, zeros(MLA_HEAD_PAD - MLA_NOPE_DIM - MLA_ROPE_DIM)], axis=1)

    w_f = jnp.concatenate([c_qa, in_rope_lanes(k_pe), b_q, b_k, b_v, b_r, c_kv, in_rope_lanes(_rot_half_cols(k_pe)),
                           b_glr, zeros(ZF_W - ZF_GLR - GLA_GATE_RANK)], axis=1)
    assert w_f.shape[1] == ZF_W
    return w_in[:, :a_end].astype(BF16), w_f.astype(BF16)


def _prep_mla(w_qb, w_kvb):
    rq = w_qb.shape[0]
    wq = w_qb.reshape(rq, MLA_HEADS, MLA_NOPE_DIM + MLA_ROPE_DIM)
    nope, rope = wq[..., :MLA_NOPE_DIM], wq[..., MLA_NOPE_DIM:]
    pad = jnp.zeros((rq, MLA_HEADS, MLA_HEAD_PAD - MLA_NOPE_DIM - MLA_ROPE_DIM), w_qb.dtype)
    w_q = jnp.concatenate([nope, rope, pad], axis=-1).reshape(rq, MLA_QK_W)
    w_qrot = jnp.concatenate([jnp.zeros_like(nope), _rot_half_cols(rope), pad], axis=-1).reshape(rq, MLA_QK_W)
    rkv = w_kvb.shape[0]
    wkv = w_kvb.reshape(rkv, MLA_HEADS, MLA_NOPE_DIM + MLA_V_DIM)
    k_nope, v = wkv[..., :MLA_NOPE_DIM], wkv[..., MLA_NOPE_DIM:]
    w_kn = jnp.concatenate([k_nope, jnp.zeros((rkv, MLA_HEADS, MLA_HEAD_PAD - MLA_NOPE_DIM), w_kvb.dtype)],
                           axis=-1).reshape(rkv, MLA_QK_W)
    return w_q.astype(BF16), w_qrot.astype(BF16), w_kn.astype(BF16), v.reshape(rkv, C_V_W).astype(BF16)


def kernel(x, p, positions, emb_ln_g, emb_ln_b, w_in, w_branch_gate, da_lambda_q1, da_lambda_k1, da_lambda_q2,
           da_lambda_k2, da_subln_g, gla_w_gate2, gla_b_gate, gla_norm_g, mla_q_norm_g, mla_w_qb, mla_kv_norm_g,
           mla_w_kvb, w_o_a, w_o_b, w_o_c, w_out, ffn_w_gu, ffn_w_down, ln_g, ln_b, ple_w_proj, ple_w_gate):
    b, t, d = x.shape
    n = b * t
    depth = w_in.shape[0]
    alpha = float(DEEPNORM_ALPHA)
    cos_tab, sin_tab = _rope_tables(positions)
    slopes = jnp.asarray(_alibi_slopes(DA_HEADS))
    no_slopes = jnp.zeros((1,), F32)
    no_lam = jnp.zeros((4, DA_QK_DIM), F32)
    no_gain = jnp.ones((1, MLA_V_DIM), F32)

    h, hb = _layer_norm(x.reshape(n, d), emb_ln_g, emb_ln_b)
    for i in range(depth):
        w_gu = ffn_w_gu[i].astype(BF16)
        w_down = ffn_w_down[i].astype(BF16)
        h, hb = _ffn_ln(h, hb, w_gu[0], w_down[0], ln_g[i, 0], ln_b[i, 0], alpha)

        w_a, w_f = _prep_in_proj(w_in[i])
        z_a = _matmul(hb, w_a, BF16, "in_proj_a").reshape(b, t, -1)
        z_f = _matmul(hb, w_f, F32, "in_proj_f")
        lam_init = 0.8 - 0.6 * math.exp(-0.3 * i)
        lam_params = jnp.stack([da_lambda_q1[i], da_lambda_k1[i], da_lambda_q2[i], da_lambda_k2[i]]).astype(F32)
        o_a = _attention(z_a, z_a, z_a, 0, DA_HEADS, 2 * DA_HEADS, 2 * DA_QK_DIM, DA_V_DIM, DA_HEADS,
                         diff=True, score_scale=1.0, slopes=slopes, lam_params=lam_params,
                         gain=da_subln_g[i].reshape(1, -1), lam_init=lam_init, name="diff_attn")
        w2_pad = jnp.zeros((128, B_K_W), BF16).at[:GLA_GATE_RANK].set(gla_w_gate2[i].astype(BF16))
        o_b = _gla(z_f.reshape(b, t, ZF_W), w2_pad, gla_b_gate[i], gla_norm_g[i])
        w_q, w_qrot, w_kn, w_v = _prep_mla(mla_w_qb[i], mla_w_kvb[i])
        q_c, k_c, v_c = _mla_proj(z_f, cos_tab, sin_tab, mla_q_norm_g[i], mla_kv_norm_g[i], w_q, w_qrot, w_kn, w_v)
        o_c = _attention(q_c.reshape(b, t, -1), k_c.reshape(b, t, -1), v_c.reshape(b, t, -1), 0, 0, 0,
                         MLA_HEAD_PAD, MLA_V_DIM, MLA_HEADS, diff=False,
                         score_scale=(MLA_NOPE_DIM + MLA_ROPE_DIM) ** -0.5, slopes=no_slopes, lam_params=no_lam,
                         gain=no_gain, lam_init=0.0, name="mla_attn")
        merged = _merge(hb, o_a.reshape(n, -1), o_b.reshape(n, -1), o_c.reshape(n, -1),
                        w_branch_gate[i].astype(BF16), w_o_a[i].astype(BF16), w_o_b[i].astype(BF16),
                        w_o_c[i].astype(BF16))
        h, hb = _proj_ln(h, merged, w_out[i].astype(BF16), ln_g[i, 1], ln_b[i, 1], alpha)

        resid = _ple_resid(hb, ple_w_gate[i].astype(BF16), p[i].reshape(n, -1), ple_w_proj[i].astype(BF16), h, alpha)
        h, hb = _ffn_ln(resid, hb, w_gu[1], w_down[1], ln_g[i, 2], ln_b[i, 2], 1.0)
    return h.reshape(b, t, d)
```

```python
import functools
import math

import numpy as np
import jax
import jax.numpy as jnp
from jax import lax
from jax.experimental import pallas as pl
from jax.experimental.pallas import tpu as pltpu

F32 = jnp.float32
BF16 = jnp.bfloat16

DEPTH = 2
CHUNK = 64
N_BRANCHES = 3
DA_HEADS = 12
DA_QK_DIM = 64
DA_V_DIM = 128
GLA_HEADS = 4
GLA_DK = 128
GLA_DV = 256
GLA_GATE_RANK = 16
GLA_GATE_NORMALIZER = 16.0
MLA_HEADS = 12
MLA_Q_RANK = 768
MLA_KV_RANK = 512
MLA_NOPE_DIM = 128
MLA_ROPE_DIM = 64
MLA_V_DIM = 128
ROPE_THETA = 10000.0
LN_EPS = 1e-5
RMS_EPS = 1e-6
DEEPNORM_ALPHA = (2 * DEPTH) ** 0.25

A_QK_W = DA_HEADS * 2 * DA_QK_DIM
A_V_W = DA_HEADS * DA_V_DIM
B_K_W = GLA_HEADS * GLA_DK
B_V_W = GLA_HEADS * GLA_DV
C_V_W = MLA_HEADS * MLA_V_DIM
MLA_HEAD_PAD = 256
MLA_QK_W = MLA_HEADS * MLA_HEAD_PAD

ZF_CQA = 0
ZF_KPE = 768
ZF_BQ = 1024
ZF_BK = 1536
ZF_BV = 2048
ZF_BR = 3072
ZF_CKV = 4096
ZF_KPER = 4608
ZF_GLR = 4864
ZF_W = 5120

VMEM_CAP_V7X = 64 * 1024 * 1024
VMEM_LIMIT = VMEM_CAP_V7X - 8 * 1024 * 1024
VMEM_LIMIT_FFN = VMEM_CAP_V7X - 4 * 1024 * 1024
FFN_CHUNK = 512
NEG_BIG = -1e30
LOG2E = math.log2(math.e)


def _cparams(*sem, vmem=VMEM_LIMIT):
    return pltpu.CompilerParams(dimension_semantics=sem, vmem_limit_bytes=vmem)


def _dot(a, b):
    return jnp.dot(a, b, preferred_element_type=F32)


def _dot_nt(a, b):
    return lax.dot_general(a, b, (((1,), (1,)), ((), ())), preferred_element_type=F32)


def _dot_tn(a, b):
    return lax.dot_general(a, b, (((0,), (0,)), ((), ())), preferred_element_type=F32)


def _ln_rows(y, g, b):
    mu = jnp.mean(y, axis=-1, keepdims=True)
    yc = y - mu
    var = jnp.mean(yc * yc, axis=-1, keepdims=True)
    return yc * lax.rsqrt(var + LN_EPS) * g + b


def _rms_rows(y, g):
    return y * lax.rsqrt(jnp.mean(y * y, axis=-1, keepdims=True) + RMS_EPS) * g


def _ln_kernel(x_ref, g_ref, b_ref, o_ref, ob_ref):
    y = _ln_rows(x_ref[...], g_ref[...], b_ref[...])
    o_ref[...] = y
    ob_ref[...] = y.astype(BF16)


def _layer_norm(x, g, b):
    n, d = x.shape
    tr = min(256, n)
    row = pl.BlockSpec((tr, d), lambda i: (i, 0))
    vec = pl.BlockSpec((1, d), lambda i: (0, 0))
    return pl.pallas_call(
        _ln_kernel, grid=(n // tr,), in_specs=[row, vec, vec], out_specs=[row, row],
        out_shape=[jax.ShapeDtypeStruct((n, d), F32), jax.ShapeDtypeStruct((n, d), BF16)],
        compiler_params=_cparams("parallel"), name="ln_emb",
    )(x, g.reshape(1, d), b.reshape(1, d))


LN_SLAB = 64


def _resid_copy(resid_hbm, dst_ref, sem, block):
    tm = dst_ref.shape[0]
    start = pl.multiple_of(block * tm, tm)
    return pltpu.make_async_copy(resid_hbm.at[pl.ds(start, tm), :], dst_ref, sem)


def _ln_epilogue(o_ref, ob_ref, g_ref, b_ref, out_scale):
    g = g_ref[...]
    b = b_ref[...]

    def body(r, carry):
        sl = pl.ds(pl.multiple_of(r * LN_SLAB, LN_SLAB), LN_SLAB)
        y = _ln_rows(o_ref[sl, :] * out_scale, g, b)
        o_ref[sl, :] = y
        ob_ref[sl, :] = y.astype(BF16)
        return carry

    lax.fori_loop(0, o_ref.shape[0] // LN_SLAB, body, 0)


def _ffn_kernel(resid_hbm, hb_ref, wgu_ref, wd_ref, g_ref, b_ref, o_ref, ob_ref, sem, *, res_scale):
    f = pl.program_id(1)
    tf = wd_ref.shape[0]

    @pl.when(f == 0)
    def _():
        _resid_copy(resid_hbm, o_ref, sem, pl.program_id(0)).start()

    gu = _dot(hb_ref[...], wgu_ref[...])
    gate = gu[:, :tf]
    act = (gate * jax.nn.sigmoid(gate) * gu[:, tf:]).astype(BF16)

    @pl.when(f == 0)
    def _():
        _resid_copy(resid_hbm, o_ref, sem, pl.program_id(0)).wait()
        o_ref[...] = o_ref[...] * (2.0 * res_scale)

    o_ref[...] += _dot(act, wd_ref[...])

    @pl.when(f == pl.num_programs(1) - 1)
    def _():
        _ln_epilogue(o_ref, ob_ref, g_ref, b_ref, 0.5)


def _ffn_ln(resid, hb, w_gu, w_down, layer, half, ln_g, ln_b, res_scale):
    n, d = hb.shape
    nf, tf = w_gu.shape[2], w_gu.shape[4] // 2
    tm = min(512, n)
    vec = pl.BlockSpec((1, d), lambda i, f: (0, 0))
    row = pl.BlockSpec((tm, d), lambda i, f: (i, 0))
    return pl.pallas_call(
        functools.partial(_ffn_kernel, res_scale=res_scale),
        grid=(n // tm, nf),
        in_specs=[
            pl.BlockSpec(memory_space=pl.ANY),
            pl.BlockSpec((tm, d), lambda i, f: (i, 0), pipeline_mode=pl.Buffered(1)),
            pl.BlockSpec((None, None, None, d, 2 * tf), lambda i, f: (layer, half, f, 0, 0)),
            pl.BlockSpec((None, None, tf, d), lambda i, f: (layer, half, f, 0)),
            vec, vec,
        ],
        out_specs=[row, row],
        out_shape=[jax.ShapeDtypeStruct((n, d), F32), jax.ShapeDtypeStruct((n, d), BF16)],
        scratch_shapes=[pltpu.SemaphoreType.DMA(())],
        compiler_params=_cparams("parallel", "arbitrary", vmem=VMEM_LIMIT_FFN), name="ffn_ln",
    )(resid, hb, w_gu, w_down, ln_g.reshape(1, d), ln_b.reshape(1, d))


def _proj_ln_kernel(resid_hbm, x_ref, w_ref, g_ref, b_ref, o_ref, ob_ref, r_sc, sem, *, res_scale):
    i = pl.program_id(0)
    k = pl.program_id(1)

    @pl.when((i == 0) & (k == 0))
    def _():
        _resid_copy(resid_hbm, r_sc, sem, 0).start()

    @pl.when((k == 1) & (i + 1 < pl.num_programs(0)))
    def _():
        _resid_copy(resid_hbm, r_sc, sem, i + 1).start()

    @pl.when(k == 0)
    def _():
        _resid_copy(resid_hbm, r_sc, sem, i).wait()
        o_ref[...] = r_sc[...] * res_scale + _dot(x_ref[...], w_ref[...])

    @pl.when(k > 0)
    def _():
        o_ref[...] += _dot(x_ref[...], w_ref[...])

    @pl.when(k == pl.num_programs(1) - 1)
    def _():
        _ln_epilogue(o_ref, ob_ref, g_ref, b_ref, 1.0)


def _proj_ln(resid, x, w, layer, ln_g, ln_b, res_scale):
    n, kdim = x.shape
    d = w.shape[2]
    tm = min(512, n)
    tk = 512
    assert kdim // tk >= 2
    vec = pl.BlockSpec((1, d), lambda i, k: (0, 0))
    row = pl.BlockSpec((tm, d), lambda i, k: (i, 0))
    return pl.pallas_call(
        functools.partial(_proj_ln_kernel, res_scale=res_scale),
        grid=(n // tm, kdim // tk),
        in_specs=[
            pl.BlockSpec(memory_space=pl.ANY),
            pl.BlockSpec((tm, tk), lambda i, k: (i, k)),
            pl.BlockSpec((None, tk, d), lambda i, k: (layer, k, 0)),
            vec, vec,
        ],
        out_specs=[row, row],
        out_shape=[jax.ShapeDtypeStruct((n, d), F32), jax.ShapeDtypeStruct((n, d), BF16)],
        scratch_shapes=[pltpu.VMEM((tm, d), F32), pltpu.SemaphoreType.DMA(())],
        compiler_params=_cparams("arbitrary", "arbitrary"), name="proj_ln",
    )(resid, x, w, ln_g.reshape(1, d), ln_b.reshape(1, d))


def _mm_kernel(x_ref, w_ref, o_ref):
    o_ref[...] = _dot(x_ref[...], w_ref[...]).astype(o_ref.dtype)


def _matmul(x, w, layer, out_dtype, name):
    n, kdim = x.shape
    m = w.shape[2]
    tm = min(1024, n)
    tn = 512
    return pl.pallas_call(
        _mm_kernel, grid=(n // tm, m // tn),
        in_specs=[pl.BlockSpec((tm, kdim), lambda i, j: (i, 0)),
                  pl.BlockSpec((None, kdim, tn), lambda i, j: (layer, 0, j))],
        out_specs=pl.BlockSpec((tm, tn), lambda i, j: (i, j)),
        out_shape=jax.ShapeDtypeStruct((n, m), out_dtype),
        compiler_params=_cparams("parallel", "parallel"), name=name,
    )(x, w)


def _ple_kernel(hb_ref, wg_ref, p_ref, wp_ref, h_ref, o_ref, *, alpha):
    gate = jax.nn.sigmoid(_dot(hb_ref[...], wg_ref[...]))
    proj = _dot(p_ref[...].astype(BF16), wp_ref[...])
    o_ref[...] = alpha * h_ref[...] + gate * proj


def _ple_resid(hb, w_gate, p, w_proj, layer, h, alpha):
    n, d = hb.shape
    pd = p.shape[2]
    tm = min(1024, n)
    tn = 512
    tile = pl.BlockSpec((tm, tn), lambda i, j: (i, j))
    return pl.pallas_call(
        functools.partial(_ple_kernel, alpha=alpha), grid=(n // tm, d // tn),
        in_specs=[
            pl.BlockSpec((tm, d), lambda i, j: (i, 0)),
            pl.BlockSpec((None, d, tn), lambda i, j: (layer, 0, j)),
            pl.BlockSpec((None, tm, pd), lambda i, j: (layer, i, 0)),
            pl.BlockSpec((None, pd, tn), lambda i, j: (layer, 0, j)),
            tile,
        ],
        out_specs=tile,
        out_shape=jax.ShapeDtypeStruct((n, d), F32),
        compiler_params=_cparams("parallel", "parallel"), name="ple_resid",
    )(hb, w_gate, p, w_proj, h)


def _merge_kernel(hb_ref, oa_ref, ob_ref, oc_ref, g0_ref, g1_ref, g2_ref, wa_ref, wb_ref, wc_ref, o_ref):
    x = hb_ref[...]
    acc = jax.nn.sigmoid(_dot(x, g0_ref[...])) * _dot(oa_ref[...], wa_ref[...])
    acc += jax.nn.sigmoid(_dot(x, g1_ref[...])) * _dot(ob_ref[...], wb_ref[...])
    acc += jax.nn.sigmoid(_dot(x, g2_ref[...])) * _dot(oc_ref[...], wc_ref[...])
    o_ref[...] = acc.astype(o_ref.dtype)


def _merge(hb, o_a, o_b, o_c, w_bg, w_oa, w_ob, w_oc, layer):
    n, d = hb.shape
    tm = min(512, n)
    tn = 256
    nj = d // tn

    def rows(w):
        return pl.BlockSpec((tm, w), lambda i, j: (i, 0))

    def cols(kdim, off):
        return pl.BlockSpec((None, kdim, tn), lambda i, j: (layer, 0, off + j))

    return pl.pallas_call(
        _merge_kernel, grid=(n // tm, nj),
        in_specs=[rows(d), rows(o_a.shape[1]), rows(o_b.shape[1]), rows(o_c.shape[1]),
                  cols(d, 0), cols(d, nj), cols(d, 2 * nj),
                  cols(w_oa.shape[1], 0), cols(w_ob.shape[1], 0), cols(w_oc.shape[1], 0)],
        out_specs=pl.BlockSpec((tm, tn), lambda i, j: (i, j)),
        out_shape=jax.ShapeDtypeStruct((n, d), BF16),
        compiler_params=_cparams("parallel", "parallel"), name="merge",
    )(hb, o_a, o_b, o_c, w_bg, w_bg, w_bg, w_oa, w_ob, w_oc)


def _rope_tab_kernel(pos_ref, inv_ref, c_ref, s_ref):
    ang = pos_ref[...].astype(F32) * inv_ref[...]
    lane = lax.broadcasted_iota(jnp.int32, ang.shape, 1)
    rope = (lane >= MLA_NOPE_DIM) & (lane < MLA_NOPE_DIM + MLA_ROPE_DIM)
    c_ref[...] = jnp.where(lane < MLA_NOPE_DIM, 1.0, jnp.where(rope, jnp.cos(ang), 0.0))
    s_ref[...] = jnp.where(rope, jnp.sin(ang), 0.0)


def _rope_tables(positions):
    n = positions.size
    half = MLA_ROPE_DIM // 2
    inv = ROPE_THETA ** (-np.arange(half, dtype=np.float32) / half)
    inv_row = np.zeros((1, MLA_HEAD_PAD), np.float32)
    inv_row[0, MLA_NOPE_DIM:MLA_NOPE_DIM + MLA_ROPE_DIM] = np.concatenate([inv, inv])
    tr = min(512, n)
    tab = pl.BlockSpec((tr, MLA_HEAD_PAD), lambda i: (i, 0))
    return pl.pallas_call(
        _rope_tab_kernel, grid=(n // tr,),
        in_specs=[pl.BlockSpec((tr, 1), lambda i: (i, 0)), pl.BlockSpec((1, MLA_HEAD_PAD), lambda i: (0, 0))],
        out_specs=[tab, tab],
        out_shape=[jax.ShapeDtypeStruct((n, MLA_HEAD_PAD), F32)] * 2,
        compiler_params=_cparams("parallel"), name="rope_tables",
    )(positions.reshape(n, 1), jnp.asarray(inv_row))


def _mla_proj_kernel(cqa_ref, ckv_ref, kpe_ref, kper_ref, c_ref, s_ref, gq_ref, gkv_ref,
                     wq_ref, wqr_ref, wkn_ref, wv_ref, q_out, k_out, v_out):
    cos = c_ref[...]
    sin = s_ref[...]
    xq = _rms_rows(cqa_ref[...], gq_ref[...]).astype(BF16)
    q = _dot(xq, wq_ref[...])
    q_rot = _dot(xq, wqr_ref[...])
    xkv = _rms_rows(ckv_ref[...], gkv_ref[...]).astype(BF16)
    k_nope = _dot(xkv, wkn_ref[...])
    k_rope = kpe_ref[...] * cos + kper_ref[...] * sin
    for h in range(MLA_HEADS):
        sl = slice(h * MLA_HEAD_PAD, (h + 1) * MLA_HEAD_PAD)
        q_out[:, sl] = (q[:, sl] * cos + q_rot[:, sl] * sin).astype(BF16)
        k_out[:, sl] = (k_nope[:, sl] + k_rope).astype(BF16)
    v_out[...] = _dot(xkv, wv_ref[...]).astype(BF16)


def _mla_proj(z_f, cos_tab, sin_tab, g_q, g_kv, w_q, w_qrot, w_kn, w_v, layer):
    n = z_f.shape[0]
    tm = min(256, n)

    def zcols(width, off):
        return pl.BlockSpec((tm, width), lambda i: (i, off // width))

    def whole(a):
        return pl.BlockSpec((None,) + a.shape[1:], lambda i: (layer, 0, 0))

    def rows(width):
        return pl.BlockSpec((tm, width), lambda i: (i, 0))

    g_q = g_q.reshape(g_q.shape[0], 1, -1)
    g_kv = g_kv.reshape(g_kv.shape[0], 1, -1)
    return pl.pallas_call(
        _mla_proj_kernel, grid=(n // tm,),
        in_specs=[zcols(MLA_Q_RANK, ZF_CQA), zcols(MLA_KV_RANK, ZF_CKV), zcols(MLA_HEAD_PAD, ZF_KPE),
                  zcols(MLA_HEAD_PAD, ZF_KPER), rows(MLA_HEAD_PAD), rows(MLA_HEAD_PAD),
                  whole(g_q), whole(g_kv), whole(w_q), whole(w_qrot), whole(w_kn), whole(w_v)],
        out_specs=[rows(MLA_QK_W), rows(MLA_QK_W), rows(C_V_W)],
        out_shape=[jax.ShapeDtypeStruct((n, MLA_QK_W), BF16), jax.ShapeDtypeStruct((n, MLA_QK_W), BF16),
                   jax.ShapeDtypeStruct((n, C_V_W), BF16)],
        compiler_params=_cparams("parallel"), name="mla_proj",
    )(z_f, z_f, z_f, z_f, cos_tab, sin_tab, g_q, g_kv, w_q, w_qrot, w_kn, w_v)


ATTN_TILE = 512


def _attn_kernel(slopes_ref, lam_ref, q_ref, k_ref, v_ref, g_ref, o_ref, vt_sc, bias_sc, m_sc, l_sc, acc_sc,
                 *, diff, score_scale, lam_init):
    t = ATTN_TILE
    qi = pl.program_id(2)
    nt = k_ref.shape[1] // t
    c1 = score_scale * LOG2E
    if diff:
        slope2 = slopes_ref[pl.program_id(1)] * LOG2E

    @pl.when(qi == 0)
    def _():
        for j in range(nt):
            vt_sc[j] = v_ref[0, j * t:(j + 1) * t, :].astype(F32).T.astype(BF16)
        key = lax.broadcasted_iota(jnp.int32, (t, t), 0)
        qry = lax.broadcasted_iota(jnp.int32, (t, t), 1)
        allowed = (key // CHUNK) <= (qry // CHUNK)
        if diff:
            bias_sc[0] = slope2 * (qry - key).astype(F32)
            bias_sc[1] = jnp.where(allowed, slope2 * jnp.abs(qry - key).astype(F32), -NEG_BIG)
        else:
            bias_sc[0] = jnp.where(allowed, 0.0, -NEG_BIG)

    qt = q_ref[0].astype(F32).T
    if diff:
        feat = lax.broadcasted_iota(jnp.int32, qt.shape, 0)
        qt = qt * (DA_QK_DIM ** -0.5)
        qts = [jnp.where(feat < DA_QK_DIM, qt, 0.0).astype(BF16), jnp.where(feat >= DA_QK_DIM, qt, 0.0).astype(BF16)]
    else:
        qts = [qt.astype(BF16)]
    nmap = len(qts)

    m_sc[...] = jnp.full(m_sc.shape, NEG_BIG, F32)
    l_sc[...] = jnp.zeros(l_sc.shape, F32)
    acc_sc[...] = jnp.zeros(acc_sc.shape, F32)

    def step(kj, diag):
        k = k_ref[0, pl.ds(pl.multiple_of(kj * t, t), t), :]
        vt = vt_sc[kj]
        if diff:
            bias = bias_sc[1] if diag else bias_sc[0] + ((qi - kj) * t).astype(F32) * slope2
        else:
            bias = bias_sc[0] if diag else None
        for i in range(nmap):
            s = _dot(k, qts[i]) * c1
            if bias is not None:
                s = s - bias
            m_prev = m_sc[i]
            m_new = jnp.maximum(m_prev, jnp.max(s, axis=0, keepdims=True))
            alpha = jnp.exp2(m_prev - m_new)
            p = jnp.exp2(s - m_new)
            l_sc[i] = alpha * l_sc[i] + jnp.sum(p, axis=0, keepdims=True)
            acc_sc[i] = alpha * acc_sc[i] + _dot(vt, p.astype(BF16))
            m_sc[i] = m_new

    def full_step(kj, carry):
        step(kj, False)
        return carry

    lax.fori_loop(0, qi, full_step, 0)
    step(qi, True)

    o = acc_sc[0] / l_sc[0]
    if diff:
        lp = lam_ref[...]
        lam = (jnp.exp(jnp.sum(lp[0:1] * lp[1:2], axis=1, keepdims=True))
               - jnp.exp(jnp.sum(lp[2:3] * lp[3:4], axis=1, keepdims=True)) + lam_init)
        o = o - lam * (acc_sc[1] / l_sc[1])
        o = o * lax.rsqrt(jnp.mean(o * o, axis=0, keepdims=True) + RMS_EPS) * g_ref[...] * (1.0 - lam_init)
    o_ref[0] = o.T.astype(o_ref.dtype)


def _attention(q_arr, k_arr, v_arr, q_off, k_off, v_off, dqk, dv, heads, *, diff, score_scale,
               slopes, lam_params, gain_col, lam_init, name):
    b, t, _ = q_arr.shape
    tq = ATTN_TILE
    nmap = 2 if diff else 1
    return pl.pallas_call(
        functools.partial(_attn_kernel, diff=diff, score_scale=score_scale, lam_init=lam_init),
        grid=(b, heads, t // tq),
        in_specs=[
            pl.BlockSpec(memory_space=pltpu.SMEM),
            pl.BlockSpec(lam_params.shape, lambda bi, h, i: (0, 0)),
            pl.BlockSpec((1, tq, dqk), lambda bi, h, i: (bi, i, q_off + h)),
            pl.BlockSpec((1, t, dqk), lambda bi, h, i: (bi, 0, k_off + h)),
            pl.BlockSpec((1, t, dv), lambda bi, h, i: (bi, 0, v_off + h)),
            pl.BlockSpec(gain_col.shape, lambda bi, h, i: (0, 0)),
        ],
        out_specs=pl.BlockSpec((1, tq, dv), lambda bi, h, i: (bi, i, h)),
        out_shape=jax.ShapeDtypeStruct((b, t, heads * dv), BF16),
        scratch_shapes=[pltpu.VMEM((t // tq, dv, tq), BF16), pltpu.VMEM((nmap, tq, tq), F32),
                        pltpu.VMEM((nmap, 1, tq), F32), pltpu.VMEM((nmap, 1, tq), F32),
                        pltpu.VMEM((nmap, dv, tq), F32)],
        compiler_params=_cparams("parallel", "parallel", "arbitrary"), name=name,
    )(slopes, lam_params, q_arr, k_arr, v_arr, gain_col)


def _alibi_slopes(n):
    def pow2_slopes(m):
        start = 2.0 ** (-8.0 / m)
        return [start ** (i + 1) for i in range(m)]
    c = 2 ** int(math.floor(math.log2(n)))
    s = pow2_slopes(c)
    if c < n:
        s = s + pow2_slopes(2 * c)[0::2][: n - c]
    return np.array(s, dtype=np.float32)


def _gla_kernel(q_ref, k_ref, v_ref, r_ref, glr_ref, w2_ref, bg_ref, ng_ref, o_ref, st_sc):
    c = CHUNK
    scale = GLA_DK ** -0.5
    ri = lax.broadcasted_iota(jnp.int32, (c, c), 0)
    ci = lax.broadcasted_iota(jnp.int32, (c, c), 1)
    causal = ci <= ri
    tri = causal.astype(BF16)
    w2 = w2_ref[...]
    bg = bg_ref[...]
    ng = ng_ref[...]
    st_sc[...] = jnp.zeros(st_sc.shape, F32)

    def body(n, carry):
        sl = pl.ds(pl.multiple_of(n * c, c), c)
        pre = _dot(glr_ref[0, sl, :].astype(BF16), w2) + bg
        log_a = (jnp.minimum(pre, 0.0) - jnp.log1p(jnp.exp(-jnp.abs(pre)))) / GLA_GATE_NORMALIZER
        hi = log_a.astype(BF16)
        rem = log_a - hi.astype(F32)
        mid = rem.astype(BF16)
        lo = (rem - mid.astype(F32)).astype(BF16)
        bcum = _dot(tri, hi) + _dot(tri, mid) + _dot(tri, lo)
        b_mid = bcum[c // 2:c // 2 + 1, :]
        b_last = bcum[c - 1:c, :]
        q = q_ref[0, sl, :] * scale
        k = k_ref[0, sl, :]
        v = v_ref[0, sl, :].astype(BF16)
        att = _dot_nt((q * jnp.exp(bcum - b_mid)).astype(BF16), (k * jnp.exp(b_mid - bcum)).astype(BF16))
        att = jnp.where(causal, att, 0.0)
        state_t = st_sc[...]
        o = _dot(att.astype(BF16), v) + _dot_nt((q * jnp.exp(bcum)).astype(BF16), state_t.astype(BF16))
        upd_t = _dot_tn(v, (k * jnp.exp(b_last - bcum)).astype(BF16))
        st_sc[...] = state_t * jnp.exp(b_last) + upd_t
        r = r_ref[0, sl, :]
        o_ref[0, sl, :] = (_rms_rows(o, ng) * (r * jax.nn.sigmoid(r))).astype(o_ref.dtype)
        return carry

    lax.fori_loop(0, q_ref.shape[1] // c, body, 0)


def _gla(z_f3, w_gate2_pad, b_gate, norm_g, layer):
    b, t, _ = z_f3.shape

    def zcols(width, off):
        return pl.BlockSpec((1, t, width), lambda bi, h: (bi, 0, off // width + h))

    return pl.pallas_call(
        _gla_kernel, grid=(b, GLA_HEADS),
        in_specs=[zcols(GLA_DK, ZF_BQ), zcols(GLA_DK, ZF_BK), zcols(GLA_DV, ZF_BV), zcols(GLA_DV, ZF_BR),
                  pl.BlockSpec((1, t, 128), lambda bi, h: (bi, 0, ZF_GLR // 128)),
                  pl.BlockSpec((None, 128, GLA_DK), lambda bi, h: (layer, 0, h)),
                  pl.BlockSpec((None, 1, GLA_DK), lambda bi, h: (layer, 0, h)),
                  pl.BlockSpec((None, 1, GLA_DV), lambda bi, h: (layer, 0, 0))],
        out_specs=pl.BlockSpec((1, t, GLA_DV), lambda bi, h: (bi, 0, h)),
        out_shape=jax.ShapeDtypeStruct((b, t, B_V_W), BF16),
        scratch_shapes=[pltpu.VMEM((GLA_DV, GLA_DK), F32)],
        compiler_params=_cparams("parallel", "parallel"), name="gla",
    )(z_f3, z_f3, z_f3, z_f3, z_f3, w_gate2_pad, b_gate.reshape(b_gate.shape[0], 1, -1),
      norm_g.reshape(norm_g.shape[0], 1, -1))


def _rot_half_cols(w):
    half = w.shape[-1] // 2
    return jnp.concatenate([-w[..., half:], w[..., :half]], axis=-1)


def _prep_in_proj(w_in):
    lead = w_in.shape[:-1]
    offs = np.cumsum((A_QK_W, A_QK_W, A_V_W, B_K_W, B_K_W, B_V_W, GLA_GATE_RANK, B_V_W, MLA_Q_RANK,
                      MLA_KV_RANK + MLA_ROPE_DIM))
    a_end = offs[2]
    b_q, b_k, b_v, b_glr, b_r, c_qa, c_kva = [w_in[..., offs[i]:offs[i + 1]] for i in range(2, 9)]
    c_kv, k_pe = c_kva[..., :MLA_KV_RANK], c_kva[..., MLA_KV_RANK:]

    def zeros(width):
        return jnp.zeros(lead + (width,), w_in.dtype)

    def in_rope_lanes(w):
        return jnp.concatenate([zeros(MLA_NOPE_DIM), w, zeros(MLA_HEAD_PAD - MLA_NOPE_DIM - MLA_ROPE_DIM)], axis=-1)

    w_f = jnp.concatenate([c_qa, in_rope_lanes(k_pe), b_q, b_k, b_v, b_r, c_kv, in_rope_lanes(_rot_half_cols(k_pe)),
                           b_glr, zeros(ZF_W - ZF_GLR - GLA_GATE_RANK)], axis=-1)
    assert w_f.shape[-1] == ZF_W
    return w_in[..., :a_end].astype(BF16), w_f.astype(BF16)


def _prep_mla(w_qb, w_kvb):
    lead = w_qb.shape[:-1]
    wq = w_qb.reshape(lead + (MLA_HEADS, MLA_NOPE_DIM + MLA_ROPE_DIM))
    nope, rope = wq[..., :MLA_NOPE_DIM], wq[..., MLA_NOPE_DIM:]
    pad = jnp.zeros(lead + (MLA_HEADS, MLA_HEAD_PAD - MLA_NOPE_DIM - MLA_ROPE_DIM), w_qb.dtype)
    w_q = jnp.concatenate([nope, rope, pad], axis=-1).reshape(lead + (MLA_QK_W,))
    w_qrot = jnp.concatenate([jnp.zeros_like(nope), _rot_half_cols(rope), pad], axis=-1).reshape(lead + (MLA_QK_W,))
    lead = w_kvb.shape[:-1]
    wkv = w_kvb.reshape(lead + (MLA_HEADS, MLA_NOPE_DIM + MLA_V_DIM))
    k_nope, v = wkv[..., :MLA_NOPE_DIM], wkv[..., MLA_NOPE_DIM:]
    w_kn = jnp.concatenate([k_nope, jnp.zeros(lead + (MLA_HEADS, MLA_HEAD_PAD - MLA_NOPE_DIM), w_kvb.dtype)],
                           axis=-1).reshape(lead + (MLA_QK_W,))
    return w_q.astype(BF16), w_qrot.astype(BF16), w_kn.astype(BF16), v.reshape(lead + (C_V_W,)).astype(BF16)


def _prep_ffn(w_gu, w_down, tf):
    nl, two, d, f2 = w_gu.shape
    ff = f2 // 2
    nf = -(-ff // tf)
    padw = nf * tf - ff

    def chunks(w):
        w = jnp.pad(w.astype(BF16), ((0, 0), (0, 0), (0, 0), (0, padw)))
        return w.reshape(nl, two, d, nf, 1, tf)

    gu = jnp.concatenate([chunks(w_gu[..., :ff]), chunks(w_gu[..., ff:])], axis=4)
    gu = gu.transpose(0, 1, 3, 2, 4, 5).reshape(nl, two, nf, d, 2 * tf)
    wd = jnp.pad(w_down.astype(BF16), ((0, 0), (0, 0), (0, padw), (0, 0)))
    return gu, wd


def kernel(x, p, positions, emb_ln_g, emb_ln_b, w_in, w_branch_gate, da_lambda_q1, da_lambda_k1, da_lambda_q2,
           da_lambda_k2, da_subln_g, gla_w_gate2, gla_b_gate, gla_norm_g, mla_q_norm_g, mla_w_qb, mla_kv_norm_g,
           mla_w_kvb, w_o_a, w_o_b, w_o_c, w_out, ffn_w_gu, ffn_w_down, ln_g, ln_b, ple_w_proj, ple_w_gate):
    b, t, d = x.shape
    n = b * t
    depth = w_in.shape[0]
    alpha = float(DEEPNORM_ALPHA)

    w_gu, w_down = _prep_ffn(ffn_w_gu, ffn_w_down, FFN_CHUNK)
    w_a, w_f = _prep_in_proj(w_in)
    w_q, w_qrot, w_kn, w_v = _prep_mla(mla_w_qb, mla_w_kvb)
    w_bg = w_branch_gate.astype(BF16)
    w_oa, w_ob, w_oc, w_o = (w.astype(BF16) for w in (w_o_a, w_o_b, w_o_c, w_out))
    w_pg, w_pp = ple_w_gate.astype(BF16), ple_w_proj.astype(BF16)
    w2_pad = jnp.pad(gla_w_gate2.astype(BF16), ((0, 0), (0, 128 - GLA_GATE_RANK), (0, 0)))
    p = p.reshape(depth, n, -1)

    cos_tab, sin_tab = _rope_tables(positions)
    slopes = jnp.asarray(_alibi_slopes(DA_HEADS))
    no_slopes = jnp.zeros((1,), F32)
    no_lam = jnp.zeros((4, DA_QK_DIM), F32)
    no_gain = jnp.ones((MLA_V_DIM, 1), F32)

    h, hb = _layer_norm(x.reshape(n, d), emb_ln_g, emb_ln_b)
    for i in range(depth):
        h, hb = _ffn_ln(h, hb, w_gu, w_down, i, 0, ln_g[i, 0], ln_b[i, 0], alpha)

        z_a = _matmul(hb, w_a, i, BF16, "in_proj_a").reshape(b, t, -1)
        z_f = _matmul(hb, w_f, i, F32, "in_proj_f")
        lam_init = 0.8 - 0.6 * math.exp(-0.3 * i)
        lam_params = jnp.stack([da_lambda_q1[i], da_lambda_k1[i], da_lambda_q2[i], da_lambda_k2[i]]).astype(F32)
        o_a = _attention(z_a, z_a, z_a, 0, DA_HEADS, 2 * DA_HEADS, 2 * DA_QK_DIM, DA_V_DIM, DA_HEADS,
                         diff=True, score_scale=1.0, slopes=slopes, lam_params=lam_params,
                         gain_col=da_subln_g[i].reshape(-1, 1), lam_init=lam_init, name="diff_attn")
        o_b = _gla(z_f.reshape(b, t, ZF_W), w2_pad, gla_b_gate, gla_norm_g, i)
        q_c, k_c, v_c = _mla_proj(z_f, cos_tab, sin_tab, mla_q_norm_g, mla_kv_norm_g, w_q, w_qrot, w_kn, w_v, i)
        o_c = _attention(q_c.reshape(b, t, -1), k_c.reshape(b, t, -1), v_c.reshape(b, t, -1), 0, 0, 0,
                         MLA_HEAD_PAD, MLA_V_DIM, MLA_HEADS, diff=False,
                         score_scale=(MLA_NOPE_DIM + MLA_ROPE_DIM) ** -0.5, slopes=no_slopes, lam_params=no_lam,
                         gain_col=no_gain, lam_init=0.0, name="mla_attn")
        merged = _merge(hb, o_a.reshape(n, -1), o_b.reshape(n, -1), o_c.reshape(n, -1), w_bg, w_oa, w_ob, w_oc, i)
        h, hb = _proj_ln(h, merged, w_o, i, ln_g[i, 1], ln_b[i, 1], alpha)

        resid = _ple_resid(hb, w_pg, p, w_pp, i, h, alpha)
        h, hb = _ffn_ln(resid, hb, w_gu, w_down, i, 1, ln_g[i, 2], ln_b[i, 2], 1.0)
    return h.reshape(b, t, d)
```

```python
import functools
import math

import numpy as np
import jax
import jax.numpy as jnp
from jax import lax
from jax.experimental import pallas as pl
from jax.experimental.pallas import tpu as pltpu

F32 = jnp.float32
BF16 = jnp.bfloat16

DEPTH = 2
CHUNK = 64
N_BRANCHES = 3
DA_HEADS = 12
DA_QK_DIM = 64
DA_V_DIM = 128
GLA_HEADS = 4
GLA_DK = 128
GLA_DV = 256
GLA_GATE_RANK = 16
GLA_GATE_NORMALIZER = 16.0
MLA_HEADS = 12
MLA_Q_RANK = 768
MLA_KV_RANK = 512
MLA_NOPE_DIM = 128
MLA_ROPE_DIM = 64
MLA_V_DIM = 128
ROPE_THETA = 10000.0
LN_EPS = 1e-5
RMS_EPS = 1e-6
DEEPNORM_ALPHA = (2 * DEPTH) ** 0.25

A_QK_W = DA_HEADS * 2 * DA_QK_DIM
A_V_W = DA_HEADS * DA_V_DIM
B_K_W = GLA_HEADS * GLA_DK
B_V_W = GLA_HEADS * GLA_DV
C_V_W = MLA_HEADS * MLA_V_DIM
MLA_HEAD_PAD = 256
MLA_QK_W = MLA_HEADS * MLA_HEAD_PAD

ZF_CQA = 0
ZF_KPE = 768
ZF_BQ = 1024
ZF_BK = 1536
ZF_BV = 2048
ZF_BR = 3072
ZF_CKV = 4096
ZF_KPER = 4608
ZF_GLR = 4864
ZF_W = 5120

VMEM_CAP_V7X = 64 * 1024 * 1024
VMEM_LIMIT = VMEM_CAP_V7X - 8 * 1024 * 1024
VMEM_LIMIT_FFN = VMEM_CAP_V7X - 4 * 1024 * 1024
FFN_CHUNK = 256
NEG_BIG = -1e30
LOG2E = math.log2(math.e)


def _cparams(*sem, vmem=VMEM_LIMIT):
    return pltpu.CompilerParams(dimension_semantics=sem, vmem_limit_bytes=vmem)


def _dot(a, b):
    return jnp.dot(a, b, preferred_element_type=F32)


def _dot_nt(a, b):
    return lax.dot_general(a, b, (((1,), (1,)), ((), ())), preferred_element_type=F32)


def _dot_tn(a, b):
    return lax.dot_general(a, b, (((0,), (0,)), ((), ())), preferred_element_type=F32)


def _ln_rows(y, g, b):
    mu = jnp.mean(y, axis=-1, keepdims=True)
    yc = y - mu
    var = jnp.mean(yc * yc, axis=-1, keepdims=True)
    return yc * lax.rsqrt(var + LN_EPS) * g + b


def _rms_rows(y, g):
    return y * lax.rsqrt(jnp.mean(y * y, axis=-1, keepdims=True) + RMS_EPS) * g


def _ln_kernel(x_ref, g_ref, b_ref, o_ref, ob_ref):
    y = _ln_rows(x_ref[...], g_ref[...], b_ref[...])
    o_ref[...] = y
    ob_ref[...] = y.astype(BF16)


def _layer_norm(x, g, b):
    n, d = x.shape
    tr = min(256, n)
    row = pl.BlockSpec((tr, d), lambda i: (i, 0))
    vec = pl.BlockSpec((1, d), lambda i: (0, 0))
    return pl.pallas_call(
        _ln_kernel, grid=(n // tr,), in_specs=[row, vec, vec], out_specs=[row, row],
        out_shape=[jax.ShapeDtypeStruct((n, d), F32), jax.ShapeDtypeStruct((n, d), BF16)],
        compiler_params=_cparams("parallel"), name="ln_emb",
    )(x, g.reshape(1, d), b.reshape(1, d))


LN_SLAB = 64


def _resid_copy(resid_hbm, dst_ref, sem, block):
    tm = dst_ref.shape[0]
    start = pl.multiple_of(block * tm, tm)
    return pltpu.make_async_copy(resid_hbm.at[pl.ds(start, tm), :], dst_ref, sem)


def _ln_epilogue(o_ref, ob_ref, g_ref, b_ref, out_scale):
    g = g_ref[...]
    b = b_ref[...]

    def body(r, carry):
        sl = pl.ds(pl.multiple_of(r * LN_SLAB, LN_SLAB), LN_SLAB)
        y = _ln_rows(o_ref[sl, :] * out_scale, g, b)
        o_ref[sl, :] = y
        ob_ref[sl, :] = y.astype(BF16)
        return carry

    lax.fori_loop(0, o_ref.shape[0] // LN_SLAB, body, 0)


def _ffn_kernel(resid_hbm, hb_ref, wgu_ref, wd_ref, g_ref, b_ref, o_ref, ob_ref, sem, *, res_scale):
    f = pl.program_id(1)
    tf = wd_ref.shape[0]

    @pl.when(f == 0)
    def _():
        _resid_copy(resid_hbm, o_ref, sem, pl.program_id(0)).start()

    gu = _dot(hb_ref[...], wgu_ref[...])
    gate = gu[:, :tf]
    act = (gate * jax.nn.sigmoid(gate) * gu[:, tf:]).astype(BF16)

    @pl.when(f == 0)
    def _():
        _resid_copy(resid_hbm, o_ref, sem, pl.program_id(0)).wait()
        o_ref[...] = o_ref[...] * (2.0 * res_scale)

    o_ref[...] += _dot(act, wd_ref[...])

    @pl.when(f == pl.num_programs(1) - 1)
    def _():
        _ln_epilogue(o_ref, ob_ref, g_ref, b_ref, 0.5)


def _ffn_ln(resid, hb, w_gu, w_down, layer, half, ln_g, ln_b, res_scale):
    n, d = hb.shape
    nf, tf = w_gu.shape[2], w_gu.shape[4] // 2
    tm = min(1024, n)
    vec = pl.BlockSpec((1, d), lambda i, f: (0, 0))
    row = pl.BlockSpec((tm, d), lambda i, f: (i, 0), pipeline_mode=pl.Buffered(1))
    return pl.pallas_call(
        functools.partial(_ffn_kernel, res_scale=res_scale),
        grid=(n // tm, nf),
        in_specs=[
            pl.BlockSpec(memory_space=pl.ANY),
            pl.BlockSpec((tm, d), lambda i, f: (i, 0)),
            pl.BlockSpec((None, None, None, d, 2 * tf), lambda i, f: (layer, half, f, 0, 0)),
            pl.BlockSpec((None, None, tf, d), lambda i, f: (layer, half, f, 0)),
            vec, vec,
        ],
        out_specs=[row, row],
        out_shape=[jax.ShapeDtypeStruct((n, d), F32), jax.ShapeDtypeStruct((n, d), BF16)],
        scratch_shapes=[pltpu.SemaphoreType.DMA(())],
        compiler_params=_cparams("parallel", "arbitrary", vmem=VMEM_LIMIT_FFN), name="ffn_ln",
    )(resid, hb, w_gu, w_down, ln_g.reshape(1, d), ln_b.reshape(1, d))


def _proj_ln_kernel(resid_hbm, x_ref, w_ref, g_ref, b_ref, o_ref, ob_ref, r_sc, sem, *, res_scale):
    i = pl.program_id(0)
    k = pl.program_id(1)

    @pl.when((i == 0) & (k == 0))
    def _():
        _resid_copy(resid_hbm, r_sc, sem, 0).start()

    @pl.when((k == 1) & (i + 1 < pl.num_programs(0)))
    def _():
        _resid_copy(resid_hbm, r_sc, sem, i + 1).start()

    @pl.when(k == 0)
    def _():
        _resid_copy(resid_hbm, r_sc, sem, i).wait()
        o_ref[...] = r_sc[...] * res_scale + _dot(x_ref[...], w_ref[...])

    @pl.when(k > 0)
    def _():
        o_ref[...] += _dot(x_ref[...], w_ref[...])

    @pl.when(k == pl.num_programs(1) - 1)
    def _():
        _ln_epilogue(o_ref, ob_ref, g_ref, b_ref, 1.0)


def _proj_ln(resid, x, w, layer, ln_g, ln_b, res_scale):
    n, kdim = x.shape
    d = w.shape[2]
    tm = min(512, n)
    tk = 512
    assert kdim // tk >= 2
    vec = pl.BlockSpec((1, d), lambda i, k: (0, 0))
    row = pl.BlockSpec((tm, d), lambda i, k: (i, 0))
    return pl.pallas_call(
        functools.partial(_proj_ln_kernel, res_scale=res_scale),
        grid=(n // tm, kdim // tk),
        in_specs=[
            pl.BlockSpec(memory_space=pl.ANY),
            pl.BlockSpec((tm, tk), lambda i, k: (i, k)),
            pl.BlockSpec((None, tk, d), lambda i, k: (layer, k, 0)),
            vec, vec,
        ],
        out_specs=[row, row],
        out_shape=[jax.ShapeDtypeStruct((n, d), F32), jax.ShapeDtypeStruct((n, d), BF16)],
        scratch_shapes=[pltpu.VMEM((tm, d), F32), pltpu.SemaphoreType.DMA(())],
        compiler_params=_cparams("arbitrary", "arbitrary"), name="proj_ln",
    )(resid, x, w, ln_g.reshape(1, d), ln_b.reshape(1, d))


def _mm_kernel(x_ref, w_ref, o_ref):
    o_ref[...] = _dot(x_ref[...], w_ref[...]).astype(o_ref.dtype)


def _matmul(x, w, layer, out_dtype, name):
    n, kdim = x.shape
    m = w.shape[2]
    tm = min(1024, n)
    tn = 512
    return pl.pallas_call(
        _mm_kernel, grid=(n // tm, m // tn),
        in_specs=[pl.BlockSpec((tm, kdim), lambda i, j: (i, 0)),
                  pl.BlockSpec((None, kdim, tn), lambda i, j: (layer, 0, j))],
        out_specs=pl.BlockSpec((tm, tn), lambda i, j: (i, j)),
        out_shape=jax.ShapeDtypeStruct((n, m), out_dtype),
        compiler_params=_cparams("parallel", "parallel"), name=name,
    )(x, w)


def _ple_kernel(hb_ref, wg_ref, p_ref, wp_ref, h_ref, o_ref, *, alpha):
    gate = jax.nn.sigmoid(_dot(hb_ref[...], wg_ref[...]))
    proj = _dot(p_ref[...].astype(BF16), wp_ref[...])
    o_ref[...] = alpha * h_ref[...] + gate * proj


def _ple_resid(hb, w_gate, p, w_proj, layer, h, alpha):
    n, d = hb.shape
    pd = p.shape[2]
    tm = min(1024, n)
    tn = 512
    tile = pl.BlockSpec((tm, tn), lambda i, j: (i, j))
    return pl.pallas_call(
        functools.partial(_ple_kernel, alpha=alpha), grid=(n // tm, d // tn),
        in_specs=[
            pl.BlockSpec((tm, d), lambda i, j: (i, 0)),
            pl.BlockSpec((None, d, tn), lambda i, j: (layer, 0, j)),
            pl.BlockSpec((None, tm, pd), lambda i, j: (layer, i, 0)),
            pl.BlockSpec((None, pd, tn), lambda i, j: (layer, 0, j)),
            tile,
        ],
        out_specs=tile,
        out_shape=jax.ShapeDtypeStruct((n, d), F32),
        compiler_params=_cparams("parallel", "parallel"), name="ple_resid",
    )(hb, w_gate, p, w_proj, h)


def _merge_kernel(hb_ref, oa_ref, ob_ref, oc_ref, g0_ref, g1_ref, g2_ref, wa_ref, wb_ref, wc_ref, o_ref):
    x = hb_ref[...]
    acc = jax.nn.sigmoid(_dot(x, g0_ref[...])) * _dot(oa_ref[...], wa_ref[...])
    acc += jax.nn.sigmoid(_dot(x, g1_ref[...])) * _dot(ob_ref[...], wb_ref[...])
    acc += jax.nn.sigmoid(_dot(x, g2_ref[...])) * _dot(oc_ref[...], wc_ref[...])
    o_ref[...] = acc.astype(o_ref.dtype)


def _merge(hb, o_a, o_b, o_c, w_bg, w_oa, w_ob, w_oc, layer):
    n, d = hb.shape
    tm = min(512, n)
    tn = 256
    nj = d // tn

    def rows(w):
        return pl.BlockSpec((tm, w), lambda i, j: (i, 0))

    def cols(kdim, off):
        return pl.BlockSpec((None, kdim, tn), lambda i, j: (layer, 0, off + j))

    return pl.pallas_call(
        _merge_kernel, grid=(n // tm, nj),
        in_specs=[rows(d), rows(o_a.shape[1]), rows(o_b.shape[1]), rows(o_c.shape[1]),
                  cols(d, 0), cols(d, nj), cols(d, 2 * nj),
                  cols(w_oa.shape[1], 0), cols(w_ob.shape[1], 0), cols(w_oc.shape[1], 0)],
        out_specs=pl.BlockSpec((tm, tn), lambda i, j: (i, j)),
        out_shape=jax.ShapeDtypeStruct((n, d), BF16),
        compiler_params=_cparams("parallel", "parallel"), name="merge",
    )(hb, o_a, o_b, o_c, w_bg, w_bg, w_bg, w_oa, w_ob, w_oc)


def _rope_tab_kernel(pos_ref, inv_ref, c_ref, s_ref):
    ang = pos_ref[...].astype(F32) * inv_ref[...]
    lane = lax.broadcasted_iota(jnp.int32, ang.shape, 1)
    rope = (lane >= MLA_NOPE_DIM) & (lane < MLA_NOPE_DIM + MLA_ROPE_DIM)
    c_ref[...] = jnp.where(lane < MLA_NOPE_DIM, 1.0, jnp.where(rope, jnp.cos(ang), 0.0))
    s_ref[...] = jnp.where(rope, jnp.sin(ang), 0.0)


def _rope_tables(positions):
    n = positions.size
    half = MLA_ROPE_DIM // 2
    inv = ROPE_THETA ** (-np.arange(half, dtype=np.float32) / half)
    inv_row = np.zeros((1, MLA_HEAD_PAD), np.float32)
    inv_row[0, MLA_NOPE_DIM:MLA_NOPE_DIM + MLA_ROPE_DIM] = np.concatenate([inv, inv])
    tr = min(512, n)
    tab = pl.BlockSpec((tr, MLA_HEAD_PAD), lambda i: (i, 0))
    return pl.pallas_call(
        _rope_tab_kernel, grid=(n // tr,),
        in_specs=[pl.BlockSpec((tr, 1), lambda i: (i, 0)), pl.BlockSpec((1, MLA_HEAD_PAD), lambda i: (0, 0))],
        out_specs=[tab, tab],
        out_shape=[jax.ShapeDtypeStruct((n, MLA_HEAD_PAD), F32)] * 2,
        compiler_params=_cparams("parallel"), name="rope_tables",
    )(positions.reshape(n, 1), jnp.asarray(inv_row))


def _mla_proj_kernel(cqa_ref, ckv_ref, kpe_ref, kper_ref, c_ref, s_ref, gq_ref, gkv_ref,
                     wq_ref, wqr_ref, wkn_ref, wv_ref, q_out, k_out, v_out):
    cos = c_ref[...]
    sin = s_ref[...]
    xq = _rms_rows(cqa_ref[...], gq_ref[...]).astype(BF16)
    q = _dot(xq, wq_ref[...])
    q_rot = _dot(xq, wqr_ref[...])
    xkv = _rms_rows(ckv_ref[...], gkv_ref[...]).astype(BF16)
    k_nope = _dot(xkv, wkn_ref[...])
    k_rope = kpe_ref[...] * cos + kper_ref[...] * sin
    for h in range(MLA_HEADS):
        sl = slice(h * MLA_HEAD_PAD, (h + 1) * MLA_HEAD_PAD)
        q_out[:, sl] = (q[:, sl] * cos + q_rot[:, sl] * sin).astype(BF16)
        k_out[:, sl] = (k_nope[:, sl] + k_rope).astype(BF16)
    v_out[...] = _dot(xkv, wv_ref[...]).astype(BF16)


def _mla_proj(z_f, cos_tab, sin_tab, g_q, g_kv, w_q, w_qrot, w_kn, w_v, layer):
    n = z_f.shape[0]
    tm = min(256, n)

    def zcols(width, off):
        return pl.BlockSpec((tm, width), lambda i: (i, off // width))

    def whole(a):
        return pl.BlockSpec((None,) + a.shape[1:], lambda i: (layer, 0, 0))

    def rows(width):
        return pl.BlockSpec((tm, width), lambda i: (i, 0))

    g_q = g_q.reshape(g_q.shape[0], 1, -1)
    g_kv = g_kv.reshape(g_kv.shape[0], 1, -1)
    return pl.pallas_call(
        _mla_proj_kernel, grid=(n // tm,),
        in_specs=[zcols(MLA_Q_RANK, ZF_CQA), zcols(MLA_KV_RANK, ZF_CKV), zcols(MLA_HEAD_PAD, ZF_KPE),
                  zcols(MLA_HEAD_PAD, ZF_KPER), rows(MLA_HEAD_PAD), rows(MLA_HEAD_PAD),
                  whole(g_q), whole(g_kv), whole(w_q), whole(w_qrot), whole(w_kn), whole(w_v)],
        out_specs=[rows(MLA_QK_W), rows(MLA_QK_W), rows(C_V_W)],
        out_shape=[jax.ShapeDtypeStruct((n, MLA_QK_W), BF16), jax.ShapeDtypeStruct((n, MLA_QK_W), BF16),
                   jax.ShapeDtypeStruct((n, C_V_W), BF16)],
        compiler_params=_cparams("parallel"), name="mla_proj",
    )(z_f, z_f, z_f, z_f, cos_tab, sin_tab, g_q, g_kv, w_q, w_qrot, w_kn, w_v)


ATTN_TILE = 512


def _attn_kernel(slopes_ref, lam_ref, q_ref, k_ref, v_ref, g_ref, o_ref, vt_sc, bias_sc, m_sc, l_sc, acc_sc,
                 *, diff, score_scale, lam_init):
    t = ATTN_TILE
    qi = pl.program_id(2)
    nt = k_ref.shape[1] // t
    c1 = score_scale * LOG2E
    if diff:
        slope2 = slopes_ref[pl.program_id(1)] * LOG2E

    @pl.when(qi == 0)
    def _():
        for j in range(nt):
            vt_sc[j] = v_ref[0, j * t:(j + 1) * t, :].astype(F32).T.astype(BF16)
        key = lax.broadcasted_iota(jnp.int32, (t, t), 0)
        qry = lax.broadcasted_iota(jnp.int32, (t, t), 1)
        allowed = (key // CHUNK) <= (qry // CHUNK)
        if diff:
            bias_sc[0] = slope2 * (qry - key).astype(F32)
            bias_sc[1] = jnp.where(allowed, slope2 * jnp.abs(qry - key).astype(F32), -NEG_BIG)
        else:
            bias_sc[0] = jnp.where(allowed, 0.0, -NEG_BIG)

    qt = q_ref[0].astype(F32).T
    if diff:
        feat = lax.broadcasted_iota(jnp.int32, qt.shape, 0)
        qt = qt * (DA_QK_DIM ** -0.5)
        qts = [jnp.where(feat < DA_QK_DIM, qt, 0.0).astype(BF16), jnp.where(feat >= DA_QK_DIM, qt, 0.0).astype(BF16)]
    else:
        qts = [qt.astype(BF16)]
    nmap = len(qts)

    m_sc[...] = jnp.full(m_sc.shape, NEG_BIG, F32)
    l_sc[...] = jnp.zeros(l_sc.shape, F32)
    acc_sc[...] = jnp.zeros(acc_sc.shape, F32)

    def step(kj, diag):
        k = k_ref[0, pl.ds(pl.multiple_of(kj * t, t), t), :]
        vt = vt_sc[kj]
        if diff:
            bias = bias_sc[1] if diag else bias_sc[0] + ((qi - kj) * t).astype(F32) * slope2
        else:
            bias = bias_sc[0] if diag else None
        for i in range(nmap):
            s = _dot(k, qts[i]) * c1
            if bias is not None:
                s = s - bias
            m_prev = m_sc[i]
            m_new = jnp.maximum(m_prev, jnp.max(s, axis=0, keepdims=True))
            alpha = jnp.exp2(m_prev - m_new)
            p = jnp.exp2(s - m_new)
            l_sc[i] = alpha * l_sc[i] + jnp.sum(p, axis=0, keepdims=True)
            acc_sc[i] = alpha * acc_sc[i] + _dot(vt, p.astype(BF16))
            m_sc[i] = m_new

    def full_step(kj, carry):
        step(kj, False)
        return carry

    lax.fori_loop(0, qi, full_step, 0)
    step(qi, True)

    o = acc_sc[0] / l_sc[0]
    if diff:
        lp = lam_ref[...]
        lam = (jnp.exp(jnp.sum(lp[0:1] * lp[1:2], axis=1, keepdims=True))
               - jnp.exp(jnp.sum(lp[2:3] * lp[3:4], axis=1, keepdims=True)) + lam_init)
        o = o - lam * (acc_sc[1] / l_sc[1])
        o = o * lax.rsqrt(jnp.mean(o * o, axis=0, keepdims=True) + RMS_EPS) * g_ref[...] * (1.0 - lam_init)
    o_ref[0] = o.T.astype(o_ref.dtype)


def _attention(q_arr, k_arr, v_arr, q_off, k_off, v_off, dqk, dv, heads, *, diff, score_scale,
               slopes, lam_params, gain_col, lam_init, name):
    b, t, _ = q_arr.shape
    tq = ATTN_TILE
    nmap = 2 if diff else 1
    return pl.pallas_call(
        functools.partial(_attn_kernel, diff=diff, score_scale=score_scale, lam_init=lam_init),
        grid=(b, heads, t // tq),
        in_specs=[
            pl.BlockSpec(memory_space=pltpu.SMEM),
            pl.BlockSpec(lam_params.shape, lambda bi, h, i: (0, 0)),
            pl.BlockSpec((1, tq, dqk), lambda bi, h, i: (bi, i, q_off + h)),
            pl.BlockSpec((1, t, dqk), lambda bi, h, i: (bi, 0, k_off + h)),
            pl.BlockSpec((1, t, dv), lambda bi, h, i: (bi, 0, v_off + h)),
            pl.BlockSpec(gain_col.shape, lambda bi, h, i: (0, 0)),
        ],
        out_specs=pl.BlockSpec((1, tq, dv), lambda bi, h, i: (bi, i, h)),
        out_shape=jax.ShapeDtypeStruct((b, t, heads * dv), BF16),
        scratch_shapes=[pltpu.VMEM((t // tq, dv, tq), BF16), pltpu.VMEM((nmap, tq, tq), F32),
                        pltpu.VMEM((nmap, 1, tq), F32), pltpu.VMEM((nmap, 1, tq), F32),
                        pltpu.VMEM((nmap, dv, tq), F32)],
        compiler_params=_cparams("parallel", "parallel", "arbitrary"), name=name,
    )(slopes, lam_params, q_arr, k_arr, v_arr, gain_col)


def _alibi_slopes(n):
    def pow2_slopes(m):
        start = 2.0 ** (-8.0 / m)
        return [start ** (i + 1) for i in range(m)]
    c = 2 ** int(math.floor(math.log2(n)))
    s = pow2_slopes(c)
    if c < n:
        s = s + pow2_slopes(2 * c)[0::2][: n - c]
    return np.array(s, dtype=np.float32)


GLA_UNROLL = 4


def _gla_kernel(q_ref, k_ref, v_ref, r_ref, glr_ref, w2_ref, bg_ref, ng_ref, o_ref):
    c = CHUNK
    scale = GLA_DK ** -0.5
    ri = lax.broadcasted_iota(jnp.int32, (c, c), 0)
    ci = lax.broadcasted_iota(jnp.int32, (c, c), 1)
    causal = ci <= ri
    tri = causal.astype(BF16)
    w2 = w2_ref[...]
    bg = bg_ref[...]
    ng = ng_ref[...]

    def body(n, state_t):
        sl = pl.ds(pl.multiple_of(n * c, c), c)
        pre = _dot(glr_ref[0, sl, :].astype(BF16), w2) + bg
        log_a = (jnp.minimum(pre, 0.0) - jnp.log1p(jnp.exp(-jnp.abs(pre)))) / GLA_GATE_NORMALIZER
        hi = log_a.astype(BF16)
        rem = log_a - hi.astype(F32)
        mid = rem.astype(BF16)
        lo = (rem - mid.astype(F32)).astype(BF16)
        bcum = _dot(tri, hi) + _dot(tri, mid) + _dot(tri, lo)
        b_mid = bcum[c // 2:c // 2 + 1, :]
        b_last = bcum[c - 1:c, :]
        q = q_ref[0, sl, :] * scale
        k = k_ref[0, sl, :]
        v = v_ref[0, sl, :].astype(BF16)
        att = _dot_nt((q * jnp.exp(bcum - b_mid)).astype(BF16), (k * jnp.exp(b_mid - bcum)).astype(BF16))
        att = jnp.where(causal, att, 0.0)
        o = _dot(att.astype(BF16), v) + _dot_nt((q * jnp.exp(bcum)).astype(BF16), state_t.astype(BF16))
        upd_t = _dot_tn(v, (k * jnp.exp(b_last - bcum)).astype(BF16))
        r = r_ref[0, sl, :]
        o_ref[0, sl, :] = (_rms_rows(o, ng) * (r * jax.nn.sigmoid(r))).astype(o_ref.dtype)
        return state_t * jnp.exp(b_last) + upd_t

    lax.fori_loop(0, q_ref.shape[1] // c, body, jnp.zeros((GLA_DV, GLA_DK), F32), unroll=GLA_UNROLL)


def _gla(z_f3, w_gate2_pad, b_gate, norm_g, layer):
    b, t, _ = z_f3.shape

    def zcols(width, off):
        return pl.BlockSpec((1, t, width), lambda bi, h: (bi, 0, off // width + h))

    return pl.pallas_call(
        _gla_kernel, grid=(b, GLA_HEADS),
        in_specs=[zcols(GLA_DK, ZF_BQ), zcols(GLA_DK, ZF_BK), zcols(GLA_DV, ZF_BV), zcols(GLA_DV, ZF_BR),
                  pl.BlockSpec((1, t, 128), lambda bi, h: (bi, 0, ZF_GLR // 128)),
                  pl.BlockSpec((None, 128, GLA_DK), lambda bi, h: (layer, 0, h)),
                  pl.BlockSpec((None, 1, GLA_DK), lambda bi, h: (layer, 0, h)),
                  pl.BlockSpec((None, 1, GLA_DV), lambda bi, h: (layer, 0, 0))],
        out_specs=pl.BlockSpec((1, t, GLA_DV), lambda bi, h: (bi, 0, h)),
        out_shape=jax.ShapeDtypeStruct((b, t, B_V_W), BF16),
        compiler_params=_cparams("parallel", "parallel"), name="gla",
    )(z_f3, z_f3, z_f3, z_f3, z_f3, w_gate2_pad, b_gate.reshape(b_gate.shape[0], 1, -1),
      norm_g.reshape(norm_g.shape[0], 1, -1))


def _rot_half_cols(w):
    half = w.shape[-1] // 2
    return jnp.concatenate([-w[..., half:], w[..., :half]], axis=-1)


def _prep_in_proj(w_in):
    lead = w_in.shape[:-1]
    offs = np.cumsum((A_QK_W, A_QK_W, A_V_W, B_K_W, B_K_W, B_V_W, GLA_GATE_RANK, B_V_W, MLA_Q_RANK,
                      MLA_KV_RANK + MLA_ROPE_DIM))
    a_end = offs[2]
    b_q, b_k, b_v, b_glr, b_r, c_qa, c_kva = [w_in[..., offs[i]:offs[i + 1]] for i in range(2, 9)]
    c_kv, k_pe = c_kva[..., :MLA_KV_RANK], c_kva[..., MLA_KV_RANK:]

    def zeros(width):
        return jnp.zeros(lead + (width,), w_in.dtype)

    def in_rope_lanes(w):
        return jnp.concatenate([zeros(MLA_NOPE_DIM), w, zeros(MLA_HEAD_PAD - MLA_NOPE_DIM - MLA_ROPE_DIM)], axis=-1)

    w_f = jnp.concatenate([c_qa, in_rope_lanes(k_pe), b_q, b_k, b_v, b_r, c_kv, in_rope_lanes(_rot_half_cols(k_pe)),
                           b_glr, zeros(ZF_W - ZF_GLR - GLA_GATE_RANK)], axis=-1)
    assert w_f.shape[-1] == ZF_W
    return w_in[..., :a_end].astype(BF16), w_f.astype(BF16)


def _prep_mla(w_qb, w_kvb):
    lead = w_qb.shape[:-1]
    wq = w_qb.reshape(lead + (MLA_HEADS, MLA_NOPE_DIM + MLA_ROPE_DIM))
    nope, rope = wq[..., :MLA_NOPE_DIM], wq[..., MLA_NOPE_DIM:]
    pad = jnp.zeros(lead + (MLA_HEADS, MLA_HEAD_PAD - MLA_NOPE_DIM - MLA_ROPE_DIM), w_qb.dtype)
    w_q = jnp.concatenate([nope, rope, pad], axis=-1).reshape(lead + (MLA_QK_W,))
    w_qrot = jnp.concatenate([jnp.zeros_like(nope), _rot_half_cols(rope), pad], axis=-1).reshape(lead + (MLA_QK_W,))
    lead = w_kvb.shape[:-1]
    wkv = w_kvb.reshape(lead + (MLA_HEADS, MLA_NOPE_DIM + MLA_V_DIM))
    k_nope, v = wkv[..., :MLA_NOPE_DIM], wkv[..., MLA_NOPE_DIM:]
    w_kn = jnp.concatenate([k_nope, jnp.zeros(lead + (MLA_HEADS, MLA_HEAD_PAD - MLA_NOPE_DIM), w_kvb.dtype)],
                           axis=-1).reshape(lead + (MLA_QK_W,))
    return w_q.astype(BF16), w_qrot.astype(BF16), w_kn.astype(BF16), v.reshape(lead + (C_V_W,)).astype(BF16)


def _cast_gu_kernel(g_ref, u_ref, o_ref):
    tf = g_ref.shape[1]
    o_ref[:, :tf] = g_ref[...].astype(BF16)
    o_ref[:, tf:] = u_ref[...].astype(BF16)


def _prep_ffn_gu(w_gu, tf):
    nl, two, d, f2 = w_gu.shape
    ff = f2 // 2
    assert ff % tf == 0
    nf = ff // tf

    def cols(off):
        return pl.BlockSpec((None, None, d, tf), lambda lj, f: (lj // two, lj % two, 0, off + f))

    return pl.pallas_call(
        _cast_gu_kernel, grid=(nl * two, nf), in_specs=[cols(0), cols(nf)],
        out_specs=pl.BlockSpec((None, None, None, d, 2 * tf), lambda lj, f: (lj // two, lj % two, f, 0, 0)),
        out_shape=jax.ShapeDtypeStruct((nl, two, nf, d, 2 * tf), BF16),
        compiler_params=_cparams("parallel", "parallel"), name="cast_gu",
    )(w_gu, w_gu)


def kernel(x, p, positions, emb_ln_g, emb_ln_b, w_in, w_branch_gate, da_lambda_q1, da_lambda_k1, da_lambda_q2,
           da_lambda_k2, da_subln_g, gla_w_gate2, gla_b_gate, gla_norm_g, mla_q_norm_g, mla_w_qb, mla_kv_norm_g,
           mla_w_kvb, w_o_a, w_o_b, w_o_c, w_out, ffn_w_gu, ffn_w_down, ln_g, ln_b, ple_w_proj, ple_w_gate):
    b, t, d = x.shape
    n = b * t
    depth = w_in.shape[0]
    alpha = float(DEEPNORM_ALPHA)

    w_gu = _prep_ffn_gu(ffn_w_gu, FFN_CHUNK)
    w_down = ffn_w_down.astype(BF16)
    w_a, w_f = _prep_in_proj(w_in)
    w_q, w_qrot, w_kn, w_v = _prep_mla(mla_w_qb, mla_w_kvb)
    w_bg = w_branch_gate.astype(BF16)
    w_oa, w_ob, w_oc, w_o = (w.astype(BF16) for w in (w_o_a, w_o_b, w_o_c, w_out))
    w_pg, w_pp = ple_w_gate.astype(BF16), ple_w_proj.astype(BF16)
    w2_pad = jnp.pad(gla_w_gate2.astype(BF16), ((0, 0), (0, 128 - GLA_GATE_RANK), (0, 0)))
    p = p.reshape(depth, n, -1)

    cos_tab, sin_tab = _rope_tables(positions)
    slopes = jnp.asarray(_alibi_slopes(DA_HEADS))
    no_slopes = jnp.zeros((1,), F32)
    no_lam = jnp.zeros((4, DA_QK_DIM), F32)
    no_gain = jnp.ones((MLA_V_DIM, 1), F32)

    h, hb = _layer_norm(x.reshape(n, d), emb_ln_g, emb_ln_b)
    for i in range(depth):
        h, hb = _ffn_ln(h, hb, w_gu, w_down, i, 0, ln_g[i, 0], ln_b[i, 0], alpha)

        z_a = _matmul(hb, w_a, i, BF16, "in_proj_a").reshape(b, t, -1)
        z_f = _matmul(hb, w_f, i, F32, "in_proj_f")
        lam_init = 0.8 - 0.6 * math.exp(-0.3 * i)
        lam_params = jnp.stack([da_lambda_q1[i], da_lambda_k1[i], da_lambda_q2[i], da_lambda_k2[i]]).astype(F32)
        o_a = _attention(z_a, z_a, z_a, 0, DA_HEADS, 2 * DA_HEADS, 2 * DA_QK_DIM, DA_V_DIM, DA_HEADS,
                         diff=True, score_scale=1.0, slopes=slopes, lam_params=lam_params,
                         gain_col=da_subln_g[i].reshape(-1, 1), lam_init=lam_init, name="diff_attn")
        o_b = _gla(z_f.reshape(b, t, ZF_W), w2_pad, gla_b_gate, gla_norm_g, i)
        q_c, k_c, v_c = _mla_proj(z_f, cos_tab, sin_tab, mla_q_norm_g, mla_kv_norm_g, w_q, w_qrot, w_kn, w_v, i)
        o_c = _attention(q_c.reshape(b, t, -1), k_c.reshape(b, t, -1), v_c.reshape(b, t, -1), 0, 0, 0,
                         MLA_HEAD_PAD, MLA_V_DIM, MLA_HEADS, diff=False,
                         score_scale=(MLA_NOPE_DIM + MLA_ROPE_DIM) ** -0.5, slopes=no_slopes, lam_params=no_lam,
                         gain_col=no_gain, lam_init=0.0, name="mla_attn")
        merged = _merge(hb, o_a.reshape(n, -1), o_b.reshape(n, -1), o_c.reshape(n, -1), w_bg, w_oa, w_ob, w_oc, i)
        h, hb = _proj_ln(h, merged, w_o, i, ln_g[i, 1], ln_b[i, 1], alpha)

        resid = _ple_resid(hb, w_pg, p, w_pp, i, h, alpha)
        h, hb = _ffn_ln(resid, hb, w_gu, w_down, i, 1, ln_g[i, 2], ln_b[i, 2], 1.0)
    return h.reshape(b, t, d)
```

```python
import functools
import math

import numpy as np
import jax
import jax.numpy as jnp
from jax import lax
from jax.experimental import pallas as pl
from jax.experimental.pallas import tpu as pltpu

F32 = jnp.float32
BF16 = jnp.bfloat16

DEPTH = 2
CHUNK = 64
N_BRANCHES = 3
DA_HEADS = 12
DA_QK_DIM = 64
DA_V_DIM = 128
GLA_HEADS = 4
GLA_DK = 128
GLA_DV = 256
GLA_GATE_RANK = 16
GLA_GATE_NORMALIZER = 16.0
MLA_HEADS = 12
MLA_Q_RANK = 768
MLA_KV_RANK = 512
MLA_NOPE_DIM = 128
MLA_ROPE_DIM = 64
MLA_V_DIM = 128
ROPE_THETA = 10000.0
LN_EPS = 1e-5
RMS_EPS = 1e-6
DEEPNORM_ALPHA = (2 * DEPTH) ** 0.25

A_QK_W = DA_HEADS * 2 * DA_QK_DIM
A_V_W = DA_HEADS * DA_V_DIM
B_K_W = GLA_HEADS * GLA_DK
B_V_W = GLA_HEADS * GLA_DV
C_V_W = MLA_HEADS * MLA_V_DIM
MLA_HEAD_PAD = 256
MLA_QK_W = MLA_HEADS * MLA_HEAD_PAD

ZF_CQA = 0
ZF_KPE = 768
ZF_BQ = 1024
ZF_BK = 1536
ZF_BV = 2048
ZF_BR = 3072
ZF_CKV = 4096
ZF_KPER = 4608
ZF_GLR = 4864
ZF_W = 5120

VMEM_CAP_V7X = 64 * 1024 * 1024
VMEM_LIMIT = VMEM_CAP_V7X - 8 * 1024 * 1024
VMEM_LIMIT_FFN = VMEM_CAP_V7X - 4 * 1024 * 1024
FFN_CHUNK = 256
NEG_BIG = -1e30
LOG2E = math.log2(math.e)


def _cparams(*sem, vmem=VMEM_LIMIT):
    return pltpu.CompilerParams(dimension_semantics=sem, vmem_limit_bytes=vmem)


def _dot(a, b):
    return jnp.dot(a, b, preferred_element_type=F32)


def _dot_nt(a, b):
    return lax.dot_general(a, b, (((1,), (1,)), ((), ())), preferred_element_type=F32)


def _dot_tn(a, b):
    return lax.dot_general(a, b, (((0,), (0,)), ((), ())), preferred_element_type=F32)


def _ln_rows(y, g, b):
    mu = jnp.mean(y, axis=-1, keepdims=True)
    yc = y - mu
    var = jnp.mean(yc * yc, axis=-1, keepdims=True)
    return yc * lax.rsqrt(var + LN_EPS) * g + b


def _rms_rows(y, g):
    return y * lax.rsqrt(jnp.mean(y * y, axis=-1, keepdims=True) + RMS_EPS) * g


def _ln_kernel(x_ref, g_ref, b_ref, o_ref, ob_ref):
    y = _ln_rows(x_ref[...], g_ref[...], b_ref[...])
    o_ref[...] = y
    ob_ref[...] = y.astype(BF16)


def _layer_norm(x, g, b):
    n, d = x.shape
    tr = min(256, n)
    row = pl.BlockSpec((tr, d), lambda i: (i, 0))
    vec = pl.BlockSpec((1, d), lambda i: (0, 0))
    return pl.pallas_call(
        _ln_kernel, grid=(n // tr,), in_specs=[row, vec, vec], out_specs=[row, row],
        out_shape=[jax.ShapeDtypeStruct((n, d), F32), jax.ShapeDtypeStruct((n, d), BF16)],
        compiler_params=_cparams("parallel"), name="ln_emb",
    )(x, g.reshape(1, d), b.reshape(1, d))


LN_SLAB = 128


def _resid_copy(resid_hbm, dst_ref, sem, block):
    tm = dst_ref.shape[0]
    start = pl.multiple_of(block * tm, tm)
    return pltpu.make_async_copy(resid_hbm.at[pl.ds(start, tm), :], dst_ref, sem)


def _ln_epilogue(o_ref, ob_ref, g_ref, b_ref, out_scale):
    g = g_ref[...]
    b = b_ref[...]

    def body(r, carry):
        sl = pl.ds(pl.multiple_of(r * LN_SLAB, LN_SLAB), LN_SLAB)
        y = _ln_rows(o_ref[sl, :] * out_scale, g, b)
        o_ref[sl, :] = y
        ob_ref[sl, :] = y.astype(BF16)
        return carry

    lax.fori_loop(0, o_ref.shape[0] // LN_SLAB, body, 0)


def _ffn_kernel(resid_hbm, hb_ref, wgu_ref, wd_ref, g_ref, b_ref, o_ref, ob_ref, sem, *, res_scale):
    f = pl.program_id(1)
    tf = wd_ref.shape[0]

    @pl.when(f == 0)
    def _():
        _resid_copy(resid_hbm, o_ref, sem, pl.program_id(0)).start()

    gu = _dot(hb_ref[...], wgu_ref[...])
    gate = gu[:, :tf]
    act = (gate * jax.nn.sigmoid(gate) * gu[:, tf:]).astype(BF16)

    @pl.when(f == 0)
    def _():
        _resid_copy(resid_hbm, o_ref, sem, pl.program_id(0)).wait()
        o_ref[...] = o_ref[...] * (2.0 * res_scale)

    o_ref[...] += _dot(act, wd_ref[...])

    @pl.when(f == pl.num_programs(1) - 1)
    def _():
        _ln_epilogue(o_ref, ob_ref, g_ref, b_ref, 0.5)


def _ffn_ln(resid, hb, w_gu, w_down, layer, half, ln_g, ln_b, res_scale):
    n, d = hb.shape
    nf, tf = w_gu.shape[2], w_gu.shape[4] // 2
    tm = min(1024, n)
    vec = pl.BlockSpec((1, d), lambda i, f: (0, 0))
    row = pl.BlockSpec((tm, d), lambda i, f: (i, 0), pipeline_mode=pl.Buffered(1))
    return pl.pallas_call(
        functools.partial(_ffn_kernel, res_scale=res_scale),
        grid=(n // tm, nf),
        in_specs=[
            pl.BlockSpec(memory_space=pl.ANY),
            pl.BlockSpec((tm, d), lambda i, f: (i, 0)),
            pl.BlockSpec((None, None, None, d, 2 * tf), lambda i, f: (layer, half, f, 0, 0)),
            pl.BlockSpec((None, None, tf, d), lambda i, f: (layer, half, f, 0)),
            vec, vec,
        ],
        out_specs=[row, row],
        out_shape=[jax.ShapeDtypeStruct((n, d), F32), jax.ShapeDtypeStruct((n, d), BF16)],
        scratch_shapes=[pltpu.SemaphoreType.DMA(())],
        compiler_params=_cparams("parallel", "arbitrary", vmem=VMEM_LIMIT_FFN), name="ffn_ln",
    )(resid, hb, w_gu, w_down, ln_g.reshape(1, d), ln_b.reshape(1, d))


def _proj_ln_kernel(resid_hbm, x_ref, w_ref, g_ref, b_ref, o_ref, ob_ref, r_sc, sem, *, res_scale):
    i = pl.program_id(0)
    k = pl.program_id(1)

    @pl.when((i == 0) & (k == 0))
    def _():
        _resid_copy(resid_hbm, r_sc, sem, 0).start()

    @pl.when((k == 1) & (i + 1 < pl.num_programs(0)))
    def _():
        _resid_copy(resid_hbm, r_sc, sem, i + 1).start()

    @pl.when(k == 0)
    def _():
        _resid_copy(resid_hbm, r_sc, sem, i).wait()
        o_ref[...] = r_sc[...] * res_scale + _dot(x_ref[...], w_ref[...])

    @pl.when(k > 0)
    def _():
        o_ref[...] += _dot(x_ref[...], w_ref[...])

    @pl.when(k == pl.num_programs(1) - 1)
    def _():
        _ln_epilogue(o_ref, ob_ref, g_ref, b_ref, 1.0)


def _proj_ln(resid, x, w, layer, ln_g, ln_b, res_scale):
    n, kdim = x.shape
    d = w.shape[2]
    tm = min(512, n)
    tk = 512
    assert kdim // tk >= 2
    vec = pl.BlockSpec((1, d), lambda i, k: (0, 0))
    row = pl.BlockSpec((tm, d), lambda i, k: (i, 0))
    return pl.pallas_call(
        functools.partial(_proj_ln_kernel, res_scale=res_scale),
        grid=(n // tm, kdim // tk),
        in_specs=[
            pl.BlockSpec(memory_space=pl.ANY),
            pl.BlockSpec((tm, tk), lambda i, k: (i, k)),
            pl.BlockSpec((None, tk, d), lambda i, k: (layer, k, 0)),
            vec, vec,
        ],
        out_specs=[row, row],
        out_shape=[jax.ShapeDtypeStruct((n, d), F32), jax.ShapeDtypeStruct((n, d), BF16)],
        scratch_shapes=[pltpu.VMEM((tm, d), F32), pltpu.SemaphoreType.DMA(())],
        compiler_params=_cparams("arbitrary", "arbitrary"), name="proj_ln",
    )(resid, x, w, ln_g.reshape(1, d), ln_b.reshape(1, d))


def _mm_kernel(x_ref, w_ref, o_ref):
    o_ref[...] = _dot(x_ref[...], w_ref[...]).astype(o_ref.dtype)


def _matmul(x, w, layer, out_dtype, name):
    n, kdim = x.shape
    m = w.shape[2]
    tm = min(1024, n)
    tn = 512
    return pl.pallas_call(
        _mm_kernel, grid=(n // tm, m // tn),
        in_specs=[pl.BlockSpec((tm, kdim), lambda i, j: (i, 0)),
                  pl.BlockSpec((None, kdim, tn), lambda i, j: (layer, 0, j))],
        out_specs=pl.BlockSpec((tm, tn), lambda i, j: (i, j)),
        out_shape=jax.ShapeDtypeStruct((n, m), out_dtype),
        compiler_params=_cparams("parallel", "parallel"), name=name,
    )(x, w)


def _ple_kernel(hb_ref, wg_ref, p_ref, wp_ref, h_ref, o_ref, *, alpha):
    gate = jax.nn.sigmoid(_dot(hb_ref[...], wg_ref[...]))
    proj = _dot(p_ref[...].astype(BF16), wp_ref[...])
    o_ref[...] = alpha * h_ref[...] + gate * proj


def _ple_resid(hb, w_gate, p, w_proj, layer, h, alpha):
    n, d = hb.shape
    pd = p.shape[2]
    tm = min(1024, n)
    tn = 512
    tile = pl.BlockSpec((tm, tn), lambda i, j: (i, j))
    return pl.pallas_call(
        functools.partial(_ple_kernel, alpha=alpha), grid=(n // tm, d // tn),
        in_specs=[
            pl.BlockSpec((tm, d), lambda i, j: (i, 0)),
            pl.BlockSpec((None, d, tn), lambda i, j: (layer, 0, j)),
            pl.BlockSpec((None, tm, pd), lambda i, j: (layer, i, 0)),
            pl.BlockSpec((None, pd, tn), lambda i, j: (layer, 0, j)),
            tile,
        ],
        out_specs=tile,
        out_shape=jax.ShapeDtypeStruct((n, d), F32),
        compiler_params=_cparams("parallel", "parallel"), name="ple_resid",
    )(hb, w_gate, p, w_proj, h)


def _merge_kernel(hb_ref, oa_ref, ob_ref, oc_ref, g0_ref, g1_ref, g2_ref, wa_ref, wb_ref, wc_ref, o_ref):
    x = hb_ref[...]
    acc = jax.nn.sigmoid(_dot(x, g0_ref[...])) * _dot(oa_ref[...], wa_ref[...])
    acc += jax.nn.sigmoid(_dot(x, g1_ref[...])) * _dot(ob_ref[...], wb_ref[...])
    acc += jax.nn.sigmoid(_dot(x, g2_ref[...])) * _dot(oc_ref[...], wc_ref[...])
    o_ref[...] = acc.astype(o_ref.dtype)


def _merge(hb, o_a, o_b, o_c, w_bg, w_oa, w_ob, w_oc, layer):
    n, d = hb.shape
    tm = min(512, n)
    tn = 512
    nj = d // tn

    def rows(w):
        return pl.BlockSpec((tm, w), lambda i, j: (i, 0), pipeline_mode=pl.Buffered(1))

    def cols(kdim, off):
        return pl.BlockSpec((None, kdim, tn), lambda i, j: (layer, 0, off + j))

    return pl.pallas_call(
        _merge_kernel, grid=(n // tm, nj),
        in_specs=[rows(d), rows(o_a.shape[1]), rows(o_b.shape[1]), rows(o_c.shape[1]),
                  cols(d, 0), cols(d, nj), cols(d, 2 * nj),
                  cols(w_oa.shape[1], 0), cols(w_ob.shape[1], 0), cols(w_oc.shape[1], 0)],
        out_specs=pl.BlockSpec((tm, tn), lambda i, j: (i, j)),
        out_shape=jax.ShapeDtypeStruct((n, d), BF16),
        compiler_params=_cparams("parallel", "parallel"), name="merge",
    )(hb, o_a, o_b, o_c, w_bg, w_bg, w_bg, w_oa, w_ob, w_oc)


def _rope_tab_kernel(pos_ref, inv_ref, c_ref, s_ref):
    ang = pos_ref[...].astype(F32) * inv_ref[...]
    lane = lax.broadcasted_iota(jnp.int32, ang.shape, 1)
    rope = (lane >= MLA_NOPE_DIM) & (lane < MLA_NOPE_DIM + MLA_ROPE_DIM)
    c_ref[...] = jnp.where(lane < MLA_NOPE_DIM, 1.0, jnp.where(rope, jnp.cos(ang), 0.0))
    s_ref[...] = jnp.where(rope, jnp.sin(ang), 0.0)


def _rope_tables(positions):
    n = positions.size
    half = MLA_ROPE_DIM // 2
    inv = ROPE_THETA ** (-np.arange(half, dtype=np.float32) / half)
    inv_row = np.zeros((1, MLA_HEAD_PAD), np.float32)
    inv_row[0, MLA_NOPE_DIM:MLA_NOPE_DIM + MLA_ROPE_DIM] = np.concatenate([inv, inv])
    tr = min(512, n)
    tab = pl.BlockSpec((tr, MLA_HEAD_PAD), lambda i: (i, 0))
    return pl.pallas_call(
        _rope_tab_kernel, grid=(n // tr,),
        in_specs=[pl.BlockSpec((tr, 1), lambda i: (i, 0)), pl.BlockSpec((1, MLA_HEAD_PAD), lambda i: (0, 0))],
        out_specs=[tab, tab],
        out_shape=[jax.ShapeDtypeStruct((n, MLA_HEAD_PAD), F32)] * 2,
        compiler_params=_cparams("parallel"), name="rope_tables",
    )(positions.reshape(n, 1), jnp.asarray(inv_row))


def _mla_proj_kernel(cqa_ref, ckv_ref, kpe_ref, kper_ref, c_ref, s_ref, gq_ref, gkv_ref,
                     wq_ref, wqr_ref, wkn_ref, wv_ref, q_out, k_out, v_out):
    cos = c_ref[...]
    sin = s_ref[...]
    xq = _rms_rows(cqa_ref[...], gq_ref[...]).astype(BF16)
    q = _dot(xq, wq_ref[...])
    q_rot = _dot(xq, wqr_ref[...])
    xkv = _rms_rows(ckv_ref[...], gkv_ref[...]).astype(BF16)
    k_nope = _dot(xkv, wkn_ref[...])
    k_rope = kpe_ref[...] * cos + kper_ref[...] * sin
    for h in range(MLA_HEADS):
        sl = slice(h * MLA_HEAD_PAD, (h + 1) * MLA_HEAD_PAD)
        q_out[:, sl] = (q[:, sl] * cos + q_rot[:, sl] * sin).astype(BF16)
        k_out[:, sl] = (k_nope[:, sl] + k_rope).astype(BF16)
    v_out[...] = _dot(xkv, wv_ref[...]).astype(BF16)


def _mla_proj(z_f, cos_tab, sin_tab, g_q, g_kv, w_q, w_qrot, w_kn, w_v, layer):
    n = z_f.shape[0]
    tm = min(256, n)

    def zcols(width, off):
        return pl.BlockSpec((tm, width), lambda i: (i, off // width))

    def whole(a):
        return pl.BlockSpec((None,) + a.shape[1:], lambda i: (layer, 0, 0))

    def rows(width):
        return pl.BlockSpec((tm, width), lambda i: (i, 0))

    g_q = g_q.reshape(g_q.shape[0], 1, -1)
    g_kv = g_kv.reshape(g_kv.shape[0], 1, -1)
    return pl.pallas_call(
        _mla_proj_kernel, grid=(n // tm,),
        in_specs=[zcols(MLA_Q_RANK, ZF_CQA), zcols(MLA_KV_RANK, ZF_CKV), zcols(MLA_HEAD_PAD, ZF_KPE),
                  zcols(MLA_HEAD_PAD, ZF_KPER), rows(MLA_HEAD_PAD), rows(MLA_HEAD_PAD),
                  whole(g_q), whole(g_kv), whole(w_q), whole(w_qrot), whole(w_kn), whole(w_v)],
        out_specs=[rows(MLA_QK_W), rows(MLA_QK_W), rows(C_V_W)],
        out_shape=[jax.ShapeDtypeStruct((n, MLA_QK_W), BF16), jax.ShapeDtypeStruct((n, MLA_QK_W), BF16),
                   jax.ShapeDtypeStruct((n, C_V_W), BF16)],
        compiler_params=_cparams("parallel"), name="mla_proj",
    )(z_f, z_f, z_f, z_f, cos_tab, sin_tab, g_q, g_kv, w_q, w_qrot, w_kn, w_v)


ATTN_TILE = 512
ATTN_HEADS_PER_STEP = 2


def _attn_kernel(slopes_ref, lam_ref, q_ref, k_ref, v_ref, g_ref, o_ref, vt_sc, bias_sc, m_sc, l_sc, acc_sc,
                 *, diff, score_scale, lam_init, dqk, dv):
    t = ATTN_TILE
    hp = ATTN_HEADS_PER_STEP
    qi = pl.program_id(2)
    nt = k_ref.shape[1] // t
    c1 = score_scale * LOG2E
    nmap = 2 if diff else 1
    if diff:
        slope2 = [slopes_ref[pl.program_id(1) * hp + hh] * LOG2E for hh in range(hp)]

    @pl.when(qi == 0)
    def _():
        for hh in range(hp):
            for j in range(nt):
                vt_sc[hh, j] = v_ref[0, j * t:(j + 1) * t, hh * dv:(hh + 1) * dv].astype(F32).T.astype(BF16)
        key = lax.broadcasted_iota(jnp.int32, (t, t), 0)
        qry = lax.broadcasted_iota(jnp.int32, (t, t), 1)
        allowed = (key // CHUNK) <= (qry // CHUNK)
        if diff:
            for hh in range(hp):
                bias_sc[hh, 0] = slope2[hh] * (qry - key).astype(F32)
                bias_sc[hh, 1] = jnp.where(allowed, slope2[hh] * jnp.abs(qry - key).astype(F32), -NEG_BIG)
        else:
            bias_sc[0, 0] = jnp.where(allowed, 0.0, -NEG_BIG)

    qts = []
    for hh in range(hp):
        qt = q_ref[0, :, hh * dqk:(hh + 1) * dqk].astype(F32).T
        if diff:
            feat = lax.broadcasted_iota(jnp.int32, qt.shape, 0)
            qt = qt * (DA_QK_DIM ** -0.5)
            qts.append(jnp.where(feat < DA_QK_DIM, qt, 0.0).astype(BF16))
            qts.append(jnp.where(feat >= DA_QK_DIM, qt, 0.0).astype(BF16))
        else:
            qts.append(qt.astype(BF16))

    m_sc[...] = jnp.full(m_sc.shape, NEG_BIG, F32)
    l_sc[...] = jnp.zeros(l_sc.shape, F32)
    acc_sc[...] = jnp.zeros(acc_sc.shape, F32)

    def step(kj, diag):
        ks = pl.ds(pl.multiple_of(kj * t, t), t)
        for hh in range(hp):
            k = k_ref[0, ks, hh * dqk:(hh + 1) * dqk]
            vt = vt_sc[hh, kj]
            if diff:
                bias = bias_sc[hh, 1] if diag else bias_sc[hh, 0] + ((qi - kj) * t).astype(F32) * slope2[hh]
            else:
                bias = bias_sc[0, 0] if diag else None
            for i in range(hh * nmap, (hh + 1) * nmap):
                s = _dot(k, qts[i]) * c1
                if bias is not None:
                    s = s - bias
                m_prev = m_sc[i]
                m_new = jnp.maximum(m_prev, jnp.max(s, axis=0, keepdims=True))
                alpha = jnp.exp2(m_prev - m_new)
                p = jnp.exp2(s - m_new)
                l_sc[i] = alpha * l_sc[i] + jnp.sum(p, axis=0, keepdims=True)
                acc_sc[i] = alpha * acc_sc[i] + _dot(vt, p.astype(BF16))
                m_sc[i] = m_new

    def full_step(kj, carry):
        step(kj, False)
        return carry

    lax.fori_loop(0, qi, full_step, 0)
    step(qi, True)

    if diff:
        lp = lam_ref[...]
        lam = (jnp.exp(jnp.sum(lp[0:1] * lp[1:2], axis=1, keepdims=True))
               - jnp.exp(jnp.sum(lp[2:3] * lp[3:4], axis=1, keepdims=True)) + lam_init)
    for hh in range(hp):
        o = acc_sc[hh * nmap] / l_sc[hh * nmap]
        if diff:
            o = o - lam * (acc_sc[hh * nmap + 1] / l_sc[hh * nmap + 1])
            o = o * lax.rsqrt(jnp.mean(o * o, axis=0, keepdims=True) + RMS_EPS) * g_ref[...] * (1.0 - lam_init)
        o_ref[0, :, hh * dv:(hh + 1) * dv] = o.T.astype(o_ref.dtype)


def _attention(q_arr, k_arr, v_arr, q_off, k_off, v_off, dqk, dv, heads, *, diff, score_scale,
               slopes, lam_params, gain_col, lam_init, name):
    b, t, _ = q_arr.shape
    tq = ATTN_TILE
    hp = ATTN_HEADS_PER_STEP
    assert heads % hp == 0 and q_off % hp == 0 and k_off % hp == 0 and v_off % hp == 0
    nmap = 2 if diff else 1
    nbias = (hp, 2) if diff else (1, 1)
    return pl.pallas_call(
        functools.partial(_attn_kernel, diff=diff, score_scale=score_scale, lam_init=lam_init, dqk=dqk, dv=dv),
        grid=(b, heads // hp, t // tq),
        in_specs=[
            pl.BlockSpec(memory_space=pltpu.SMEM),
            pl.BlockSpec(lam_params.shape, lambda bi, g, i: (0, 0)),
            pl.BlockSpec((1, tq, hp * dqk), lambda bi, g, i: (bi, i, q_off // hp + g)),
            pl.BlockSpec((1, t, hp * dqk), lambda bi, g, i: (bi, 0, k_off // hp + g)),
            pl.BlockSpec((1, t, hp * dv), lambda bi, g, i: (bi, 0, v_off // hp + g)),
            pl.BlockSpec(gain_col.shape, lambda bi, g, i: (0, 0)),
        ],
        out_specs=pl.BlockSpec((1, tq, hp * dv), lambda bi, g, i: (bi, i, g)),
        out_shape=jax.ShapeDtypeStruct((b, t, heads * dv), BF16),
        scratch_shapes=[pltpu.VMEM((hp, t // tq, dv, tq), BF16), pltpu.VMEM(nbias + (tq, tq), F32),
                        pltpu.VMEM((hp * nmap, 1, tq), F32), pltpu.VMEM((hp * nmap, 1, tq), F32),
                        pltpu.VMEM((hp * nmap, dv, tq), F32)],
        compiler_params=_cparams("parallel", "parallel", "arbitrary"), name=name,
    )(slopes, lam_params, q_arr, k_arr, v_arr, gain_col)


def _alibi_slopes(n):
    def pow2_slopes(m):
        start = 2.0 ** (-8.0 / m)
        return [start ** (i + 1) for i in range(m)]
    c = 2 ** int(math.floor(math.log2(n)))
    s = pow2_slopes(c)
    if c < n:
        s = s + pow2_slopes(2 * c)[0::2][: n - c]
    return np.array(s, dtype=np.float32)


GLA_UNROLL = 4


def _gla_kernel(q_ref, k_ref, v_ref, r_ref, glr_ref, w2_ref, bg_ref, ng_ref, o_ref):
    c = CHUNK
    scale = GLA_DK ** -0.5
    ri = lax.broadcasted_iota(jnp.int32, (c, c), 0)
    ci = lax.broadcasted_iota(jnp.int32, (c, c), 1)
    causal = ci <= ri
    tri = causal.astype(BF16)
    w2 = w2_ref[...]
    bg = bg_ref[...]
    ng = ng_ref[...]

    def body(n, state_t):
        sl = pl.ds(pl.multiple_of(n * c, c), c)
        pre = _dot(glr_ref[0, sl, :].astype(BF16), w2) + bg
        log_a = (jnp.minimum(pre, 0.0) - jnp.log1p(jnp.exp(-jnp.abs(pre)))) / GLA_GATE_NORMALIZER
        hi = log_a.astype(BF16)
        rem = log_a - hi.astype(F32)
        mid = rem.astype(BF16)
        lo = (rem - mid.astype(F32)).astype(BF16)
        bcum = _dot(tri, hi) + _dot(tri, mid) + _dot(tri, lo)
        b_mid = bcum[c // 2:c // 2 + 1, :]
        b_last = bcum[c - 1:c, :]
        q = q_ref[0, sl, :] * scale
        k = k_ref[0, sl, :]
        v = v_ref[0, sl, :].astype(BF16)
        att = _dot_nt((q * jnp.exp(bcum - b_mid)).astype(BF16), (k * jnp.exp(b_mid - bcum)).astype(BF16))
        att = jnp.where(causal, att, 0.0)
        o = _dot(att.astype(BF16), v) + _dot_nt((q * jnp.exp(bcum)).astype(BF16), state_t.astype(BF16))
        upd_t = _dot_tn(v, (k * jnp.exp(b_last - bcum)).astype(BF16))
        r = r_ref[0, sl, :]
        o_ref[0, sl, :] = (_rms_rows(o, ng) * (r * jax.nn.sigmoid(r))).astype(o_ref.dtype)
        return state_t * jnp.exp(b_last) + upd_t

    lax.fori_loop(0, q_ref.shape[1] // c, body, jnp.zeros((GLA_DV, GLA_DK), F32), unroll=GLA_UNROLL)


def _gla(z_f3, w_gate2_pad, b_gate, norm_g, layer):
    b, t, _ = z_f3.shape

    def zcols(width, off):
        return pl.BlockSpec((1, t, width), lambda bi, h: (bi, 0, off // width + h))

    return pl.pallas_call(
        _gla_kernel, grid=(b, GLA_HEADS),
        in_specs=[zcols(GLA_DK, ZF_BQ), zcols(GLA_DK, ZF_BK), zcols(GLA_DV, ZF_BV), zcols(GLA_DV, ZF_BR),
                  pl.BlockSpec((1, t, 128), lambda bi, h: (bi, 0, ZF_GLR // 128)),
                  pl.BlockSpec((None, 128, GLA_DK), lambda bi, h: (layer, 0, h)),
                  pl.BlockSpec((None, 1, GLA_DK), lambda bi, h: (layer, 0, h)),
                  pl.BlockSpec((None, 1, GLA_DV), lambda bi, h: (layer, 0, 0))],
        out_specs=pl.BlockSpec((1, t, GLA_DV), lambda bi, h: (bi, 0, h)),
        out_shape=jax.ShapeDtypeStruct((b, t, B_V_W), BF16),
        compiler_params=_cparams("parallel", "parallel"), name="gla",
    )(z_f3, z_f3, z_f3, z_f3, z_f3, w_gate2_pad, b_gate.reshape(b_gate.shape[0], 1, -1),
      norm_g.reshape(norm_g.shape[0], 1, -1))


def _rot_half_cols(w):
    half = w.shape[-1] // 2
    return jnp.concatenate([-w[..., half:], w[..., :half]], axis=-1)


def _prep_in_proj(w_in):
    lead = w_in.shape[:-1]
    offs = np.cumsum((A_QK_W, A_QK_W, A_V_W, B_K_W, B_K_W, B_V_W, GLA_GATE_RANK, B_V_W, MLA_Q_RANK,
                      MLA_KV_RANK + MLA_ROPE_DIM))
    a_end = offs[2]
    b_q, b_k, b_v, b_glr, b_r, c_qa, c_kva = [w_in[..., offs[i]:offs[i + 1]] for i in range(2, 9)]
    c_kv, k_pe = c_kva[..., :MLA_KV_RANK], c_kva[..., MLA_KV_RANK:]

    def zeros(width):
        return jnp.zeros(lead + (width,), w_in.dtype)

    def in_rope_lanes(w):
        return jnp.concatenate([zeros(MLA_NOPE_DIM), w, zeros(MLA_HEAD_PAD - MLA_NOPE_DIM - MLA_ROPE_DIM)], axis=-1)

    w_f = jnp.concatenate([c_qa, in_rope_lanes(k_pe), b_q, b_k, b_v, b_r, c_kv, in_rope_lanes(_rot_half_cols(k_pe)),
                           b_glr, zeros(ZF_W - ZF_GLR - GLA_GATE_RANK)], axis=-1)
    assert w_f.shape[-1] == ZF_W
    return w_in[..., :a_end].astype(BF16), w_f.astype(BF16)


def _prep_mla(w_qb, w_kvb):
    lead = w_qb.shape[:-1]
    wq = w_qb.reshape(lead + (MLA_HEADS, MLA_NOPE_DIM + MLA_ROPE_DIM))
    nope, rope = wq[..., :MLA_NOPE_DIM], wq[..., MLA_NOPE_DIM:]
    pad = jnp.zeros(lead + (MLA_HEADS, MLA_HEAD_PAD - MLA_NOPE_DIM - MLA_ROPE_DIM), w_qb.dtype)
    w_q = jnp.concatenate([nope, rope, pad], axis=-1).reshape(lead + (MLA_QK_W,))
    w_qrot = jnp.concatenate([jnp.zeros_like(nope), _rot_half_cols(rope), pad], axis=-1).reshape(lead + (MLA_QK_W,))
    lead = w_kvb.shape[:-1]
    wkv = w_kvb.reshape(lead + (MLA_HEADS, MLA_NOPE_DIM + MLA_V_DIM))
    k_nope, v = wkv[..., :MLA_NOPE_DIM], wkv[..., MLA_NOPE_DIM:]
    w_kn = jnp.concatenate([k_nope, jnp.zeros(lead + (MLA_HEADS, MLA_HEAD_PAD - MLA_NOPE_DIM), w_kvb.dtype)],
                           axis=-1).reshape(lead + (MLA_QK_W,))
    return w_q.astype(BF16), w_qrot.astype(BF16), w_kn.astype(BF16), v.reshape(lead + (C_V_W,)).astype(BF16)


def _cast_gu_kernel(g_ref, u_ref, o_ref):
    tf = g_ref.shape[1]
    o_ref[:, :tf] = g_ref[...].astype(BF16)
    o_ref[:, tf:] = u_ref[...].astype(BF16)


def _prep_ffn_gu(w_gu, tf):
    nl, two, d, f2 = w_gu.shape
    ff = f2 // 2
    assert ff % tf == 0
    nf = ff // tf

    def cols(off):
        return pl.BlockSpec((None, None, d, tf), lambda lj, f: (lj // two, lj % two, 0, off + f))

    return pl.pallas_call(
        _cast_gu_kernel, grid=(nl * two, nf), in_specs=[cols(0), cols(nf)],
        out_specs=pl.BlockSpec((None, None, None, d, 2 * tf), lambda lj, f: (lj // two, lj % two, f, 0, 0)),
        out_shape=jax.ShapeDtypeStruct((nl, two, nf, d, 2 * tf), BF16),
        compiler_params=_cparams("parallel", "parallel"), name="cast_gu",
    )(w_gu, w_gu)


def kernel(x, p, positions, emb_ln_g, emb_ln_b, w_in, w_branch_gate, da_lambda_q1, da_lambda_k1, da_lambda_q2,
           da_lambda_k2, da_subln_g, gla_w_gate2, gla_b_gate, gla_norm_g, mla_q_norm_g, mla_w_qb, mla_kv_norm_g,
           mla_w_kvb, w_o_a, w_o_b, w_o_c, w_out, ffn_w_gu, ffn_w_down, ln_g, ln_b, ple_w_proj, ple_w_gate):
    b, t, d = x.shape
    n = b * t
    depth = w_in.shape[0]
    alpha = float(DEEPNORM_ALPHA)

    w_gu = _prep_ffn_gu(ffn_w_gu, FFN_CHUNK)
    w_down = ffn_w_down.astype(BF16)
    w_a, w_f = _prep_in_proj(w_in)
    w_q, w_qrot, w_kn, w_v = _prep_mla(mla_w_qb, mla_w_kvb)
    w_bg = w_branch_gate.astype(BF16)
    w_oa, w_ob, w_oc, w_o = (w.astype(BF16) for w in (w_o_a, w_o_b, w_o_c, w_out))
    w_pg, w_pp = ple_w_gate.astype(BF16), ple_w_proj.astype(BF16)
    w2_pad = jnp.pad(gla_w_gate2.astype(BF16), ((0, 0), (0, 128 - GLA_GATE_RANK), (0, 0)))
    p = p.reshape(depth, n, -1)

    cos_tab, sin_tab = _rope_tables(positions)
    slopes = jnp.asarray(_alibi_slopes(DA_HEADS))
    no_slopes = jnp.zeros((1,), F32)
    no_lam = jnp.zeros((4, DA_QK_DIM), F32)
    no_gain = jnp.ones((MLA_V_DIM, 1), F32)

    h, hb = _layer_norm(x.reshape(n, d), emb_ln_g, emb_ln_b)
    for i in range(depth):
        h, hb = _ffn_ln(h, hb, w_gu, w_down, i, 0, ln_g[i, 0], ln_b[i, 0], alpha)

        z_a = _matmul(hb, w_a, i, BF16, "in_proj_a").reshape(b, t, -1)
        z_f = _matmul(hb, w_f, i, F32, "in_proj_f")
        lam_init = 0.8 - 0.6 * math.exp(-0.3 * i)
        lam_params = jnp.stack([da_lambda_q1[i], da_lambda_k1[i], da_lambda_q2[i], da_lambda_k2[i]]).astype(F32)
        o_a = _attention(z_a, z_a, z_a, 0, DA_HEADS, 2 * DA_HEADS, 2 * DA_QK_DIM, DA_V_DIM, DA_HEADS,
                         diff=True, score_scale=1.0, slopes=slopes, lam_params=lam_params,
                         gain_col=da_subln_g[i].reshape(-1, 1), lam_init=lam_init, name="diff_attn")
        o_b = _gla(z_f.reshape(b, t, ZF_W), w2_pad, gla_b_gate, gla_norm_g, i)
        q_c, k_c, v_c = _mla_proj(z_f, cos_tab, sin_tab, mla_q_norm_g, mla_kv_norm_g, w_q, w_qrot, w_kn, w_v, i)
        o_c = _attention(q_c.reshape(b, t, -1), k_c.reshape(b, t, -1), v_c.reshape(b, t, -1), 0, 0, 0,
                         MLA_HEAD_PAD, MLA_V_DIM, MLA_HEADS, diff=False,
                         score_scale=(MLA_NOPE_DIM + MLA_ROPE_DIM) ** -0.5, slopes=no_slopes, lam_params=no_lam,
                         gain_col=no_gain, lam_init=0.0, name="mla_attn")
        merged = _merge(hb, o_a.reshape(n, -1), o_b.reshape(n, -1), o_c.reshape(n, -1), w_bg, w_oa, w_ob, w_oc, i)
        h, hb = _proj_ln(h, merged, w_o, i, ln_g[i, 1], ln_b[i, 1], alpha)

        resid = _ple_resid(hb, w_pg, p, w_pp, i, h, alpha)
        h, hb = _ffn_ln(resid, hb, w_gu, w_down, i, 1, ln_g[i, 2], ln_b[i, 2], 1.0)
    return h.reshape(b, t, d)
```

```python
import functools
import math

import numpy as np
import jax
import jax.numpy as jnp
from jax import lax
from jax.experimental import pallas as pl
from jax.experimental.pallas import tpu as pltpu

F32 = jnp.float32
BF16 = jnp.bfloat16

DEPTH = 2
CHUNK = 64
N_BRANCHES = 3
DA_HEADS = 12
DA_QK_DIM = 64
DA_V_DIM = 128
GLA_HEADS = 4
GLA_DK = 128
GLA_DV = 256
GLA_GATE_RANK = 16
GLA_GATE_NORMALIZER = 16.0
MLA_HEADS = 12
MLA_Q_RANK = 768
MLA_KV_RANK = 512
MLA_NOPE_DIM = 128
MLA_ROPE_DIM = 64
MLA_V_DIM = 128
ROPE_THETA = 10000.0
LN_EPS = 1e-5
RMS_EPS = 1e-6
DEEPNORM_ALPHA = (2 * DEPTH) ** 0.25

A_QK_W = DA_HEADS * 2 * DA_QK_DIM
A_V_W = DA_HEADS * DA_V_DIM
B_K_W = GLA_HEADS * GLA_DK
B_V_W = GLA_HEADS * GLA_DV
C_V_W = MLA_HEADS * MLA_V_DIM
MLA_HEAD_PAD = 256
MLA_QK_W = MLA_HEADS * MLA_HEAD_PAD

ZF_CQA = 0
ZF_KPE = 768
ZF_BQ = 1024
ZF_BK = 1536
ZF_BV = 2048
ZF_BR = 3072
ZF_CKV = 4096
ZF_KPER = 4608
ZF_GLR = 4864
ZF_W = 5120

VMEM_CAP_V7X = 64 * 1024 * 1024
VMEM_LIMIT = VMEM_CAP_V7X - 8 * 1024 * 1024
VMEM_LIMIT_FFN = VMEM_CAP_V7X - 4 * 1024 * 1024
FFN_CHUNK = 256
NEG_BIG = -1e30
LOG2E = math.log2(math.e)


def _cparams(*sem, vmem=VMEM_LIMIT):
    return pltpu.CompilerParams(dimension_semantics=sem, vmem_limit_bytes=vmem)


def _dot(a, b):
    return jnp.dot(a, b, preferred_element_type=F32)


def _dot_nt(a, b):
    return lax.dot_general(a, b, (((1,), (1,)), ((), ())), preferred_element_type=F32)


def _dot_tn(a, b):
    return lax.dot_general(a, b, (((0,), (0,)), ((), ())), preferred_element_type=F32)


def _ln_rows(y, g, b):
    mu = jnp.mean(y, axis=-1, keepdims=True)
    yc = y - mu
    var = jnp.mean(yc * yc, axis=-1, keepdims=True)
    return yc * lax.rsqrt(var + LN_EPS) * g + b


def _rms_rows(y, g):
    return y * lax.rsqrt(jnp.mean(y * y, axis=-1, keepdims=True) + RMS_EPS) * g


def _ln_kernel(x_ref, g_ref, b_ref, o_ref, ob_ref):
    y = _ln_rows(x_ref[...], g_ref[...], b_ref[...])
    o_ref[...] = y
    ob_ref[...] = y.astype(BF16)


def _layer_norm(x, g, b):
    n, d = x.shape
    tr = min(256, n)
    row = pl.BlockSpec((tr, d), lambda i: (i, 0))
    vec = pl.BlockSpec((1, d), lambda i: (0, 0))
    return pl.pallas_call(
        _ln_kernel, grid=(n // tr,), in_specs=[row, vec, vec], out_specs=[row, row],
        out_shape=[jax.ShapeDtypeStruct((n, d), F32), jax.ShapeDtypeStruct((n, d), BF16)],
        compiler_params=_cparams("parallel"), name="ln_emb",
    )(x, g.reshape(1, d), b.reshape(1, d))


LN_SLAB = 128


def _resid_copy(resid_hbm, dst_ref, sem, block):
    tm = dst_ref.shape[0]
    start = pl.multiple_of(block * tm, tm)
    return pltpu.make_async_copy(resid_hbm.at[pl.ds(start, tm), :], dst_ref, sem)


def _ln_epilogue(o_ref, ob_ref, g_ref, b_ref, out_scale):
    g = g_ref[...]
    b = b_ref[...]

    def body(r, carry):
        sl = pl.ds(pl.multiple_of(r * LN_SLAB, LN_SLAB), LN_SLAB)
        y = _ln_rows(o_ref[sl, :] * out_scale, g, b)
        o_ref[sl, :] = y
        ob_ref[sl, :] = y.astype(BF16)
        return carry

    lax.fori_loop(0, o_ref.shape[0] // LN_SLAB, body, 0)


def _ffn_kernel(resid_hbm, hb_ref, wgu_ref, wd_ref, g_ref, b_ref, o_ref, ob_ref, sem, *, res_scale):
    f = pl.program_id(1)
    tf = wd_ref.shape[0]

    @pl.when(f == 0)
    def _():
        _resid_copy(resid_hbm, o_ref, sem, pl.program_id(0)).start()

    gu = _dot(hb_ref[...], wgu_ref[...])
    gate = gu[:, :tf]
    act = (gate * jax.nn.sigmoid(gate) * gu[:, tf:]).astype(BF16)

    @pl.when(f == 0)
    def _():
        _resid_copy(resid_hbm, o_ref, sem, pl.program_id(0)).wait()
        o_ref[...] = o_ref[...] * (2.0 * res_scale)

    o_ref[...] += _dot(act, wd_ref[...])

    @pl.when(f == pl.num_programs(1) - 1)
    def _():
        _ln_epilogue(o_ref, ob_ref, g_ref, b_ref, 0.5)


def _ffn_ln(resid, hb, w_gu, w_down, layer, half, ln_g, ln_b, res_scale):
    n, d = hb.shape
    nf, tf = w_gu.shape[2], w_gu.shape[4] // 2
    tm = min(1024, n)
    vec = pl.BlockSpec((1, d), lambda i, f: (0, 0))
    row = pl.BlockSpec((tm, d), lambda i, f: (i, 0), pipeline_mode=pl.Buffered(1))
    return pl.pallas_call(
        functools.partial(_ffn_kernel, res_scale=res_scale),
        grid=(n // tm, nf),
        in_specs=[
            pl.BlockSpec(memory_space=pl.ANY),
            pl.BlockSpec((tm, d), lambda i, f: (i, 0)),
            pl.BlockSpec((None, None, None, d, 2 * tf), lambda i, f: (layer, half, f, 0, 0)),
            pl.BlockSpec((None, None, tf, d), lambda i, f: (layer, half, f, 0)),
            vec, vec,
        ],
        out_specs=[row, row],
        out_shape=[jax.ShapeDtypeStruct((n, d), F32), jax.ShapeDtypeStruct((n, d), BF16)],
        scratch_shapes=[pltpu.SemaphoreType.DMA(())],
        compiler_params=_cparams("parallel", "arbitrary", vmem=VMEM_LIMIT_FFN), name="ffn_ln",
    )(resid, hb, w_gu, w_down, ln_g.reshape(1, d), ln_b.reshape(1, d))


def _proj_ln_kernel(resid_hbm, x_ref, w_ref, g_ref, b_ref, o_ref, ob_ref, r_sc, sem, *, res_scale):
    i = pl.program_id(0)
    k = pl.program_id(1)

    @pl.when((i == 0) & (k == 0))
    def _():
        _resid_copy(resid_hbm, r_sc, sem, 0).start()

    @pl.when((k == 1) & (i + 1 < pl.num_programs(0)))
    def _():
        _resid_copy(resid_hbm, r_sc, sem, i + 1).start()

    @pl.when(k == 0)
    def _():
        _resid_copy(resid_hbm, r_sc, sem, i).wait()
        o_ref[...] = r_sc[...] * res_scale + _dot(x_ref[...], w_ref[...])

    @pl.when(k > 0)
    def _():
        o_ref[...] += _dot(x_ref[...], w_ref[...])

    @pl.when(k == pl.num_programs(1) - 1)
    def _():
        _ln_epilogue(o_ref, ob_ref, g_ref, b_ref, 1.0)


def _proj_ln(resid, x, w, layer, ln_g, ln_b, res_scale):
    n, kdim = x.shape
    d = w.shape[2]
    tm = min(512, n)
    tk = 512
    assert kdim // tk >= 2
    vec = pl.BlockSpec((1, d), lambda i, k: (0, 0))
    row = pl.BlockSpec((tm, d), lambda i, k: (i, 0))
    return pl.pallas_call(
        functools.partial(_proj_ln_kernel, res_scale=res_scale),
        grid=(n // tm, kdim // tk),
        in_specs=[
            pl.BlockSpec(memory_space=pl.ANY),
            pl.BlockSpec((tm, tk), lambda i, k: (i, k)),
            pl.BlockSpec((None, tk, d), lambda i, k: (layer, k, 0)),
            vec, vec,
        ],
        out_specs=[row, row],
        out_shape=[jax.ShapeDtypeStruct((n, d), F32), jax.ShapeDtypeStruct((n, d), BF16)],
        scratch_shapes=[pltpu.VMEM((tm, d), F32), pltpu.SemaphoreType.DMA(())],
        compiler_params=_cparams("arbitrary", "arbitrary"), name="proj_ln",
    )(resid, x, w, ln_g.reshape(1, d), ln_b.reshape(1, d))


def _mm_kernel(x_ref, w_ref, o_ref):
    o_ref[...] = _dot(x_ref[...], w_ref[...]).astype(o_ref.dtype)


def _matmul(x, w, layer, out_dtype, name):
    n, kdim = x.shape
    m = w.shape[2]
    tm = min(1024, n)
    tn = 512
    return pl.pallas_call(
        _mm_kernel, grid=(n // tm, m // tn),
        in_specs=[pl.BlockSpec((tm, kdim), lambda i, j: (i, 0)),
                  pl.BlockSpec((None, kdim, tn), lambda i, j: (layer, 0, j))],
        out_specs=pl.BlockSpec((tm, tn), lambda i, j: (i, j)),
        out_shape=jax.ShapeDtypeStruct((n, m), out_dtype),
        compiler_params=_cparams("parallel", "parallel"), name=name,
    )(x, w)


def _ple_kernel(hb_ref, wg_ref, p_ref, wp_ref, h_ref, o_ref, *, alpha):
    gate = jax.nn.sigmoid(_dot(hb_ref[...], wg_ref[...]))
    proj = _dot(p_ref[...].astype(BF16), wp_ref[...])
    o_ref[...] = alpha * h_ref[...] + gate * proj


def _ple_resid(hb, w_gate, p, w_proj, layer, h, alpha):
    n, d = hb.shape
    pd = p.shape[2]
    tm = min(1024, n)
    tn = 512
    tile = pl.BlockSpec((tm, tn), lambda i, j: (i, j))
    return pl.pallas_call(
        functools.partial(_ple_kernel, alpha=alpha), grid=(n // tm, d // tn),
        in_specs=[
            pl.BlockSpec((tm, d), lambda i, j: (i, 0)),
            pl.BlockSpec((None, d, tn), lambda i, j: (layer, 0, j)),
            pl.BlockSpec((None, tm, pd), lambda i, j: (layer, i, 0)),
            pl.BlockSpec((None, pd, tn), lambda i, j: (layer, 0, j)),
            tile,
        ],
        out_specs=tile,
        out_shape=jax.ShapeDtypeStruct((n, d), F32),
        compiler_params=_cparams("parallel", "parallel"), name="ple_resid",
    )(hb, w_gate, p, w_proj, h)


def _merge_kernel(hb_ref, oa_ref, ob_ref, oc_ref, g0_ref, g1_ref, g2_ref, wa_ref, wb_ref, wc_ref, o_ref):
    x = hb_ref[...]
    acc = jax.nn.sigmoid(_dot(x, g0_ref[...])) * _dot(oa_ref[...], wa_ref[...])
    acc += jax.nn.sigmoid(_dot(x, g1_ref[...])) * _dot(ob_ref[...], wb_ref[...])
    acc += jax.nn.sigmoid(_dot(x, g2_ref[...])) * _dot(oc_ref[...], wc_ref[...])
    o_ref[...] = acc.astype(o_ref.dtype)


def _merge(hb, o_a, o_b, o_c, w_bg, w_oa, w_ob, w_oc, layer):
    n, d = hb.shape
    tm = min(512, n)
    tn = 512
    nj = d // tn

    def rows(w):
        return pl.BlockSpec((tm, w), lambda i, j: (i, 0))

    def cols(kdim, off):
        return pl.BlockSpec((None, kdim, tn), lambda i, j: (layer, 0, off + j))

    return pl.pallas_call(
        _merge_kernel, grid=(n // tm, nj),
        in_specs=[rows(d), rows(o_a.shape[1]), rows(o_b.shape[1]), rows(o_c.shape[1]),
                  cols(d, 0), cols(d, nj), cols(d, 2 * nj),
                  cols(w_oa.shape[1], 0), cols(w_ob.shape[1], 0), cols(w_oc.shape[1], 0)],
        out_specs=pl.BlockSpec((tm, tn), lambda i, j: (i, j)),
        out_shape=jax.ShapeDtypeStruct((n, d), BF16),
        compiler_params=_cparams("parallel", "parallel"), name="merge",
    )(hb, o_a, o_b, o_c, w_bg, w_bg, w_bg, w_oa, w_ob, w_oc)


def _rope_tab_kernel(pos_ref, inv_ref, c_ref, s_ref):
    ang = pos_ref[...].astype(F32) * inv_ref[...]
    lane = lax.broadcasted_iota(jnp.int32, ang.shape, 1)
    rope = (lane >= MLA_NOPE_DIM) & (lane < MLA_NOPE_DIM + MLA_ROPE_DIM)
    c_ref[...] = jnp.where(lane < MLA_NOPE_DIM, 1.0, jnp.where(rope, jnp.cos(ang), 0.0))
    s_ref[...] = jnp.where(rope, jnp.sin(ang), 0.0)


def _rope_tables(positions):
    n = positions.size
    half = MLA_ROPE_DIM // 2
    inv = ROPE_THETA ** (-np.arange(half, dtype=np.float32) / half)
    inv_row = np.zeros((1, MLA_HEAD_PAD), np.float32)
    inv_row[0, MLA_NOPE_DIM:MLA_NOPE_DIM + MLA_ROPE_DIM] = np.concatenate([inv, inv])
    tr = min(512, n)
    tab = pl.BlockSpec((tr, MLA_HEAD_PAD), lambda i: (i, 0))
    return pl.pallas_call(
        _rope_tab_kernel, grid=(n // tr,),
        in_specs=[pl.BlockSpec((tr, 1), lambda i: (i, 0)), pl.BlockSpec((1, MLA_HEAD_PAD), lambda i: (0, 0))],
        out_specs=[tab, tab],
        out_shape=[jax.ShapeDtypeStruct((n, MLA_HEAD_PAD), F32)] * 2,
        compiler_params=_cparams("parallel"), name="rope_tables",
    )(positions.reshape(n, 1), jnp.asarray(inv_row))


def _mla_proj_kernel(cqa_ref, ckv_ref, kpe_ref, kper_ref, c_ref, s_ref, gq_ref, gkv_ref,
                     wq_ref, wqr_ref, wkn_ref, wv_ref, q_out, k_out, v_out):
    cos = c_ref[...]
    sin = s_ref[...]
    xq = _rms_rows(cqa_ref[...], gq_ref[...]).astype(BF16)
    q = _dot(xq, wq_ref[...])
    q_rot = _dot(xq, wqr_ref[...])
    xkv = _rms_rows(ckv_ref[...], gkv_ref[...]).astype(BF16)
    k_nope = _dot(xkv, wkn_ref[...])
    k_rope = kpe_ref[...] * cos + kper_ref[...] * sin
    for h in range(MLA_HEADS):
        sl = slice(h * MLA_HEAD_PAD, (h + 1) * MLA_HEAD_PAD)
        q_out[:, sl] = (q[:, sl] * cos + q_rot[:, sl] * sin).astype(BF16)
        k_out[:, sl] = (k_nope[:, sl] + k_rope).astype(BF16)
    v_out[...] = _dot(xkv, wv_ref[...]).astype(BF16)


def _mla_proj(z_f, cos_tab, sin_tab, g_q, g_kv, w_q, w_qrot, w_kn, w_v, layer):
    n = z_f.shape[0]
    tm = min(256, n)

    def zcols(width, off):
        return pl.BlockSpec((tm, width), lambda i: (i, off // width))

    def whole(a):
        return pl.BlockSpec((None,) + a.shape[1:], lambda i: (layer, 0, 0))

    def rows(width):
        return pl.BlockSpec((tm, width), lambda i: (i, 0))

    g_q = g_q.reshape(g_q.shape[0], 1, -1)
    g_kv = g_kv.reshape(g_kv.shape[0], 1, -1)
    return pl.pallas_call(
        _mla_proj_kernel, grid=(n // tm,),
        in_specs=[zcols(MLA_Q_RANK, ZF_CQA), zcols(MLA_KV_RANK, ZF_CKV), zcols(MLA_HEAD_PAD, ZF_KPE),
                  zcols(MLA_HEAD_PAD, ZF_KPER), rows(MLA_HEAD_PAD), rows(MLA_HEAD_PAD),
                  whole(g_q), whole(g_kv), whole(w_q), whole(w_qrot), whole(w_kn), whole(w_v)],
        out_specs=[rows(MLA_QK_W), rows(MLA_QK_W), rows(C_V_W)],
        out_shape=[jax.ShapeDtypeStruct((n, MLA_QK_W), BF16), jax.ShapeDtypeStruct((n, MLA_QK_W), BF16),
                   jax.ShapeDtypeStruct((n, C_V_W), BF16)],
        compiler_params=_cparams("parallel"), name="mla_proj",
    )(z_f, z_f, z_f, z_f, cos_tab, sin_tab, g_q, g_kv, w_q, w_qrot, w_kn, w_v)


ATTN_TILE = 512
ATTN_HEADS_PER_STEP = 2


def _attn_kernel(slopes_ref, lam_ref, q_ref, k_ref, v_ref, g_ref, o_ref, vt_sc, bias_sc, m_sc, l_sc, acc_sc,
                 *, diff, score_scale, lam_init, dqk, dv):
    t = ATTN_TILE
    hp = ATTN_HEADS_PER_STEP
    qi = pl.program_id(2)
    nt = k_ref.shape[1] // t
    c1 = score_scale * LOG2E
    nmap = 2 if diff else 1
    if diff:
        slope2 = [slopes_ref[pl.program_id(1) * hp + hh] * LOG2E for hh in range(hp)]

    @pl.when(qi == 0)
    def _():
        for hh in range(hp):
            for j in range(nt):
                vt_sc[hh, j] = v_ref[0, j * t:(j + 1) * t, hh * dv:(hh + 1) * dv].astype(F32).T.astype(BF16)
        key = lax.broadcasted_iota(jnp.int32, (t, t), 0)
        qry = lax.broadcasted_iota(jnp.int32, (t, t), 1)
        allowed = (key // CHUNK) <= (qry // CHUNK)
        if diff:
            for hh in range(hp):
                bias_sc[hh, 0] = slope2[hh] * (qry - key).astype(F32)
                bias_sc[hh, 1] = jnp.where(allowed, slope2[hh] * jnp.abs(qry - key).astype(F32), -NEG_BIG)
        else:
            bias_sc[0, 0] = jnp.where(allowed, 0.0, -NEG_BIG)

    qts = []
    for hh in range(hp):
        qt = q_ref[0, :, hh * dqk:(hh + 1) * dqk].astype(F32).T
        if diff:
            feat = lax.broadcasted_iota(jnp.int32, qt.shape, 0)
            qt = qt * (DA_QK_DIM ** -0.5)
            qts.append(jnp.where(feat < DA_QK_DIM, qt, 0.0).astype(BF16))
            qts.append(jnp.where(feat >= DA_QK_DIM, qt, 0.0).astype(BF16))
        else:
            qts.append(qt.astype(BF16))

    m_sc[...] = jnp.full(m_sc.shape, NEG_BIG, F32)
    l_sc[...] = jnp.zeros(l_sc.shape, F32)
    acc_sc[...] = jnp.zeros(acc_sc.shape, F32)

    def step(kj, diag):
        ks = pl.ds(pl.multiple_of(kj * t, t), t)
        scores = []
        for hh in range(hp):
            k = k_ref[0, ks, hh * dqk:(hh + 1) * dqk]
            if diff:
                bias = bias_sc[hh, 1] if diag else bias_sc[hh, 0] + ((qi - kj) * t).astype(F32) * slope2[hh]
            else:
                bias = bias_sc[0, 0] if diag else None
            for i in range(hh * nmap, (hh + 1) * nmap):
                s = _dot(k, qts[i]) * c1
                scores.append(s if bias is None else s - bias)
        probs, alphas = [], []
        for i, s in enumerate(scores):
            m_prev = m_sc[i]
            m_new = jnp.maximum(m_prev, jnp.max(s, axis=0, keepdims=True))
            alpha = jnp.exp2(m_prev - m_new)
            p = jnp.exp2(s - m_new)
            l_sc[i] = alpha * l_sc[i] + jnp.sum(p, axis=0, keepdims=True)
            m_sc[i] = m_new
            probs.append(p.astype(BF16))
            alphas.append(alpha)
        for i, p in enumerate(probs):
            acc_sc[i] = alphas[i] * acc_sc[i] + _dot(vt_sc[i // nmap, kj], p)

    def full_step(kj, carry):
        step(kj, False)
        return carry

    lax.fori_loop(0, qi, full_step, 0)
    step(qi, True)

    if diff:
        lp = lam_ref[...]
        lam = (jnp.exp(jnp.sum(lp[0:1] * lp[1:2], axis=1, keepdims=True))
               - jnp.exp(jnp.sum(lp[2:3] * lp[3:4], axis=1, keepdims=True)) + lam_init)
    for hh in range(hp):
        o = acc_sc[hh * nmap] / l_sc[hh * nmap]
        if diff:
            o = o - lam * (acc_sc[hh * nmap + 1] / l_sc[hh * nmap + 1])
            o = o * lax.rsqrt(jnp.mean(o * o, axis=0, keepdims=True) + RMS_EPS) * g_ref[...] * (1.0 - lam_init)
        o_ref[0, :, hh * dv:(hh + 1) * dv] = o.T.astype(o_ref.dtype)


def _attention(q_arr, k_arr, v_arr, q_off, k_off, v_off, dqk, dv, heads, *, diff, score_scale,
               slopes, lam_params, gain_col, lam_init, name):
    b, t, _ = q_arr.shape
    tq = ATTN_TILE
    hp = ATTN_HEADS_PER_STEP
    assert heads % hp == 0 and q_off % hp == 0 and k_off % hp == 0 and v_off % hp == 0
    nmap = 2 if diff else 1
    nbias = (hp, 2) if diff else (1, 1)
    return pl.pallas_call(
        functools.partial(_attn_kernel, diff=diff, score_scale=score_scale, lam_init=lam_init, dqk=dqk, dv=dv),
        grid=(b, heads // hp, t // tq),
        in_specs=[
            pl.BlockSpec(memory_space=pltpu.SMEM),
            pl.BlockSpec(lam_params.shape, lambda bi, g, i: (0, 0)),
            pl.BlockSpec((1, tq, hp * dqk), lambda bi, g, i: (bi, i, q_off // hp + g)),
            pl.BlockSpec((1, t, hp * dqk), lambda bi, g, i: (bi, 0, k_off // hp + g)),
            pl.BlockSpec((1, t, hp * dv), lambda bi, g, i: (bi, 0, v_off // hp + g)),
            pl.BlockSpec(gain_col.shape, lambda bi, g, i: (0, 0)),
        ],
        out_specs=pl.BlockSpec((1, tq, hp * dv), lambda bi, g, i: (bi, i, g)),
        out_shape=jax.ShapeDtypeStruct((b, t, heads * dv), BF16),
        scratch_shapes=[pltpu.VMEM((hp, t // tq, dv, tq), BF16), pltpu.VMEM(nbias + (tq, tq), F32),
                        pltpu.VMEM((hp * nmap, 1, tq), F32), pltpu.VMEM((hp * nmap, 1, tq), F32),
                        pltpu.VMEM((hp * nmap, dv, tq), F32)],
        compiler_params=_cparams("parallel", "parallel", "arbitrary"), name=name,
    )(slopes, lam_params, q_arr, k_arr, v_arr, gain_col)


def _alibi_slopes(n):
    def pow2_slopes(m):
        start = 2.0 ** (-8.0 / m)
        return [start ** (i + 1) for i in range(m)]
    c = 2 ** int(math.floor(math.log2(n)))
    s = pow2_slopes(c)
    if c < n:
        s = s + pow2_slopes(2 * c)[0::2][: n - c]
    return np.array(s, dtype=np.float32)


GLA_GROUP = 8


def _gla_kernel(q_ref, k_ref, v_ref, r_ref, glr_ref, w2_ref, bg_ref, ng_ref, o_ref):
    c = CHUNK
    gc = GLA_GROUP * c
    scale = GLA_DK ** -0.5
    ri = lax.broadcasted_iota(jnp.int32, (c, c), 0)
    ci = lax.broadcasted_iota(jnp.int32, (c, c), 1)
    causal = ci <= ri
    tri = causal.astype(BF16)
    w2 = w2_ref[...]
    bg = bg_ref[...]
    ng = ng_ref[...]
    chunks = [slice(g * c, (g + 1) * c) for g in range(GLA_GROUP)]

    def body(n, state_t):
        rows = pl.ds(pl.multiple_of(n * gc, gc), gc)
        pre = _dot(glr_ref[0, rows, :].astype(BF16), w2) + bg
        log_a = (jnp.minimum(pre, 0.0) - jnp.log1p(jnp.exp(-jnp.abs(pre)))) / GLA_GATE_NORMALIZER
        hi = log_a.astype(BF16)
        rem = log_a - hi.astype(F32)
        mid = rem.astype(BF16)
        lo = (rem - mid.astype(F32)).astype(BF16)
        bcum = [_dot(tri, hi[s]) + _dot(tri, mid[s]) + _dot(tri, lo[s]) for s in chunks]
        b_mid = [b[c // 2:c // 2 + 1, :] for b in bcum]
        b_last = [b[c - 1:c, :] for b in bcum]
        q = q_ref[0, rows, :] * scale
        k = k_ref[0, rows, :]
        v = v_ref[0, rows, :].astype(BF16)
        att = [_dot_nt((q[s] * jnp.exp(b - bm)).astype(BF16), (k[s] * jnp.exp(bm - b)).astype(BF16))
               for s, b, bm in zip(chunks, bcum, b_mid)]
        att = [jnp.where(causal, a, 0.0).astype(BF16) for a in att]
        o_intra = [_dot(a, v[s]) for a, s in zip(att, chunks)]
        upd_t = [_dot_tn(v[s], (k[s] * jnp.exp(bl - b)).astype(BF16)) for s, b, bl in zip(chunks, bcum, b_last)]
        q_dec = [(q[s] * jnp.exp(b)).astype(BF16) for s, b in zip(chunks, bcum)]
        outs = []
        for g in range(GLA_GROUP):
            outs.append(o_intra[g] + _dot_nt(q_dec[g], state_t.astype(BF16)))
            state_t = state_t * jnp.exp(b_last[g]) + upd_t[g]
        o = jnp.concatenate(outs, axis=0)
        r = r_ref[0, rows, :]
        o_ref[0, rows, :] = (_rms_rows(o, ng) * (r * jax.nn.sigmoid(r))).astype(o_ref.dtype)
        return state_t

    lax.fori_loop(0, q_ref.shape[1] // gc, body, jnp.zeros((GLA_DV, GLA_DK), F32))


def _gla(z_f3, w_gate2_pad, b_gate, norm_g, layer):
    b, t, _ = z_f3.shape

    def zcols(width, off):
        return pl.BlockSpec((1, t, width), lambda bi, h: (bi, 0, off // width + h))

    return pl.pallas_call(
        _gla_kernel, grid=(b, GLA_HEADS),
        in_specs=[zcols(GLA_DK, ZF_BQ), zcols(GLA_DK, ZF_BK), zcols(GLA_DV, ZF_BV), zcols(GLA_DV, ZF_BR),
                  pl.BlockSpec((1, t, 128), lambda bi, h: (bi, 0, ZF_GLR // 128)),
                  pl.BlockSpec((None, 128, GLA_DK), lambda bi, h: (layer, 0, h)),
                  pl.BlockSpec((None, 1, GLA_DK), lambda bi, h: (layer, 0, h)),
                  pl.BlockSpec((None, 1, GLA_DV), lambda bi, h: (layer, 0, 0))],
        out_specs=pl.BlockSpec((1, t, GLA_DV), lambda bi, h: (bi, 0, h)),
        out_shape=jax.ShapeDtypeStruct((b, t, B_V_W), BF16),
        compiler_params=_cparams("parallel", "parallel"), name="gla",
    )(z_f3, z_f3, z_f3, z_f3, z_f3, w_gate2_pad, b_gate.reshape(b_gate.shape[0], 1, -1),
      norm_g.reshape(norm_g.shape[0], 1, -1))


def _rot_half_cols(w):
    half = w.shape[-1] // 2
    return jnp.concatenate([-w[..., half:], w[..., :half]], axis=-1)


def _prep_in_proj(w_in):
    lead = w_in.shape[:-1]
    offs = np.cumsum((A_QK_W, A_QK_W, A_V_W, B_K_W, B_K_W, B_V_W, GLA_GATE_RANK, B_V_W, MLA_Q_RANK,
                      MLA_KV_RANK + MLA_ROPE_DIM))
    a_end = offs[2]
    b_q, b_k, b_v, b_glr, b_r, c_qa, c_kva = [w_in[..., offs[i]:offs[i + 1]] for i in range(2, 9)]
    c_kv, k_pe = c_kva[..., :MLA_KV_RANK], c_kva[..., MLA_KV_RANK:]

    def zeros(width):
        return jnp.zeros(lead + (width,), w_in.dtype)

    def in_rope_lanes(w):
        return jnp.concatenate([zeros(MLA_NOPE_DIM), w, zeros(MLA_HEAD_PAD - MLA_NOPE_DIM - MLA_ROPE_DIM)], axis=-1)

    w_f = jnp.concatenate([c_qa, in_rope_lanes(k_pe), b_q, b_k, b_v, b_r, c_kv, in_rope_lanes(_rot_half_cols(k_pe)),
                           b_glr, zeros(ZF_W - ZF_GLR - GLA_GATE_RANK)], axis=-1)
    assert w_f.shape[-1] == ZF_W
    return w_in[..., :a_end].astype(BF16), w_f.astype(BF16)


def _prep_mla(w_qb, w_kvb):
    lead = w_qb.shape[:-1]
    wq = w_qb.reshape(lead + (MLA_HEADS, MLA_NOPE_DIM + MLA_ROPE_DIM))
    nope, rope = wq[..., :MLA_NOPE_DIM], wq[..., MLA_NOPE_DIM:]
    pad = jnp.zeros(lead + (MLA_HEADS, MLA_HEAD_PAD - MLA_NOPE_DIM - MLA_ROPE_DIM), w_qb.dtype)
    w_q = jnp.concatenate([nope, rope, pad], axis=-1).reshape(lead + (MLA_QK_W,))
    w_qrot = jnp.concatenate([jnp.zeros_like(nope), _rot_half_cols(rope), pad], axis=-1).reshape(lead + (MLA_QK_W,))
    lead = w_kvb.shape[:-1]
    wkv = w_kvb.reshape(lead + (MLA_HEADS, MLA_NOPE_DIM + MLA_V_DIM))
    k_nope, v = wkv[..., :MLA_NOPE_DIM], wkv[..., MLA_NOPE_DIM:]
    w_kn = jnp.concatenate([k_nope, jnp.zeros(lead + (MLA_HEADS, MLA_HEAD_PAD - MLA_NOPE_DIM), w_kvb.dtype)],
                           axis=-1).reshape(lead + (MLA_QK_W,))
    return w_q.astype(BF16), w_qrot.astype(BF16), w_kn.astype(BF16), v.reshape(lead + (C_V_W,)).astype(BF16)


FFN_SRC_BLOCK = 256


def _cast_gu_kernel(*refs, nb, k):
    srcs, o_ref = refs[:-1], refs[-1]
    f = pl.program_id(1)
    sb = srcs[0].shape[1]
    for j, src in enumerate(srcs):
        x = src[...].astype(BF16)
        s = j % k
        if nb % k and s >= nb % k:
            x = jnp.where(f * k + s < nb, x, jnp.zeros_like(x))
        o_ref[:, j * sb:(j + 1) * sb] = x


def _prep_ffn_gu(w_gu, tf):
    nl, two, d, f2 = w_gu.shape
    ff = f2 // 2
    sb = FFN_SRC_BLOCK
    assert ff % sb == 0 and tf % sb == 0
    nb, k = ff // sb, tf // sb
    nf = -(-nb // k)

    def cols(off, s):
        return pl.BlockSpec((None, None, d, sb),
                            lambda lj, f: (lj // two, lj % two, 0, off + jnp.minimum(f * k + s, nb - 1)))

    return pl.pallas_call(
        functools.partial(_cast_gu_kernel, nb=nb, k=k), grid=(nl * two, nf),
        in_specs=[cols(off, s) for off in (0, nb) for s in range(k)],
        out_specs=pl.BlockSpec((None, None, None, d, 2 * tf), lambda lj, f: (lj // two, lj % two, f, 0, 0)),
        out_shape=jax.ShapeDtypeStruct((nl, two, nf, d, 2 * tf), BF16),
        compiler_params=_cparams("parallel", "parallel"), name="cast_gu",
    )(*([w_gu] * (2 * k)))


def kernel(x, p, positions, emb_ln_g, emb_ln_b, w_in, w_branch_gate, da_lambda_q1, da_lambda_k1, da_lambda_q2,
           da_lambda_k2, da_subln_g, gla_w_gate2, gla_b_gate, gla_norm_g, mla_q_norm_g, mla_w_qb, mla_kv_norm_g,
           mla_w_kvb, w_o_a, w_o_b, w_o_c, w_out, ffn_w_gu, ffn_w_down, ln_g, ln_b, ple_w_proj, ple_w_gate):
    b, t, d = x.shape
    n = b * t
    depth = w_in.shape[0]
    alpha = float(DEEPNORM_ALPHA)

    w_gu = _prep_ffn_gu(ffn_w_gu, FFN_CHUNK)
    ff_pad = w_gu.shape[2] * FFN_CHUNK - ffn_w_down.shape[2]
    w_down = jnp.pad(ffn_w_down.astype(BF16), ((0, 0), (0, 0), (0, ff_pad), (0, 0)))
    w_a, w_f = _prep_in_proj(w_in)
    w_q, w_qrot, w_kn, w_v = _prep_mla(mla_w_qb, mla_w_kvb)
    w_bg = w_branch_gate.astype(BF16)
    w_oa, w_ob, w_oc, w_o = (w.astype(BF16) for w in (w_o_a, w_o_b, w_o_c, w_out))
    w_pg, w_pp = ple_w_gate.astype(BF16), ple_w_proj.astype(BF16)
    w2_pad = jnp.pad(gla_w_gate2.astype(BF16), ((0, 0), (0, 128 - GLA_GATE_RANK), (0, 0)))
    p = p.reshape(depth, n, -1)

    cos_tab, sin_tab = _rope_tables(positions)
    slopes = jnp.asarray(_alibi_slopes(DA_HEADS))
    no_slopes = jnp.zeros((1,), F32)
    no_lam = jnp.zeros((4, DA_QK_DIM), F32)
    no_gain = jnp.ones((MLA_V_DIM, 1), F32)

    h, hb = _layer_norm(x.reshape(n, d), emb_ln_g, emb_ln_b)
    for i in range(depth):
        h, hb = _ffn_ln(h, hb, w_gu, w_down, i, 0, ln_g[i, 0], ln_b[i, 0], alpha)

        z_a = _matmul(hb, w_a, i, BF16, "in_proj_a").reshape(b, t, -1)
        z_f = _matmul(hb, w_f, i, F32, "in_proj_f")
        lam_init = 0.8 - 0.6 * math.exp(-0.3 * i)
        lam_params = jnp.stack([da_lambda_q1[i], da_lambda_k1[i], da_lambda_q2[i], da_lambda_k2[i]]).astype(F32)
        o_a = _attention(z_a, z_a, z_a, 0, DA_HEADS, 2 * DA_HEADS, 2 * DA_QK_DIM, DA_V_DIM, DA_HEADS,
                         diff=True, score_scale=1.0, slopes=slopes, lam_params=lam_params,
                         gain_col=da_subln_g[i].reshape(-1, 1), lam_init=lam_init, name="diff_attn")
        o_b = _gla(z_f.reshape(b, t, ZF_W), w2_pad, gla_b_gate, gla_norm_g, i)
        q_c, k_c, v_c = _mla_proj(z_f, cos_tab, sin_tab, mla_q_norm_g, mla_kv_norm_g, w_q, w_qrot, w_kn, w_v, i)
        o_c = _attention(q_c.reshape(b, t, -1), k_c.reshape(b, t, -1), v_c.reshape(b, t, -1), 0, 0, 0,
                         MLA_HEAD_PAD, MLA_V_DIM, MLA_HEADS, diff=False,
                         score_scale=(MLA_NOPE_DIM + MLA_ROPE_DIM) ** -0.5, slopes=no_slopes, lam_params=no_lam,
                         gain_col=no_gain, lam_init=0.0, name="mla_attn")
        merged = _merge(hb, o_a.reshape(n, -1), o_b.reshape(n, -1), o_c.reshape(n, -1), w_bg, w_oa, w_ob, w_oc, i)
        h, hb = _proj_ln(h, merged, w_o, i, ln_g[i, 1], ln_b[i, 1], alpha)

        resid = _ple_resid(hb, w_pg, p, w_pp, i, h, alpha)
        h, hb = _ffn_ln(resid, hb, w_gu, w_down, i, 1, ln_g[i, 2], ln_b[i, 2], 1.0)
    return h.reshape(b, t, d)
```

```python
import functools
import math

import numpy as np
import jax
import jax.numpy as jnp
from jax import lax
from jax.experimental import pallas as pl
from jax.experimental.pallas import tpu as pltpu

F32 = jnp.float32
BF16 = jnp.bfloat16

DEPTH = 2
CHUNK = 64
N_BRANCHES = 3
DA_HEADS = 12
DA_QK_DIM = 64
DA_V_DIM = 128
GLA_HEADS = 4
GLA_DK = 128
GLA_DV = 256
GLA_GATE_RANK = 16
GLA_GATE_NORMALIZER = 16.0
MLA_HEADS = 12
MLA_Q_RANK = 768
MLA_KV_RANK = 512
MLA_NOPE_DIM = 128
MLA_ROPE_DIM = 64
MLA_V_DIM = 128
ROPE_THETA = 10000.0
LN_EPS = 1e-5
RMS_EPS = 1e-6
DEEPNORM_ALPHA = (2 * DEPTH) ** 0.25

A_QK_W = DA_HEADS * 2 * DA_QK_DIM
A_V_W = DA_HEADS * DA_V_DIM
B_K_W = GLA_HEADS * GLA_DK
B_V_W = GLA_HEADS * GLA_DV
C_V_W = MLA_HEADS * MLA_V_DIM
MLA_HEAD_PAD = 256
MLA_QK_W = MLA_HEADS * MLA_HEAD_PAD

ZF_CQA = 0
ZF_KPE = 768
ZF_BQ = 1024
ZF_BK = 1536
ZF_BV = 2048
ZF_BR = 3072
ZF_CKV = 4096
ZF_KPER = 4608
ZF_GLR = 4864
ZF_W = 5120

VMEM_CAP_V7X = 64 * 1024 * 1024
VMEM_LIMIT = VMEM_CAP_V7X - 8 * 1024 * 1024
VMEM_LIMIT_FFN = VMEM_CAP_V7X - 4 * 1024 * 1024
FFN_CHUNK = 256
FFN_ROW_GROUPS = 2
NEG_BIG = -1e30
LOG2E = math.log2(math.e)


def _cparams(*sem, vmem=VMEM_LIMIT):
    return pltpu.CompilerParams(dimension_semantics=sem, vmem_limit_bytes=vmem)


def _dot(a, b):
    return jnp.dot(a, b, preferred_element_type=F32)


def _dot_nt(a, b):
    return lax.dot_general(a, b, (((1,), (1,)), ((), ())), preferred_element_type=F32)


def _dot_tn(a, b):
    return lax.dot_general(a, b, (((0,), (0,)), ((), ())), preferred_element_type=F32)


def _ln_rows(y, g, b):
    mu = jnp.mean(y, axis=-1, keepdims=True)
    yc = y - mu
    var = jnp.mean(yc * yc, axis=-1, keepdims=True)
    return yc * lax.rsqrt(var + LN_EPS) * g + b


def _rms_rows(y, g):
    return y * lax.rsqrt(jnp.mean(y * y, axis=-1, keepdims=True) + RMS_EPS) * g


def _ln_kernel(x_ref, g_ref, b_ref, o_ref, ob_ref):
    y = _ln_rows(x_ref[...], g_ref[...], b_ref[...])
    o_ref[...] = y
    ob_ref[...] = y.astype(BF16)


def _layer_norm(x, g, b):
    n, d = x.shape
    tr = min(256, n)
    row = pl.BlockSpec((tr, d), lambda i: (i, 0))
    vec = pl.BlockSpec((1, d), lambda i: (0, 0))
    return pl.pallas_call(
        _ln_kernel, grid=(n // tr,), in_specs=[row, vec, vec], out_specs=[row, row],
        out_shape=[jax.ShapeDtypeStruct((n, d), F32), jax.ShapeDtypeStruct((n, d), BF16)],
        compiler_params=_cparams("parallel"), name="ln_emb",
    )(x, g.reshape(1, d), b.reshape(1, d))


LN_SLAB = 128


def _resid_copy(resid_hbm, dst_ref, sem, block):
    tm = dst_ref.shape[0]
    start = pl.multiple_of(block * tm, tm)
    return pltpu.make_async_copy(resid_hbm.at[pl.ds(start, tm), :], dst_ref, sem)


def _ln_epilogue(o_ref, ob_ref, g_ref, b_ref, out_scale):
    g = g_ref[...]
    b = b_ref[...]

    def body(r, carry):
        sl = pl.ds(pl.multiple_of(r * LN_SLAB, LN_SLAB), LN_SLAB)
        y = _ln_rows(o_ref[sl, :] * out_scale, g, b)
        o_ref[sl, :] = y
        ob_ref[sl, :] = y.astype(BF16)
        return carry

    lax.fori_loop(0, o_ref.shape[0] // LN_SLAB, body, 0)


def _ffn_kernel(resid_hbm, hb_ref, wgu_ref, wd_ref, g_ref, b_ref, o_ref, ob_ref, sem, *, res_scale):
    f = pl.program_id(1)
    tf = wd_ref.shape[0]

    @pl.when(f == 0)
    def _():
        cp = _resid_copy(resid_hbm, o_ref, sem, pl.program_id(0))
        cp.start()
        cp.wait()
        o_ref[...] = o_ref[...] * (2.0 * res_scale)

    rows = o_ref.shape[0] // FFN_ROW_GROUPS
    groups = [slice(r * rows, (r + 1) * rows) for r in range(FFN_ROW_GROUPS)]
    gus = [_dot(hb_ref[g, :], wgu_ref[...]) for g in groups]
    acts = [(gu[:, :tf] * jax.nn.sigmoid(gu[:, :tf]) * gu[:, tf:]).astype(BF16) for gu in gus]
    for g, act in zip(groups, acts):
        o_ref[g, :] += _dot(act, wd_ref[...])

    @pl.when(f == pl.num_programs(1) - 1)
    def _():
        _ln_epilogue(o_ref, ob_ref, g_ref, b_ref, 0.5)


def _ffn_ln(resid, hb, w_gu, w_down, layer, half, ln_g, ln_b, res_scale):
    n, d = hb.shape
    nf, tf = w_gu.shape[2], w_gu.shape[4] // 2
    tm = min(1024, n)
    vec = pl.BlockSpec((1, d), lambda i, f: (0, 0))
    row = pl.BlockSpec((tm, d), lambda i, f: (i, 0), pipeline_mode=pl.Buffered(1))
    return pl.pallas_call(
        functools.partial(_ffn_kernel, res_scale=res_scale),
        grid=(n // tm, nf),
        in_specs=[
            pl.BlockSpec(memory_space=pl.ANY),
            pl.BlockSpec((tm, d), lambda i, f: (i, 0)),
            pl.BlockSpec((None, None, None, d, 2 * tf), lambda i, f: (layer, half, f, 0, 0)),
            pl.BlockSpec((None, None, tf, d), lambda i, f: (layer, half, f, 0)),
            vec, vec,
        ],
        out_specs=[row, row],
        out_shape=[jax.ShapeDtypeStruct((n, d), F32), jax.ShapeDtypeStruct((n, d), BF16)],
        scratch_shapes=[pltpu.SemaphoreType.DMA(())],
        compiler_params=_cparams("parallel", "arbitrary", vmem=VMEM_LIMIT_FFN), name="ffn_ln",
    )(resid, hb, w_gu, w_down, ln_g.reshape(1, d), ln_b.reshape(1, d))


def _proj_ln_kernel(resid_hbm, x_ref, w_ref, g_ref, b_ref, o_ref, ob_ref, r_sc, sem, *, res_scale):
    i = pl.program_id(0)
    k = pl.program_id(1)

    @pl.when((i == 0) & (k == 0))
    def _():
        _resid_copy(resid_hbm, r_sc, sem, 0).start()

    @pl.when((k == 1) & (i + 1 < pl.num_programs(0)))
    def _():
        _resid_copy(resid_hbm, r_sc, sem, i + 1).start()

    @pl.when(k == 0)
    def _():
        _resid_copy(resid_hbm, r_sc, sem, i).wait()
        o_ref[...] = r_sc[...] * res_scale + _dot(x_ref[...], w_ref[...])

    @pl.when(k > 0)
    def _():
        o_ref[...] += _dot(x_ref[...], w_ref[...])

    @pl.when(k == pl.num_programs(1) - 1)
    def _():
        _ln_epilogue(o_ref, ob_ref, g_ref, b_ref, 1.0)


def _proj_ln(resid, x, w, layer, ln_g, ln_b, res_scale):
    n, kdim = x.shape
    d = w.shape[2]
    tm = min(512, n)
    tk = 512
    assert kdim // tk >= 2
    vec = pl.BlockSpec((1, d), lambda i, k: (0, 0))
    row = pl.BlockSpec((tm, d), lambda i, k: (i, 0))
    return pl.pallas_call(
        functools.partial(_proj_ln_kernel, res_scale=res_scale),
        grid=(n // tm, kdim // tk),
        in_specs=[
            pl.BlockSpec(memory_space=pl.ANY),
            pl.BlockSpec((tm, tk), lambda i, k: (i, k)),
            pl.BlockSpec((None, tk, d), lambda i, k: (layer, k, 0)),
            vec, vec,
        ],
        out_specs=[row, row],
        out_shape=[jax.ShapeDtypeStruct((n, d), F32), jax.ShapeDtypeStruct((n, d), BF16)],
        scratch_shapes=[pltpu.VMEM((tm, d), F32), pltpu.SemaphoreType.DMA(())],
        compiler_params=_cparams("arbitrary", "arbitrary"), name="proj_ln",
    )(resid, x, w, ln_g.reshape(1, d), ln_b.reshape(1, d))


def _mm_kernel(x_ref, w_ref, o_ref):
    o_ref[...] = _dot(x_ref[...], w_ref[...]).astype(o_ref.dtype)


def _matmul(x, w, layer, out_dtype, name):
    n, kdim = x.shape
    m = w.shape[2]
    tm = min(1024, n)
    tn = 512
    return pl.pallas_call(
        _mm_kernel, grid=(n // tm, m // tn),
        in_specs=[pl.BlockSpec((tm, kdim), lambda i, j: (i, 0)),
                  pl.BlockSpec((None, kdim, tn), lambda i, j: (layer, 0, j))],
        out_specs=pl.BlockSpec((tm, tn), lambda i, j: (i, j)),
        out_shape=jax.ShapeDtypeStruct((n, m), out_dtype),
        compiler_params=_cparams("parallel", "parallel"), name=name,
    )(x, w)


def _ple_kernel(hb_ref, wg_ref, p_ref, wp_ref, h_ref, o_ref, *, alpha):
    gate = jax.nn.sigmoid(_dot(hb_ref[...], wg_ref[...]))
    proj = _dot(p_ref[...].astype(BF16), wp_ref[...])
    o_ref[...] = alpha * h_ref[...] + gate * proj


def _ple_resid(hb, w_gate, p, w_proj, layer, h, alpha):
    n, d = hb.shape
    pd = p.shape[2]
    tm = min(1024, n)
    tn = 512
    tile = pl.BlockSpec((tm, tn), lambda i, j: (i, j))
    return pl.pallas_call(
        functools.partial(_ple_kernel, alpha=alpha), grid=(n // tm, d // tn),
        in_specs=[
            pl.BlockSpec((tm, d), lambda i, j: (i, 0)),
            pl.BlockSpec((None, d, tn), lambda i, j: (layer, 0, j)),
            pl.BlockSpec((None, tm, pd), lambda i, j: (layer, i, 0)),
            pl.BlockSpec((None, pd, tn), lambda i, j: (layer, 0, j)),
            tile,
        ],
        out_specs=tile,
        out_shape=jax.ShapeDtypeStruct((n, d), F32),
        compiler_params=_cparams("parallel", "parallel"), name="ple_resid",
    )(hb, w_gate, p, w_proj, h)


def _merge_kernel(hb_ref, oa_ref, ob_ref, oc_ref, g0_ref, g1_ref, g2_ref, wa_ref, wb_ref, wc_ref, o_ref):
    x = hb_ref[...]
    acc = jax.nn.sigmoid(_dot(x, g0_ref[...])) * _dot(oa_ref[...], wa_ref[...])
    acc += jax.nn.sigmoid(_dot(x, g1_ref[...])) * _dot(ob_ref[...], wb_ref[...])
    acc += jax.nn.sigmoid(_dot(x, g2_ref[...])) * _dot(oc_ref[...], wc_ref[...])
    o_ref[...] = acc.astype(o_ref.dtype)


def _merge(hb, o_a, o_b, o_c, w_bg, w_oa, w_ob, w_oc, layer):
    n, d = hb.shape
    tm = min(512, n)
    tn = 512
    nj = d // tn

    def rows(w):
        return pl.BlockSpec((tm, w), lambda i, j: (i, 0))

    def cols(kdim, off):
        return pl.BlockSpec((None, kdim, tn), lambda i, j: (layer, 0, off + j))

    return pl.pallas_call(
        _merge_kernel, grid=(n // tm, nj),
        in_specs=[rows(d), rows(o_a.shape[1]), rows(o_b.shape[1]), rows(o_c.shape[1]),
                  cols(d, 0), cols(d, nj), cols(d, 2 * nj),
                  cols(w_oa.shape[1], 0), cols(w_ob.shape[1], 0), cols(w_oc.shape[1], 0)],
        out_specs=pl.BlockSpec((tm, tn), lambda i, j: (i, j)),
        out_shape=jax.ShapeDtypeStruct((n, d), BF16),
        compiler_params=_cparams("parallel", "parallel"), name="merge",
    )(hb, o_a, o_b, o_c, w_bg, w_bg, w_bg, w_oa, w_ob, w_oc)


def _rope_tab_kernel(pos_ref, inv_ref, c_ref, s_ref):
    ang = pos_ref[...].astype(F32) * inv_ref[...]
    lane = lax.broadcasted_iota(jnp.int32, ang.shape, 1)
    rope = (lane >= MLA_NOPE_DIM) & (lane < MLA_NOPE_DIM + MLA_ROPE_DIM)
    c_ref[...] = jnp.where(lane < MLA_NOPE_DIM, 1.0, jnp.where(rope, jnp.cos(ang), 0.0))
    s_ref[...] = jnp.where(rope, jnp.sin(ang), 0.0)


def _rope_tables(positions):
    n = positions.size
    half = MLA_ROPE_DIM // 2
    inv = ROPE_THETA ** (-np.arange(half, dtype=np.float32) / half)
    inv_row = np.zeros((1, MLA_HEAD_PAD), np.float32)
    inv_row[0, MLA_NOPE_DIM:MLA_NOPE_DIM + MLA_ROPE_DIM] = np.concatenate([inv, inv])
    tr = min(512, n)
    tab = pl.BlockSpec((tr, MLA_HEAD_PAD), lambda i: (i, 0))
    return pl.pallas_call(
        _rope_tab_kernel, grid=(n // tr,),
        in_specs=[pl.BlockSpec((tr, 1), lambda i: (i, 0)), pl.BlockSpec((1, MLA_HEAD_PAD), lambda i: (0, 0))],
        out_specs=[tab, tab],
        out_shape=[jax.ShapeDtypeStruct((n, MLA_HEAD_PAD), F32)] * 2,
        compiler_params=_cparams("parallel"), name="rope_tables",
    )(positions.reshape(n, 1), jnp.asarray(inv_row))


def _mla_proj_kernel(cqa_ref, ckv_ref, kpe_ref, kper_ref, c_ref, s_ref, gq_ref, gkv_ref,
                     wq_ref, wqr_ref, wkn_ref, wv_ref, q_out, k_out, v_out):
    cos = c_ref[...]
    sin = s_ref[...]
    xq = _rms_rows(cqa_ref[...], gq_ref[...]).astype(BF16)
    q = _dot(xq, wq_ref[...])
    q_rot = _dot(xq, wqr_ref[...])
    xkv = _rms_rows(ckv_ref[...], gkv_ref[...]).astype(BF16)
    k_nope = _dot(xkv, wkn_ref[...])
    k_rope = kpe_ref[...] * cos + kper_ref[...] * sin
    for h in range(MLA_HEADS):
        sl = slice(h * MLA_HEAD_PAD, (h + 1) * MLA_HEAD_PAD)
        q_out[:, sl] = (q[:, sl] * cos + q_rot[:, sl] * sin).astype(BF16)
        k_out[:, sl] = (k_nope[:, sl] + k_rope).astype(BF16)
    v_out[...] = _dot(xkv, wv_ref[...]).astype(BF16)


def _mla_proj(z_f, cos_tab, sin_tab, g_q, g_kv, w_q, w_qrot, w_kn, w_v, layer):
    n = z_f.shape[0]
    tm = min(256, n)

    def zcols(width, off):
        return pl.BlockSpec((tm, width), lambda i: (i, off // width))

    def whole(a):
        return pl.BlockSpec((None,) + a.shape[1:], lambda i: (layer, 0, 0))

    def rows(width):
        return pl.BlockSpec((tm, width), lambda i: (i, 0))

    g_q = g_q.reshape(g_q.shape[0], 1, -1)
    g_kv = g_kv.reshape(g_kv.shape[0], 1, -1)
    return pl.pallas_call(
        _mla_proj_kernel, grid=(n // tm,),
        in_specs=[zcols(MLA_Q_RANK, ZF_CQA), zcols(MLA_KV_RANK, ZF_CKV), zcols(MLA_HEAD_PAD, ZF_KPE),
                  zcols(MLA_HEAD_PAD, ZF_KPER), rows(MLA_HEAD_PAD), rows(MLA_HEAD_PAD),
                  whole(g_q), whole(g_kv), whole(w_q), whole(w_qrot), whole(w_kn), whole(w_v)],
        out_specs=[rows(MLA_QK_W), rows(MLA_QK_W), rows(C_V_W)],
        out_shape=[jax.ShapeDtypeStruct((n, MLA_QK_W), BF16), jax.ShapeDtypeStruct((n, MLA_QK_W), BF16),
                   jax.ShapeDtypeStruct((n, C_V_W), BF16)],
        compiler_params=_cparams("parallel"), name="mla_proj",
    )(z_f, z_f, z_f, z_f, cos_tab, sin_tab, g_q, g_kv, w_q, w_qrot, w_kn, w_v)


ATTN_TILE = 512
ATTN_MAPS_PER_STEP = 4


def _attn_kernel(slopes_ref, lam_ref, q_ref, k_ref, v_ref, g_ref, o_ref, vt_sc, bias_sc, m_sc, l_sc, acc_sc,
                 *, diff, score_scale, lam_init, dqk, dv):
    t = ATTN_TILE
    nmap = 2 if diff else 1
    hp = ATTN_MAPS_PER_STEP // nmap
    qi = pl.program_id(2)
    nt = k_ref.shape[1] // t
    c1 = score_scale * LOG2E
    if diff:
        slope2 = [slopes_ref[pl.program_id(1) * hp + hh] * LOG2E for hh in range(hp)]

    @pl.when(qi == 0)
    def _():
        for hh in range(hp):
            for j in range(nt):
                vt_sc[hh, j] = v_ref[0, j * t:(j + 1) * t, hh * dv:(hh + 1) * dv].astype(F32).T.astype(BF16)
        key = lax.broadcasted_iota(jnp.int32, (t, t), 0)
        qry = lax.broadcasted_iota(jnp.int32, (t, t), 1)
        allowed = (key // CHUNK) <= (qry // CHUNK)
        if diff:
            for hh in range(hp):
                bias_sc[hh, 0] = slope2[hh] * (qry - key).astype(F32)
                bias_sc[hh, 1] = jnp.where(allowed, slope2[hh] * jnp.abs(qry - key).astype(F32), -NEG_BIG)
        else:
            bias_sc[0, 0] = jnp.where(allowed, 0.0, -NEG_BIG)

    qts = []
    for hh in range(hp):
        qt = q_ref[0, :, hh * dqk:(hh + 1) * dqk].astype(F32).T
        if diff:
            feat = lax.broadcasted_iota(jnp.int32, qt.shape, 0)
            qt = qt * (DA_QK_DIM ** -0.5)
            qts.append(jnp.where(feat < DA_QK_DIM, qt, 0.0).astype(BF16))
            qts.append(jnp.where(feat >= DA_QK_DIM, qt, 0.0).astype(BF16))
        else:
            qts.append(qt.astype(BF16))

    m_sc[...] = jnp.full(m_sc.shape, NEG_BIG, F32)
    l_sc[...] = jnp.zeros(l_sc.shape, F32)
    acc_sc[...] = jnp.zeros(acc_sc.shape, F32)

    def step(kj, diag):
        ks = pl.ds(pl.multiple_of(kj * t, t), t)
        scores = []
        for hh in range(hp):
            k = k_ref[0, ks, hh * dqk:(hh + 1) * dqk]
            if diff:
                bias = bias_sc[hh, 1] if diag else bias_sc[hh, 0] + ((qi - kj) * t).astype(F32) * slope2[hh]
            else:
                bias = bias_sc[0, 0] if diag else None
            for i in range(hh * nmap, (hh + 1) * nmap):
                s = _dot(k, qts[i]) * c1
                scores.append(s if bias is None else s - bias)
        probs, alphas = [], []
        for i, s in enumerate(scores):
            m_prev = m_sc[i]
            m_new = jnp.maximum(m_prev, jnp.max(s, axis=0, keepdims=True))
            alpha = jnp.exp2(m_prev - m_new)
            p = jnp.exp2(s - m_new)
            l_sc[i] = alpha * l_sc[i] + jnp.sum(p, axis=0, keepdims=True)
            m_sc[i] = m_new
            probs.append(p.astype(BF16))
            alphas.append(alpha)
        for i, p in enumerate(probs):
            acc_sc[i] = alphas[i] * acc_sc[i] + _dot(vt_sc[i // nmap, kj], p)

    def full_step(kj, carry):
        step(kj, False)
        return carry

    lax.fori_loop(0, qi, full_step, 0)
    step(qi, True)

    if diff:
        lp = lam_ref[...]
        lam = (jnp.exp(jnp.sum(lp[0:1] * lp[1:2], axis=1, keepdims=True))
               - jnp.exp(jnp.sum(lp[2:3] * lp[3:4], axis=1, keepdims=True)) + lam_init)
    for hh in range(hp):
        o = acc_sc[hh * nmap] / l_sc[hh * nmap]
        if diff:
            o = o - lam * (acc_sc[hh * nmap + 1] / l_sc[hh * nmap + 1])
            o = o * lax.rsqrt(jnp.mean(o * o, axis=0, keepdims=True) + RMS_EPS) * g_ref[...] * (1.0 - lam_init)
        o_ref[0, :, hh * dv:(hh + 1) * dv] = o.T.astype(o_ref.dtype)


def _attention(q_arr, k_arr, v_arr, q_off, k_off, v_off, dqk, dv, heads, *, diff, score_scale,
               slopes, lam_params, gain_col, lam_init, name):
    b, t, _ = q_arr.shape
    tq = ATTN_TILE
    nmap = 2 if diff else 1
    hp = ATTN_MAPS_PER_STEP // nmap
    assert heads % hp == 0 and q_off % hp == 0 and k_off % hp == 0 and v_off % hp == 0
    nbias = (hp, 2) if diff else (1, 1)
    return pl.pallas_call(
        functools.partial(_attn_kernel, diff=diff, score_scale=score_scale, lam_init=lam_init, dqk=dqk, dv=dv),
        grid=(b, heads // hp, t // tq),
        in_specs=[
            pl.BlockSpec(memory_space=pltpu.SMEM),
            pl.BlockSpec(lam_params.shape, lambda bi, g, i: (0, 0)),
            pl.BlockSpec((1, tq, hp * dqk), lambda bi, g, i: (bi, i, q_off // hp + g)),
            pl.BlockSpec((1, t, hp * dqk), lambda bi, g, i: (bi, 0, k_off // hp + g)),
            pl.BlockSpec((1, t, hp * dv), lambda bi, g, i: (bi, 0, v_off // hp + g)),
            pl.BlockSpec(gain_col.shape, lambda bi, g, i: (0, 0)),
        ],
        out_specs=pl.BlockSpec((1, tq, hp * dv), lambda bi, g, i: (bi, i, g)),
        out_shape=jax.ShapeDtypeStruct((b, t, heads * dv), BF16),
        scratch_shapes=[pltpu.VMEM((hp, t // tq, dv, tq), BF16), pltpu.VMEM(nbias + (tq, tq), F32),
                        pltpu.VMEM((hp * nmap, 1, tq), F32), pltpu.VMEM((hp * nmap, 1, tq), F32),
                        pltpu.VMEM((hp * nmap, dv, tq), F32)],
        compiler_params=_cparams("parallel", "parallel", "arbitrary"), name=name,
    )(slopes, lam_params, q_arr, k_arr, v_arr, gain_col)


def _alibi_slopes(n):
    def pow2_slopes(m):
        start = 2.0 ** (-8.0 / m)
        return [start ** (i + 1) for i in range(m)]
    c = 2 ** int(math.floor(math.log2(n)))
    s = pow2_slopes(c)
    if c < n:
        s = s + pow2_slopes(2 * c)[0::2][: n - c]
    return np.array(s, dtype=np.float32)


GLA_GROUP = 8


def _gla_kernel(q_ref, k_ref, v_ref, r_ref, glr_ref, w2_ref, bg_ref, ng_ref, o_ref):
    c = CHUNK
    gc = GLA_GROUP * c
    scale = GLA_DK ** -0.5
    ri = lax.broadcasted_iota(jnp.int32, (c, c), 0)
    ci = lax.broadcasted_iota(jnp.int32, (c, c), 1)
    causal = ci <= ri
    tri = causal.astype(BF16)
    w2 = w2_ref[...]
    bg = bg_ref[...]
    ng = ng_ref[...]
    chunks = [slice(g * c, (g + 1) * c) for g in range(GLA_GROUP)]

    def body(n, state_t):
        rows = pl.ds(pl.multiple_of(n * gc, gc), gc)
        pre = _dot(glr_ref[0, rows, :].astype(BF16), w2) + bg
        log_a = (jnp.minimum(pre, 0.0) - jnp.log1p(jnp.exp(-jnp.abs(pre)))) / GLA_GATE_NORMALIZER
        hi = log_a.astype(BF16)
        rem = log_a - hi.astype(F32)
        mid = rem.astype(BF16)
        lo = (rem - mid.astype(F32)).astype(BF16)
        bcum = [_dot(tri, hi[s]) + _dot(tri, mid[s]) + _dot(tri, lo[s]) for s in chunks]
        b_mid = [b[c // 2:c // 2 + 1, :] for b in bcum]
        b_last = [b[c - 1:c, :] for b in bcum]
        q = q_ref[0, rows, :] * scale
        k = k_ref[0, rows, :]
        v = v_ref[0, rows, :].astype(BF16)
        att = [_dot_nt((q[s] * jnp.exp(b - bm)).astype(BF16), (k[s] * jnp.exp(bm - b)).astype(BF16))
               for s, b, bm in zip(chunks, bcum, b_mid)]
        att = [jnp.where(causal, a, 0.0).astype(BF16) for a in att]
        o_intra = [_dot(a, v[s]) for a, s in zip(att, chunks)]
        upd_t = [_dot_tn(v[s], (k[s] * jnp.exp(bl - b)).astype(BF16)) for s, b, bl in zip(chunks, bcum, b_last)]
        q_dec = [(q[s] * jnp.exp(b)).astype(BF16) for s, b in zip(chunks, bcum)]
        outs = []
        for g in range(GLA_GROUP):
            outs.append(o_intra[g] + _dot_nt(q_dec[g], state_t.astype(BF16)))
            state_t = state_t * jnp.exp(b_last[g]) + upd_t[g]
        o = jnp.concatenate(outs, axis=0)
        r = r_ref[0, rows, :]
        o_ref[0, rows, :] = (_rms_rows(o, ng) * (r * jax.nn.sigmoid(r))).astype(o_ref.dtype)
        return state_t

    lax.fori_loop(0, q_ref.shape[1] // gc, body, jnp.zeros((GLA_DV, GLA_DK), F32))


def _gla(z_f3, w_gate2_pad, b_gate, norm_g, layer):
    b, t, _ = z_f3.shape

    def zcols(width, off):
        return pl.BlockSpec((1, t, width), lambda bi, h: (bi, 0, off // width + h))

    return pl.pallas_call(
        _gla_kernel, grid=(b, GLA_HEADS),
        in_specs=[zcols(GLA_DK, ZF_BQ), zcols(GLA_DK, ZF_BK), zcols(GLA_DV, ZF_BV), zcols(GLA_DV, ZF_BR),
                  pl.BlockSpec((1, t, 128), lambda bi, h: (bi, 0, ZF_GLR // 128)),
                  pl.BlockSpec((None, 128, GLA_DK), lambda bi, h: (layer, 0, h)),
                  pl.BlockSpec((None, 1, GLA_DK), lambda bi, h: (layer, 0, h)),
                  pl.BlockSpec((None, 1, GLA_DV), lambda bi, h: (layer, 0, 0))],
        out_specs=pl.BlockSpec((1, t, GLA_DV), lambda bi, h: (bi, 0, h)),
        out_shape=jax.ShapeDtypeStruct((b, t, B_V_W), BF16),
        compiler_params=_cparams("parallel", "parallel"), name="gla",
    )(z_f3, z_f3, z_f3, z_f3, z_f3, w_gate2_pad, b_gate.reshape(b_gate.shape[0], 1, -1),
      norm_g.reshape(norm_g.shape[0], 1, -1))


def _rot_half_cols(w):
    half = w.shape[-1] // 2
    return jnp.concatenate([-w[..., half:], w[..., :half]], axis=-1)


IN_SPLITS = (A_QK_W, A_QK_W, A_V_W, B_K_W, B_K_W, B_V_W, GLA_GATE_RANK, B_V_W, MLA_Q_RANK, MLA_KV_RANK + MLA_ROPE_DIM)


def _prep_in_kernel(w_ref, wa_ref, wf_ref):
    offs = [0] + list(np.cumsum(IN_SPLITS))
    (b_q, b_k, b_v, b_glr, b_r, c_qa, c_kva) = [(int(offs[i]), int(offs[i + 1])) for i in range(3, 10)]
    half = MLA_ROPE_DIM // 2

    def cols(lo, hi):
        return w_ref[:, lo:hi].astype(BF16)

    wa_ref[...] = cols(0, int(offs[3]))
    wf_ref[...] = jnp.zeros(wf_ref.shape, BF16)
    for dst, (lo, hi) in ((ZF_CQA, c_qa), (ZF_BQ, b_q), (ZF_BK, b_k), (ZF_BV, b_v), (ZF_BR, b_r),
                          (ZF_CKV, (c_kva[0], c_kva[0] + MLA_KV_RANK)), (ZF_GLR, b_glr)):
        wf_ref[:, dst:dst + hi - lo] = cols(lo, hi)
    pe = c_kva[0] + MLA_KV_RANK
    wf_ref[:, ZF_KPE + MLA_NOPE_DIM:ZF_KPE + MLA_NOPE_DIM + MLA_ROPE_DIM] = cols(pe, pe + MLA_ROPE_DIM)
    wf_ref[:, ZF_KPER + MLA_NOPE_DIM:ZF_KPER + MLA_NOPE_DIM + half] = -cols(pe + half, pe + MLA_ROPE_DIM)
    wf_ref[:, ZF_KPER + MLA_NOPE_DIM + half:ZF_KPER + MLA_NOPE_DIM + MLA_ROPE_DIM] = cols(pe, pe + half)


def _prep_in_proj(w_in):
    nl, d, width = w_in.shape
    assert width == sum(IN_SPLITS)
    a_w = sum(IN_SPLITS[:3])
    tr = min(256, d)
    return pl.pallas_call(
        _prep_in_kernel, grid=(nl, d // tr),
        in_specs=[pl.BlockSpec((None, tr, width), lambda l, i: (l, i, 0))],
        out_specs=[pl.BlockSpec((None, tr, a_w), lambda l, i: (l, i, 0)),
                   pl.BlockSpec((None, tr, ZF_W), lambda l, i: (l, i, 0))],
        out_shape=[jax.ShapeDtypeStruct((nl, d, a_w), BF16), jax.ShapeDtypeStruct((nl, d, ZF_W), BF16)],
        compiler_params=_cparams("parallel", "parallel"), name="prep_in_proj",
    )(w_in)


def _prep_mla(w_qb, w_kvb):
    lead = w_qb.shape[:-1]
    wq = w_qb.reshape(lead + (MLA_HEADS, MLA_NOPE_DIM + MLA_ROPE_DIM))
    nope, rope = wq[..., :MLA_NOPE_DIM], wq[..., MLA_NOPE_DIM:]
    pad = jnp.zeros(lead + (MLA_HEADS, MLA_HEAD_PAD - MLA_NOPE_DIM - MLA_ROPE_DIM), w_qb.dtype)
    w_q = jnp.concatenate([nope, rope, pad], axis=-1).reshape(lead + (MLA_QK_W,))
    w_qrot = jnp.concatenate([jnp.zeros_like(nope), _rot_half_cols(rope), pad], axis=-1).reshape(lead + (MLA_QK_W,))
    lead = w_kvb.shape[:-1]
    wkv = w_kvb.reshape(lead + (MLA_HEADS, MLA_NOPE_DIM + MLA_V_DIM))
    k_nope, v = wkv[..., :MLA_NOPE_DIM], wkv[..., MLA_NOPE_DIM:]
    w_kn = jnp.concatenate([k_nope, jnp.zeros(lead + (MLA_HEADS, MLA_HEAD_PAD - MLA_NOPE_DIM), w_kvb.dtype)],
                           axis=-1).reshape(lead + (MLA_QK_W,))
    return w_q.astype(BF16), w_qrot.astype(BF16), w_kn.astype(BF16), v.reshape(lead + (C_V_W,)).astype(BF16)


FFN_SRC_BLOCK = 256


def _cast_gu_kernel(*refs, nb, k):
    srcs, o_ref = refs[:-1], refs[-1]
    f = pl.program_id(1)
    sb = srcs[0].shape[1]
    for j, src in enumerate(srcs):
        x = src[...].astype(BF16)
        s = j % k
        if nb % k and s >= nb % k:
            x = jnp.where(f * k + s < nb, x, jnp.zeros_like(x))
        o_ref[:, j * sb:(j + 1) * sb] = x


def _prep_ffn_gu(w_gu, tf):
    nl, two, d, f2 = w_gu.shape
    ff = f2 // 2
    sb = FFN_SRC_BLOCK
    assert ff % sb == 0 and tf % sb == 0
    nb, k = ff // sb, tf // sb
    nf = -(-nb // k)

    def cols(off, s):
        return pl.BlockSpec((None, None, d, sb),
                            lambda lj, f: (lj // two, lj % two, 0, off + jnp.minimum(f * k + s, nb - 1)))

    return pl.pallas_call(
        functools.partial(_cast_gu_kernel, nb=nb, k=k), grid=(nl * two, nf),
        in_specs=[cols(off, s) for off in (0, nb) for s in range(k)],
        out_specs=pl.BlockSpec((None, None, None, d, 2 * tf), lambda lj, f: (lj // two, lj % two, f, 0, 0)),
        out_shape=jax.ShapeDtypeStruct((nl, two, nf, d, 2 * tf), BF16),
        compiler_params=_cparams("parallel", "parallel"), name="cast_gu",
    )(*([w_gu] * (2 * k)))


def kernel(x, p, positions, emb_ln_g, emb_ln_b, w_in, w_branch_gate, da_lambda_q1, da_lambda_k1, da_lambda_q2,
           da_lambda_k2, da_subln_g, gla_w_gate2, gla_b_gate, gla_norm_g, mla_q_norm_g, mla_w_qb, mla_kv_norm_g,
           mla_w_kvb, w_o_a, w_o_b, w_o_c, w_out, ffn_w_gu, ffn_w_down, ln_g, ln_b, ple_w_proj, ple_w_gate):
    b, t, d = x.shape
    n = b * t
    depth = w_in.shape[0]
    alpha = float(DEEPNORM_ALPHA)

    w_gu = _prep_ffn_gu(ffn_w_gu, FFN_CHUNK)
    ff_pad = w_gu.shape[2] * FFN_CHUNK - ffn_w_down.shape[2]
    w_down = jnp.pad(ffn_w_down.astype(BF16), ((0, 0), (0, 0), (0, ff_pad), (0, 0)))
    w_a, w_f = _prep_in_proj(w_in)
    w_q, w_qrot, w_kn, w_v = _prep_mla(mla_w_qb, mla_w_kvb)
    w_bg = w_branch_gate.astype(BF16)
    w_oa, w_ob, w_oc, w_o = (w.astype(BF16) for w in (w_o_a, w_o_b, w_o_c, w_out))
    w_pg, w_pp = ple_w_gate.astype(BF16), ple_w_proj.astype(BF16)
    w2_pad = jnp.pad(gla_w_gate2.astype(BF16), ((0, 0), (0, 128 - GLA_GATE_RANK), (0, 0)))
    p = p.reshape(depth, n, -1)

    cos_tab, sin_tab = _rope_tables(positions)
    slopes = jnp.asarray(_alibi_slopes(DA_HEADS))
    no_slopes = jnp.zeros((1,), F32)
    no_lam = jnp.zeros((4, DA_QK_DIM), F32)
    no_gain = jnp.ones((MLA_V_DIM, 1), F32)

    h, hb = _layer_norm(x.reshape(n, d), emb_ln_g, emb_ln_b)
    for i in range(depth):
        h, hb = _ffn_ln(h, hb, w_gu, w_down, i, 0, ln_g[i, 0], ln_b[i, 0], alpha)

        z_a = _matmul(hb, w_a, i, BF16, "in_proj_a").reshape(b, t, -1)
        z_f = _matmul(hb, w_f, i, F32, "in_proj_f")
        lam_init = 0.8 - 0.6 * math.exp(-0.3 * i)
        lam_params = jnp.stack([da_lambda_q1[i], da_lambda_k1[i], da_lambda_q2[i], da_lambda_k2[i]]).astype(F32)
        o_a = _attention(z_a, z_a, z_a, 0, DA_HEADS, 2 * DA_HEADS, 2 * DA_QK_DIM, DA_V_DIM, DA_HEADS,
                         diff=True, score_scale=1.0, slopes=slopes, lam_params=lam_params,
                         gain_col=da_subln_g[i].reshape(-1, 1), lam_init=lam_init, name="diff_attn")
        o_b = _gla(z_f.reshape(b, t, ZF_W), w2_pad, gla_b_gate, gla_norm_g, i)
        q_c, k_c, v_c = _mla_proj(z_f, cos_tab, sin_tab, mla_q_norm_g, mla_kv_norm_g, w_q, w_qrot, w_kn, w_v, i)
        o_c = _attention(q_c.reshape(b, t, -1), k_c.reshape(b, t, -1), v_c.reshape(b, t, -1), 0, 0, 0,
                         MLA_HEAD_PAD, MLA_V_DIM, MLA_HEADS, diff=False,
                         score_scale=(MLA_NOPE_DIM + MLA_ROPE_DIM) ** -0.5, slopes=no_slopes, lam_params=no_lam,
                         gain_col=no_gain, lam_init=0.0, name="mla_attn")
        merged = _merge(hb, o_a.reshape(n, -1), o_b.reshape(n, -1), o_c.reshape(n, -1), w_bg, w_oa, w_ob, w_oc, i)
        h, hb = _proj_ln(h, merged, w_o, i, ln_g[i, 1], ln_b[i, 1], alpha)

        resid = _ple_resid(hb, w_pg, p, w_pp, i, h, alpha)
        h, hb = _ffn_ln(resid, hb, w_gu, w_down, i, 1, ln_g[i, 2], ln_b[i, 2], 1.0)
    return h.reshape(b, t, d)
```

```python
import functools
import math

import numpy as np
import jax
import jax.numpy as jnp
from jax import lax
from jax.experimental import pallas as pl
from jax.experimental.pallas import tpu as pltpu

F32 = jnp.float32
BF16 = jnp.bfloat16

DEPTH = 2
CHUNK = 64
N_BRANCHES = 3
DA_HEADS = 12
DA_QK_DIM = 64
DA_V_DIM = 128
GLA_HEADS = 4
GLA_DK = 128
GLA_DV = 256
GLA_GATE_RANK = 16
GLA_GATE_NORMALIZER = 16.0
MLA_HEADS = 12
MLA_Q_RANK = 768
MLA_KV_RANK = 512
MLA_NOPE_DIM = 128
MLA_ROPE_DIM = 64
MLA_V_DIM = 128
ROPE_THETA = 10000.0
LN_EPS = 1e-5
RMS_EPS = 1e-6
DEEPNORM_ALPHA = (2 * DEPTH) ** 0.25

A_QK_W = DA_HEADS * 2 * DA_QK_DIM
A_V_W = DA_HEADS * DA_V_DIM
B_K_W = GLA_HEADS * GLA_DK
B_V_W = GLA_HEADS * GLA_DV
C_V_W = MLA_HEADS * MLA_V_DIM
MLA_HEAD_PAD = 256
MLA_QK_W = MLA_HEADS * MLA_HEAD_PAD

ZF_CQA = 0
ZF_KPE = 768
ZF_BQ = 1024
ZF_BK = 1536
ZF_BV = 2048
ZF_BR = 3072
ZF_CKV = 4096
ZF_KPER = 4608
ZF_GLR = 4864
ZF_W = 5120

VMEM_CAP_V7X = 64 * 1024 * 1024
VMEM_LIMIT = VMEM_CAP_V7X - 8 * 1024 * 1024
VMEM_LIMIT_FFN = VMEM_CAP_V7X - 4 * 1024 * 1024
FFN_CHUNK = 256
FFN_ROW_GROUPS = 2
NEG_BIG = -1e30
LOG2E = math.log2(math.e)
DA_Q_SCALE = DA_QK_DIM ** -0.5 * LOG2E
MLA_Q_SCALE = (MLA_NOPE_DIM + MLA_ROPE_DIM) ** -0.5 * LOG2E


def _cparams(*sem, vmem=VMEM_LIMIT):
    return pltpu.CompilerParams(dimension_semantics=sem, vmem_limit_bytes=vmem)


def _dot(a, b):
    return jnp.dot(a, b, preferred_element_type=F32)


def _dot_nt(a, b):
    return lax.dot_general(a, b, (((1,), (1,)), ((), ())), preferred_element_type=F32)


def _dot_tn(a, b):
    return lax.dot_general(a, b, (((0,), (0,)), ((), ())), preferred_element_type=F32)


def _ln_rows(y, g, b):
    mu = jnp.mean(y, axis=-1, keepdims=True)
    yc = y - mu
    var = jnp.mean(yc * yc, axis=-1, keepdims=True)
    return yc * lax.rsqrt(var + LN_EPS) * g + b


def _rms_rows(y, g):
    return y * lax.rsqrt(jnp.mean(y * y, axis=-1, keepdims=True) + RMS_EPS) * g


def _ln_kernel(x_ref, g_ref, b_ref, o_ref, ob_ref):
    y = _ln_rows(x_ref[...], g_ref[...], b_ref[...])
    o_ref[...] = y
    ob_ref[...] = y.astype(BF16)


def _layer_norm(x, g, b):
    n, d = x.shape
    tr = min(256, n)
    row = pl.BlockSpec((tr, d), lambda i: (i, 0))
    vec = pl.BlockSpec((1, d), lambda i: (0, 0))
    return pl.pallas_call(
        _ln_kernel, grid=(n // tr,), in_specs=[row, vec, vec], out_specs=[row, row],
        out_shape=[jax.ShapeDtypeStruct((n, d), F32), jax.ShapeDtypeStruct((n, d), BF16)],
        compiler_params=_cparams("parallel"), name="ln_emb",
    )(x, g.reshape(1, d), b.reshape(1, d))


LN_SLAB = 128


def _resid_copy(resid_hbm, dst_ref, sem, block):
    tm = dst_ref.shape[0]
    start = pl.multiple_of(block * tm, tm)
    return pltpu.make_async_copy(resid_hbm.at[pl.ds(start, tm), :], dst_ref, sem)


def _ln_epilogue(o_ref, ob_ref, g_ref, b_ref, out_scale):
    g = g_ref[...]
    b = b_ref[...]

    def body(r, carry):
        sl = pl.ds(pl.multiple_of(r * LN_SLAB, LN_SLAB), LN_SLAB)
        y = _ln_rows(o_ref[sl, :] * out_scale, g, b)
        o_ref[sl, :] = y
        ob_ref[sl, :] = y.astype(BF16)
        return carry

    lax.fori_loop(0, o_ref.shape[0] // LN_SLAB, body, 0)


def _ffn_kernel(resid_hbm, hb_ref, wgu_ref, wd_ref, g_ref, b_ref, o_ref, ob_ref, sem, *, res_scale):
    f = pl.program_id(1)
    tf = wd_ref.shape[0]

    @pl.when(f == 0)
    def _():
        cp = _resid_copy(resid_hbm, o_ref, sem, pl.program_id(0))
        cp.start()
        cp.wait()
        o_ref[...] = o_ref[...] * (2.0 * res_scale)

    rows = o_ref.shape[0] // FFN_ROW_GROUPS
    groups = [slice(r * rows, (r + 1) * rows) for r in range(FFN_ROW_GROUPS)]
    gus = [_dot(hb_ref[g, :], wgu_ref[...]) for g in groups]
    acts = [(gu[:, :tf] * jax.nn.sigmoid(gu[:, :tf]) * gu[:, tf:]).astype(BF16) for gu in gus]
    for g, act in zip(groups, acts):
        o_ref[g, :] += _dot(act, wd_ref[...])

    @pl.when(f == pl.num_programs(1) - 1)
    def _():
        _ln_epilogue(o_ref, ob_ref, g_ref, b_ref, 0.5)


def _ffn_ln(resid, hb, w_gu, w_down, layer, half, ln_g, ln_b, res_scale):
    n, d = hb.shape
    nf, tf = w_gu.shape[2], w_gu.shape[4] // 2
    tm = min(1024, n)
    vec = pl.BlockSpec((1, d), lambda i, f: (0, 0))
    row = pl.BlockSpec((tm, d), lambda i, f: (i, 0), pipeline_mode=pl.Buffered(1))
    return pl.pallas_call(
        functools.partial(_ffn_kernel, res_scale=res_scale),
        grid=(n // tm, nf),
        in_specs=[
            pl.BlockSpec(memory_space=pl.ANY),
            pl.BlockSpec((tm, d), lambda i, f: (i, 0)),
            pl.BlockSpec((None, None, None, d, 2 * tf), lambda i, f: (layer, half, f, 0, 0)),
            pl.BlockSpec((None, None, tf, d), lambda i, f: (layer, half, f, 0)),
            vec, vec,
        ],
        out_specs=[row, row],
        out_shape=[jax.ShapeDtypeStruct((n, d), F32), jax.ShapeDtypeStruct((n, d), BF16)],
        scratch_shapes=[pltpu.SemaphoreType.DMA(())],
        compiler_params=_cparams("parallel", "arbitrary", vmem=VMEM_LIMIT_FFN), name="ffn_ln",
    )(resid, hb, w_gu, w_down, ln_g.reshape(1, d), ln_b.reshape(1, d))


def _proj_ln_kernel(resid_hbm, x_ref, w_ref, g_ref, b_ref, o_ref, ob_ref, r_sc, sem, *, res_scale):
    i = pl.program_id(0)
    k = pl.program_id(1)

    @pl.when((i == 0) & (k == 0))
    def _():
        _resid_copy(resid_hbm, r_sc, sem, 0).start()

    @pl.when((k == 1) & (i + 1 < pl.num_programs(0)))
    def _():
        _resid_copy(resid_hbm, r_sc, sem, i + 1).start()

    @pl.when(k == 0)
    def _():
        _resid_copy(resid_hbm, r_sc, sem, i).wait()
        o_ref[...] = r_sc[...] * res_scale + _dot(x_ref[...], w_ref[...])

    @pl.when(k > 0)
    def _():
        o_ref[...] += _dot(x_ref[...], w_ref[...])

    @pl.when(k == pl.num_programs(1) - 1)
    def _():
        _ln_epilogue(o_ref, ob_ref, g_ref, b_ref, 1.0)


def _proj_ln(resid, x, w, layer, ln_g, ln_b, res_scale):
    n, kdim = x.shape
    d = w.shape[2]
    tm = min(512, n)
    tk = 1024
    assert kdim // tk >= 2
    vec = pl.BlockSpec((1, d), lambda i, k: (0, 0))
    row = pl.BlockSpec((tm, d), lambda i, k: (i, 0))
    return pl.pallas_call(
        functools.partial(_proj_ln_kernel, res_scale=res_scale),
        grid=(n // tm, kdim // tk),
        in_specs=[
            pl.BlockSpec(memory_space=pl.ANY),
            pl.BlockSpec((tm, tk), lambda i, k: (i, k)),
            pl.BlockSpec((None, tk, d), lambda i, k: (layer, k, 0)),
            vec, vec,
        ],
        out_specs=[row, row],
        out_shape=[jax.ShapeDtypeStruct((n, d), F32), jax.ShapeDtypeStruct((n, d), BF16)],
        scratch_shapes=[pltpu.VMEM((tm, d), F32), pltpu.SemaphoreType.DMA(())],
        compiler_params=_cparams("arbitrary", "arbitrary"), name="proj_ln",
    )(resid, x, w, ln_g.reshape(1, d), ln_b.reshape(1, d))


def _mm_kernel(x_ref, w_ref, o_ref):
    o_ref[...] = _dot(x_ref[...], w_ref[...]).astype(o_ref.dtype)


def _mm_colscale_kernel(x_ref, w_ref, s_ref, o_ref):
    o_ref[...] = (_dot(x_ref[...], w_ref[...]) * s_ref[...]).astype(o_ref.dtype)


def _matmul(x, w, layer, out_dtype, name, col_scale=None):
    n, kdim = x.shape
    m = w.shape[2]
    tm = min(1024, n)
    tn = 512
    in_specs = [pl.BlockSpec((tm, kdim), lambda i, j: (i, 0)),
                pl.BlockSpec((None, kdim, tn), lambda i, j: (layer, 0, j))]
    args = (x, w)
    if col_scale is not None:
        in_specs.append(pl.BlockSpec((1, tn), lambda i, j: (0, j)))
        args += (col_scale,)
    return pl.pallas_call(
        _mm_kernel if col_scale is None else _mm_colscale_kernel, grid=(n // tm, m // tn),
        in_specs=in_specs,
        out_specs=pl.BlockSpec((tm, tn), lambda i, j: (i, j)),
        out_shape=jax.ShapeDtypeStruct((n, m), out_dtype),
        compiler_params=_cparams("parallel", "parallel"), name=name,
    )(*args)


def _ple_kernel(hb_ref, wg_ref, p_ref, wp_ref, h_ref, o_ref, *, alpha):
    gate = jax.nn.sigmoid(_dot(hb_ref[...], wg_ref[...]))
    proj = _dot(p_ref[...].astype(BF16), wp_ref[...])
    o_ref[...] = alpha * h_ref[...] + gate * proj


def _ple_resid(hb, w_gate, p, w_proj, layer, h, alpha):
    n, d = hb.shape
    pd = p.shape[2]
    tm = min(1024, n)
    tn = 512
    tile = pl.BlockSpec((tm, tn), lambda i, j: (i, j))
    return pl.pallas_call(
        functools.partial(_ple_kernel, alpha=alpha), grid=(n // tm, d // tn),
        in_specs=[
            pl.BlockSpec((tm, d), lambda i, j: (i, 0)),
            pl.BlockSpec((None, d, tn), lambda i, j: (layer, 0, j)),
            pl.BlockSpec((None, tm, pd), lambda i, j: (layer, i, 0)),
            pl.BlockSpec((None, pd, tn), lambda i, j: (layer, 0, j)),
            tile,
        ],
        out_specs=tile,
        out_shape=jax.ShapeDtypeStruct((n, d), F32),
        compiler_params=_cparams("parallel", "parallel"), name="ple_resid",
    )(hb, w_gate, p, w_proj, h)


def _merge_kernel(hb_ref, oa_ref, ob_ref, oc_ref, g0_ref, g1_ref, g2_ref, wa_ref, wb_ref, wc_ref, o_ref):
    x = hb_ref[...]
    acc = jax.nn.sigmoid(_dot(x, g0_ref[...])) * _dot(oa_ref[...], wa_ref[...])
    acc += jax.nn.sigmoid(_dot(x, g1_ref[...])) * _dot(ob_ref[...], wb_ref[...])
    acc += jax.nn.sigmoid(_dot(x, g2_ref[...])) * _dot(oc_ref[...], wc_ref[...])
    o_ref[...] = acc.astype(o_ref.dtype)


def _merge(hb, o_a, o_b, o_c, w_bg, w_oa, w_ob, w_oc, layer):
    n, d = hb.shape
    tm = min(512, n)
    tn = 512
    nj = d // tn

    def rows(w):
        return pl.BlockSpec((tm, w), lambda i, j: (i, 0))

    def cols(kdim, off):
        return pl.BlockSpec((None, kdim, tn), lambda i, j: (layer, 0, off + j))

    return pl.pallas_call(
        _merge_kernel, grid=(n // tm, nj),
        in_specs=[rows(d), rows(o_a.shape[1]), rows(o_b.shape[1]), rows(o_c.shape[1]),
                  cols(d, 0), cols(d, nj), cols(d, 2 * nj),
                  cols(w_oa.shape[1], 0), cols(w_ob.shape[1], 0), cols(w_oc.shape[1], 0)],
        out_specs=pl.BlockSpec((tm, tn), lambda i, j: (i, j)),
        out_shape=jax.ShapeDtypeStruct((n, d), BF16),
        compiler_params=_cparams("parallel", "parallel"), name="merge",
    )(hb, o_a, o_b, o_c, w_bg, w_bg, w_bg, w_oa, w_ob, w_oc)


def _rope_tab_kernel(pos_ref, inv_ref, c_ref, s_ref):
    ang = pos_ref[...].astype(F32) * inv_ref[...]
    lane = lax.broadcasted_iota(jnp.int32, ang.shape, 1)
    rope = (lane >= MLA_NOPE_DIM) & (lane < MLA_NOPE_DIM + MLA_ROPE_DIM)
    c_ref[...] = jnp.where(lane < MLA_NOPE_DIM, 1.0, jnp.where(rope, jnp.cos(ang), 0.0))
    s_ref[...] = jnp.where(rope, jnp.sin(ang), 0.0)


def _rope_tables(positions):
    n = positions.size
    half = MLA_ROPE_DIM // 2
    inv = ROPE_THETA ** (-np.arange(half, dtype=np.float32) / half)
    inv_row = np.zeros((1, MLA_HEAD_PAD), np.float32)
    inv_row[0, MLA_NOPE_DIM:MLA_NOPE_DIM + MLA_ROPE_DIM] = np.concatenate([inv, inv])
    tr = min(512, n)
    tab = pl.BlockSpec((tr, MLA_HEAD_PAD), lambda i: (i, 0))
    return pl.pallas_call(
        _rope_tab_kernel, grid=(n // tr,),
        in_specs=[pl.BlockSpec((tr, 1), lambda i: (i, 0)), pl.BlockSpec((1, MLA_HEAD_PAD), lambda i: (0, 0))],
        out_specs=[tab, tab],
        out_shape=[jax.ShapeDtypeStruct((n, MLA_HEAD_PAD), F32)] * 2,
        compiler_params=_cparams("parallel"), name="rope_tables",
    )(positions.reshape(n, 1), jnp.asarray(inv_row))


def _mla_proj_kernel(cqa_ref, ckv_ref, kpe_ref, kper_ref, c_ref, s_ref, gq_ref, gkv_ref,
                     wq_ref, wqr_ref, wkn_ref, wv_ref, q_out, k_out, v_out):
    cos = c_ref[...]
    sin = s_ref[...]
    xq = _rms_rows(cqa_ref[...], gq_ref[...]).astype(BF16)
    q = _dot(xq, wq_ref[...])
    q_rot = _dot(xq, wqr_ref[...])
    xkv = _rms_rows(ckv_ref[...], gkv_ref[...]).astype(BF16)
    k_nope = _dot(xkv, wkn_ref[...])
    k_rope = kpe_ref[...] * cos + kper_ref[...] * sin
    for h in range(MLA_HEADS):
        sl = slice(h * MLA_HEAD_PAD, (h + 1) * MLA_HEAD_PAD)
        q_out[:, sl] = ((q[:, sl] * cos + q_rot[:, sl] * sin) * MLA_Q_SCALE).astype(BF16)
        k_out[:, sl] = (k_nope[:, sl] + k_rope).astype(BF16)
    v_out[...] = _dot(xkv, wv_ref[...]).astype(BF16)


def _mla_proj(z_f, cos_tab, sin_tab, g_q, g_kv, w_q, w_qrot, w_kn, w_v, layer):
    n = z_f.shape[0]
    tm = min(256, n)

    def zcols(width, off):
        return pl.BlockSpec((tm, width), lambda i: (i, off // width))

    def whole(a):
        return pl.BlockSpec((None,) + a.shape[1:], lambda i: (layer, 0, 0))

    def rows(width):
        return pl.BlockSpec((tm, width), lambda i: (i, 0))

    g_q = g_q.reshape(g_q.shape[0], 1, -1)
    g_kv = g_kv.reshape(g_kv.shape[0], 1, -1)
    return pl.pallas_call(
        _mla_proj_kernel, grid=(n // tm,),
        in_specs=[zcols(MLA_Q_RANK, ZF_CQA), zcols(MLA_KV_RANK, ZF_CKV), zcols(MLA_HEAD_PAD, ZF_KPE),
                  zcols(MLA_HEAD_PAD, ZF_KPER), rows(MLA_HEAD_PAD), rows(MLA_HEAD_PAD),
                  whole(g_q), whole(g_kv), whole(w_q), whole(w_qrot), whole(w_kn), whole(w_v)],
        out_specs=[rows(MLA_QK_W), rows(MLA_QK_W), rows(C_V_W)],
        out_shape=[jax.ShapeDtypeStruct((n, MLA_QK_W), BF16), jax.ShapeDtypeStruct((n, MLA_QK_W), BF16),
                   jax.ShapeDtypeStruct((n, C_V_W), BF16)],
        compiler_params=_cparams("parallel"), name="mla_proj",
    )(z_f, z_f, z_f, z_f, cos_tab, sin_tab, g_q, g_kv, w_q, w_qrot, w_kn, w_v)


ATTN_TILE = 512
ATTN_MAPS_PER_STEP = 4


ATTN_DENOM_ROWS = 16


def _attn_kernel(slopes_ref, lam_ref, q_ref, k_ref, v_ref, g_ref, o_ref, vt_sc, bias_sc, m_sc, acc_sc,
                 *, diff, lam_init, dqk, dv):
    t = ATTN_TILE
    nmap = 2 if diff else 1
    hp = ATTN_MAPS_PER_STEP // nmap
    qi = pl.program_id(2)
    nt = k_ref.shape[1] // t
    if diff:
        slope2 = [slopes_ref[pl.program_id(1) * hp + hh] * LOG2E for hh in range(hp)]

    @pl.when(qi == 0)
    def _():
        for hh in range(hp):
            for j in range(nt):
                vt_sc[hh, j, :dv] = v_ref[0, j * t:(j + 1) * t, hh * dv:(hh + 1) * dv].astype(F32).T.astype(BF16)
                vt_sc[hh, j, dv:] = jnp.ones((ATTN_DENOM_ROWS, t), BF16)
        key = lax.broadcasted_iota(jnp.int32, (t, t), 0)
        qry = lax.broadcasted_iota(jnp.int32, (t, t), 1)
        allowed = (key // CHUNK) <= (qry // CHUNK)
        if diff:
            for hh in range(hp):
                bias_sc[hh, 0] = slope2[hh] * (qry - key).astype(F32)
                bias_sc[hh, 1] = jnp.where(allowed, slope2[hh] * jnp.abs(qry - key).astype(F32), -NEG_BIG)
        else:
            bias_sc[0, 0] = jnp.where(allowed, 0.0, -NEG_BIG)

    qts = []
    for hh in range(hp):
        qt = q_ref[0, :, hh * dqk:(hh + 1) * dqk].astype(F32).T
        if diff:
            feat = lax.broadcasted_iota(jnp.int32, qt.shape, 0)
            qts.append(jnp.where(feat < DA_QK_DIM, qt, 0.0).astype(BF16))
            qts.append(jnp.where(feat >= DA_QK_DIM, qt, 0.0).astype(BF16))
        else:
            qts.append(qt.astype(BF16))

    m_sc[...] = jnp.full(m_sc.shape, NEG_BIG, F32)
    acc_sc[...] = jnp.zeros(acc_sc.shape, F32)

    def step(kj, diag):
        ks = pl.ds(pl.multiple_of(kj * t, t), t)
        scores = []
        for hh in range(hp):
            k = k_ref[0, ks, hh * dqk:(hh + 1) * dqk]
            shift = None
            if diff:
                bias = bias_sc[hh, 1] if diag else bias_sc[hh, 0]
                if not diag:
                    shift = ((qi - kj) * t).astype(F32) * slope2[hh]
            else:
                bias = bias_sc[0, 0] if diag else None
            for i in range(hh * nmap, (hh + 1) * nmap):
                s = _dot(k, qts[i])
                scores.append((s if bias is None else s - bias, shift))
        probs, alphas = [], []
        for i, (s, shift) in enumerate(scores):
            m_prev = m_sc[i]
            m_tile = jnp.max(s, axis=0, keepdims=True)
            m_new = jnp.maximum(m_prev, m_tile if shift is None else m_tile - shift)
            alphas.append(jnp.exp2(m_prev - m_new))
            probs.append(jnp.exp2(s - (m_new if shift is None else m_new + shift)).astype(BF16))
            m_sc[i] = m_new
        for i, p in enumerate(probs):
            acc_sc[i] = alphas[i] * acc_sc[i] + _dot(vt_sc[i // nmap, kj], p)

    def full_step(kj, carry):
        step(kj, False)
        return carry

    lax.fori_loop(0, qi, full_step, 0)
    step(qi, True)

    if diff:
        lp = lam_ref[...]
        lam = (jnp.exp(jnp.sum(lp[0:1] * lp[1:2], axis=1, keepdims=True))
               - jnp.exp(jnp.sum(lp[2:3] * lp[3:4], axis=1, keepdims=True)) + lam_init)
    def normalised(i):
        acc = acc_sc[i]
        return acc[:dv] / acc[dv:dv + 1]

    for hh in range(hp):
        o = normalised(hh * nmap)
        if diff:
            o = o - lam * normalised(hh * nmap + 1)
            o = o * lax.rsqrt(jnp.mean(o * o, axis=0, keepdims=True) + RMS_EPS) * g_ref[...] * (1.0 - lam_init)
        o_ref[0, :, hh * dv:(hh + 1) * dv] = o.T.astype(o_ref.dtype)


def _attention(q_arr, k_arr, v_arr, q_off, k_off, v_off, dqk, dv, heads, *, diff,
               slopes, lam_params, gain_col, lam_init, name):
    b, t, _ = q_arr.shape
    tq = ATTN_TILE
    nmap = 2 if diff else 1
    hp = ATTN_MAPS_PER_STEP // nmap
    assert heads % hp == 0 and q_off % hp == 0 and k_off % hp == 0 and v_off % hp == 0
    nbias = (hp, 2) if diff else (1, 1)
    return pl.pallas_call(
        functools.partial(_attn_kernel, diff=diff, lam_init=lam_init, dqk=dqk, dv=dv),
        grid=(b, heads // hp, t // tq),
        in_specs=[
            pl.BlockSpec(memory_space=pltpu.SMEM),
            pl.BlockSpec(lam_params.shape, lambda bi, g, i: (0, 0)),
            pl.BlockSpec((1, tq, hp * dqk), lambda bi, g, i: (bi, i, q_off // hp + g)),
            pl.BlockSpec((1, t, hp * dqk), lambda bi, g, i: (bi, 0, k_off // hp + g)),
            pl.BlockSpec((1, t, hp * dv), lambda bi, g, i: (bi, 0, v_off // hp + g)),
            pl.BlockSpec(gain_col.shape, lambda bi, g, i: (0, 0)),
        ],
        out_specs=pl.BlockSpec((1, tq, hp * dv), lambda bi, g, i: (bi, i, g)),
        out_shape=jax.ShapeDtypeStruct((b, t, heads * dv), BF16),
        scratch_shapes=[pltpu.VMEM((hp, t // tq, dv + ATTN_DENOM_ROWS, tq), BF16), pltpu.VMEM(nbias + (tq, tq), F32),
                        pltpu.VMEM((hp * nmap, 1, tq), F32),
                        pltpu.VMEM((hp * nmap, dv + ATTN_DENOM_ROWS, tq), F32)],
        compiler_params=_cparams("parallel", "parallel", "arbitrary"), name=name,
    )(slopes, lam_params, q_arr, k_arr, v_arr, gain_col)


def _alibi_slopes(n):
    def pow2_slopes(m):
        start = 2.0 ** (-8.0 / m)
        return [start ** (i + 1) for i in range(m)]
    c = 2 ** int(math.floor(math.log2(n)))
    s = pow2_slopes(c)
    if c < n:
        s = s + pow2_slopes(2 * c)[0::2][: n - c]
    return np.array(s, dtype=np.float32)


GLA_GROUP = 8


def _gla_kernel(q_ref, k_ref, v_ref, r_ref, glr_ref, w2_ref, bg_ref, ng_ref, o_ref):
    c = CHUNK
    gc = GLA_GROUP * c
    scale = GLA_DK ** -0.5
    ri = lax.broadcasted_iota(jnp.int32, (c, c), 0)
    ci = lax.broadcasted_iota(jnp.int32, (c, c), 1)
    causal = ci <= ri
    tri = causal.astype(BF16)
    w2 = w2_ref[...]
    bg = bg_ref[...]
    ng = ng_ref[...]
    chunks = [slice(g * c, (g + 1) * c) for g in range(GLA_GROUP)]

    def body(n, state_t):
        rows = pl.ds(pl.multiple_of(n * gc, gc), gc)
        pre = _dot(glr_ref[0, rows, :].astype(BF16), w2) + bg
        log_a = (jnp.minimum(pre, 0.0) - jnp.log1p(jnp.exp(-jnp.abs(pre)))) / GLA_GATE_NORMALIZER
        hi = log_a.astype(BF16)
        rem = log_a - hi.astype(F32)
        mid = rem.astype(BF16)
        lo = (rem - mid.astype(F32)).astype(BF16)
        bcum = [_dot(tri, hi[s]) + _dot(tri, mid[s]) + _dot(tri, lo[s]) for s in chunks]
        b_mid = [b[c // 2:c // 2 + 1, :] for b in bcum]
        b_last = [b[c - 1:c, :] for b in bcum]
        q = q_ref[0, rows, :] * scale
        k = k_ref[0, rows, :]
        v = v_ref[0, rows, :].astype(BF16)
        att = [_dot_nt((q[s] * jnp.exp(b - bm)).astype(BF16), (k[s] * jnp.exp(bm - b)).astype(BF16))
               for s, b, bm in zip(chunks, bcum, b_mid)]
        att = [jnp.where(causal, a, 0.0).astype(BF16) for a in att]
        o_intra = [_dot(a, v[s]) for a, s in zip(att, chunks)]
        upd_t = [_dot_tn(v[s], (k[s] * jnp.exp(bl - b)).astype(BF16)) for s, b, bl in zip(chunks, bcum, b_last)]
        q_dec = [(q[s] * jnp.exp(b)).astype(BF16) for s, b in zip(chunks, bcum)]
        outs = []
        for g in range(GLA_GROUP):
            outs.append(o_intra[g] + _dot_nt(q_dec[g], state_t.astype(BF16)))
            state_t = state_t * jnp.exp(b_last[g]) + upd_t[g]
        o = jnp.concatenate(outs, axis=0)
        r = r_ref[0, rows, :]
        o_ref[0, rows, :] = (_rms_rows(o, ng) * (r * jax.nn.sigmoid(r))).astype(o_ref.dtype)
        return state_t

    lax.fori_loop(0, q_ref.shape[1] // gc, body, jnp.zeros((GLA_DV, GLA_DK), F32))


def _gla(z_f3, w_gate2_pad, b_gate, norm_g, layer):
    b, t, _ = z_f3.shape

    def zcols(width, off):
        return pl.BlockSpec((1, t, width), lambda bi, h: (bi, 0, off // width + h))

    return pl.pallas_call(
        _gla_kernel, grid=(b, GLA_HEADS),
        in_specs=[zcols(GLA_DK, ZF_BQ), zcols(GLA_DK, ZF_BK), zcols(GLA_DV, ZF_BV), zcols(GLA_DV, ZF_BR),
                  pl.BlockSpec((1, t, 128), lambda bi, h: (bi, 0, ZF_GLR // 128)),
                  pl.BlockSpec((None, 128, GLA_DK), lambda bi, h: (layer, 0, h)),
                  pl.BlockSpec((None, 1, GLA_DK), lambda bi, h: (layer, 0, h)),
                  pl.BlockSpec((None, 1, GLA_DV), lambda bi, h: (layer, 0, 0))],
        out_specs=pl.BlockSpec((1, t, GLA_DV), lambda bi, h: (bi, 0, h)),
        out_shape=jax.ShapeDtypeStruct((b, t, B_V_W), BF16),
        compiler_params=_cparams("parallel", "parallel"), name="gla",
    )(z_f3, z_f3, z_f3, z_f3, z_f3, w_gate2_pad, b_gate.reshape(b_gate.shape[0], 1, -1),
      norm_g.reshape(norm_g.shape[0], 1, -1))


def _rot_half_cols(w):
    half = w.shape[-1] // 2
    return jnp.concatenate([-w[..., half:], w[..., :half]], axis=-1)


IN_SPLITS = (A_QK_W, A_QK_W, A_V_W, B_K_W, B_K_W, B_V_W, GLA_GATE_RANK, B_V_W, MLA_Q_RANK, MLA_KV_RANK + MLA_ROPE_DIM)


def _prep_in_kernel(w_ref, wa_ref, wf_ref):
    offs = [0] + list(np.cumsum(IN_SPLITS))
    (b_q, b_k, b_v, b_glr, b_r, c_qa, c_kva) = [(int(offs[i]), int(offs[i + 1])) for i in range(3, 10)]
    half = MLA_ROPE_DIM // 2

    def cols(lo, hi):
        return w_ref[:, lo:hi].astype(BF16)

    wa_ref[...] = cols(0, int(offs[3]))
    wf_ref[...] = jnp.zeros(wf_ref.shape, BF16)
    for dst, (lo, hi) in ((ZF_CQA, c_qa), (ZF_BQ, b_q), (ZF_BK, b_k), (ZF_BV, b_v), (ZF_BR, b_r),
                          (ZF_CKV, (c_kva[0], c_kva[0] + MLA_KV_RANK)), (ZF_GLR, b_glr)):
        wf_ref[:, dst:dst + hi - lo] = cols(lo, hi)
    pe = c_kva[0] + MLA_KV_RANK
    wf_ref[:, ZF_KPE + MLA_NOPE_DIM:ZF_KPE + MLA_NOPE_DIM + MLA_ROPE_DIM] = cols(pe, pe + MLA_ROPE_DIM)
    wf_ref[:, ZF_KPER + MLA_NOPE_DIM:ZF_KPER + MLA_NOPE_DIM + half] = -cols(pe + half, pe + MLA_ROPE_DIM)
    wf_ref[:, ZF_KPER + MLA_NOPE_DIM + half:ZF_KPER + MLA_NOPE_DIM + MLA_ROPE_DIM] = cols(pe, pe + half)


def _prep_in_proj(w_in):
    nl, d, width = w_in.shape
    assert width == sum(IN_SPLITS)
    a_w = sum(IN_SPLITS[:3])
    tr = min(256, d)
    return pl.pallas_call(
        _prep_in_kernel, grid=(nl, d // tr),
        in_specs=[pl.BlockSpec((None, tr, width), lambda l, i: (l, i, 0))],
        out_specs=[pl.BlockSpec((None, tr, a_w), lambda l, i: (l, i, 0)),
                   pl.BlockSpec((None, tr, ZF_W), lambda l, i: (l, i, 0))],
        out_shape=[jax.ShapeDtypeStruct((nl, d, a_w), BF16), jax.ShapeDtypeStruct((nl, d, ZF_W), BF16)],
        compiler_params=_cparams("parallel", "parallel"), name="prep_in_proj",
    )(w_in)


def _prep_mla(w_qb, w_kvb):
    lead = w_qb.shape[:-1]
    wq = w_qb.reshape(lead + (MLA_HEADS, MLA_NOPE_DIM + MLA_ROPE_DIM))
    nope, rope = wq[..., :MLA_NOPE_DIM], wq[..., MLA_NOPE_DIM:]
    pad = jnp.zeros(lead + (MLA_HEADS, MLA_HEAD_PAD - MLA_NOPE_DIM - MLA_ROPE_DIM), w_qb.dtype)
    w_q = jnp.concatenate([nope, rope, pad], axis=-1).reshape(lead + (MLA_QK_W,))
    w_qrot = jnp.concatenate([jnp.zeros_like(nope), _rot_half_cols(rope), pad], axis=-1).reshape(lead + (MLA_QK_W,))
    lead = w_kvb.shape[:-1]
    wkv = w_kvb.reshape(lead + (MLA_HEADS, MLA_NOPE_DIM + MLA_V_DIM))
    k_nope, v = wkv[..., :MLA_NOPE_DIM], wkv[..., MLA_NOPE_DIM:]
    w_kn = jnp.concatenate([k_nope, jnp.zeros(lead + (MLA_HEADS, MLA_HEAD_PAD - MLA_NOPE_DIM), w_kvb.dtype)],
                           axis=-1).reshape(lead + (MLA_QK_W,))
    return w_q.astype(BF16), w_qrot.astype(BF16), w_kn.astype(BF16), v.reshape(lead + (C_V_W,)).astype(BF16)


FFN_SRC_BLOCK = 256


def _cast_gu_kernel(*refs, nb, k):
    srcs, o_ref = refs[:-1], refs[-1]
    f = pl.program_id(1)
    sb = srcs[0].shape[1]
    for j, src in enumerate(srcs):
        x = src[...].astype(BF16)
        s = j % k
        if nb % k and s >= nb % k:
            x = jnp.where(f * k + s < nb, x, jnp.zeros_like(x))
        o_ref[:, j * sb:(j + 1) * sb] = x


def _prep_ffn_gu(w_gu, tf):
    nl, two, d, f2 = w_gu.shape
    ff = f2 // 2
    sb = FFN_SRC_BLOCK
    assert ff % sb == 0 and tf % sb == 0
    nb, k = ff // sb, tf // sb
    nf = -(-nb // k)

    def cols(off, s):
        return pl.BlockSpec((None, None, d, sb),
                            lambda lj, f: (lj // two, lj % two, 0, off + jnp.minimum(f * k + s, nb - 1)))

    return pl.pallas_call(
        functools.partial(_cast_gu_kernel, nb=nb, k=k), grid=(nl * two, nf),
        in_specs=[cols(off, s) for off in (0, nb) for s in range(k)],
        out_specs=pl.BlockSpec((None, None, None, d, 2 * tf), lambda lj, f: (lj // two, lj % two, f, 0, 0)),
        out_shape=jax.ShapeDtypeStruct((nl, two, nf, d, 2 * tf), BF16),
        compiler_params=_cparams("parallel", "parallel"), name="cast_gu",
    )(*([w_gu] * (2 * k)))


def kernel(x, p, positions, emb_ln_g, emb_ln_b, w_in, w_branch_gate, da_lambda_q1, da_lambda_k1, da_lambda_q2,
           da_lambda_k2, da_subln_g, gla_w_gate2, gla_b_gate, gla_norm_g, mla_q_norm_g, mla_w_qb, mla_kv_norm_g,
           mla_w_kvb, w_o_a, w_o_b, w_o_c, w_out, ffn_w_gu, ffn_w_down, ln_g, ln_b, ple_w_proj, ple_w_gate):
    b, t, d = x.shape
    n = b * t
    depth = w_in.shape[0]
    alpha = float(DEEPNORM_ALPHA)

    w_gu = _prep_ffn_gu(ffn_w_gu, FFN_CHUNK)
    ff_pad = w_gu.shape[2] * FFN_CHUNK - ffn_w_down.shape[2]
    w_down = jnp.pad(ffn_w_down.astype(BF16), ((0, 0), (0, 0), (0, ff_pad), (0, 0)))
    w_a, w_f = _prep_in_proj(w_in)
    w_q, w_qrot, w_kn, w_v = _prep_mla(mla_w_qb, mla_w_kvb)
    w_bg = w_branch_gate.astype(BF16)
    w_oa, w_ob, w_oc, w_o = (w.astype(BF16) for w in (w_o_a, w_o_b, w_o_c, w_out))
    w_pg, w_pp = ple_w_gate.astype(BF16), ple_w_proj.astype(BF16)
    w2_pad = jnp.pad(gla_w_gate2.astype(BF16), ((0, 0), (0, 128 - GLA_GATE_RANK), (0, 0)))
    p = p.reshape(depth, n, -1)

    cos_tab, sin_tab = _rope_tables(positions)
    slopes = jnp.asarray(_alibi_slopes(DA_HEADS))
    a_scale = np.ones((1, 2 * A_QK_W + A_V_W), np.float32)
    a_scale[0, :A_QK_W] = DA_Q_SCALE
    a_scale = jnp.asarray(a_scale)
    no_slopes = jnp.zeros((1,), F32)
    no_lam = jnp.zeros((4, DA_QK_DIM), F32)
    no_gain = jnp.ones((MLA_V_DIM, 1), F32)

    h, hb = _layer_norm(x.reshape(n, d), emb_ln_g, emb_ln_b)
    for i in range(depth):
        h, hb = _ffn_ln(h, hb, w_gu, w_down, i, 0, ln_g[i, 0], ln_b[i, 0], alpha)

        z_a = _matmul(hb, w_a, i, BF16, "in_proj_a", col_scale=a_scale).reshape(b, t, -1)
        z_f = _matmul(hb, w_f, i, F32, "in_proj_f")
        lam_init = 0.8 - 0.6 * math.exp(-0.3 * i)
        lam_params = jnp.stack([da_lambda_q1[i], da_lambda_k1[i], da_lambda_q2[i], da_lambda_k2[i]]).astype(F32)
        o_a = _attention(z_a, z_a, z_a, 0, DA_HEADS, 2 * DA_HEADS, 2 * DA_QK_DIM, DA_V_DIM, DA_HEADS,
                         diff=True, slopes=slopes, lam_params=lam_params,
                         gain_col=da_subln_g[i].reshape(-1, 1), lam_init=lam_init, name="diff_attn")
        o_b = _gla(z_f.reshape(b, t, ZF_W), w2_pad, gla_b_gate, gla_norm_g, i)
        q_c, k_c, v_c = _mla_proj(z_f, cos_tab, sin_tab, mla_q_norm_g, mla_kv_norm_g, w_q, w_qrot, w_kn, w_v, i)
        o_c = _attention(q_c.reshape(b, t, -1), k_c.reshape(b, t, -1), v_c.reshape(b, t, -1), 0, 0, 0,
                         MLA_HEAD_PAD, MLA_V_DIM, MLA_HEADS, diff=False,
                         slopes=no_slopes, lam_params=no_lam,
                         gain_col=no_gain, lam_init=0.0, name="mla_attn")
        merged = _merge(hb, o_a.reshape(n, -1), o_b.reshape(n, -1), o_c.reshape(n, -1), w_bg, w_oa, w_ob, w_oc, i)
        h, hb = _proj_ln(h, merged, w_o, i, ln_g[i, 1], ln_b[i, 1], alpha)

        resid = _ple_resid(hb, w_pg, p, w_pp, i, h, alpha)
        h, hb = _ffn_ln(resid, hb, w_gu, w_down, i, 1, ln_g[i, 2], ln_b[i, 2], 1.0)
    return h.reshape(b, t, d)
```

```python
import functools
import math

import numpy as np
import jax
import jax.numpy as jnp
from jax import lax
from jax.experimental import pallas as pl
from jax.experimental.pallas import tpu as pltpu

F32 = jnp.float32
BF16 = jnp.bfloat16

DEPTH = 2
CHUNK = 64
DA_HEADS = 12
DA_QK_DIM = 64
DA_V_DIM = 128
GLA_HEADS = 4
GLA_DK = 128
GLA_DV = 256
GLA_GATE_RANK = 16
GLA_GATE_NORMALIZER = 16.0
MLA_HEADS = 12
MLA_Q_RANK = 768
MLA_KV_RANK = 512
MLA_NOPE_DIM = 128
MLA_ROPE_DIM = 64
MLA_V_DIM = 128
ROPE_THETA = 10000.0
LN_EPS = 1e-5
RMS_EPS = 1e-6
DEEPNORM_ALPHA = (2 * DEPTH) ** 0.25

A_QK_W = DA_HEADS * 2 * DA_QK_DIM
A_V_W = DA_HEADS * DA_V_DIM
B_K_W = GLA_HEADS * GLA_DK
B_V_W = GLA_HEADS * GLA_DV
C_V_W = MLA_HEADS * MLA_V_DIM
MLA_HEAD_PAD = 256
MLA_QK_W = MLA_HEADS * MLA_HEAD_PAD

ZF_CQA = 0
ZF_KPE = 768
ZF_BQ = 1024
ZF_BK = 1536
ZF_BV = 2048
ZF_BR = 3072
ZF_CKV = 4096
ZF_KPER = 4608
ZF_GLR = 4864
ZF_GLR_W = 128
ZF_W = 5120

VMEM_CAP_V7X = 64 * 1024 * 1024
VMEM_LIMIT = VMEM_CAP_V7X - 8 * 1024 * 1024
VMEM_LIMIT_FFN = VMEM_CAP_V7X - 4 * 1024 * 1024
FFN_CHUNK = 256
FFN_ROW_GROUPS = 2
NEG_BIG = -1e30
LOG2E = math.log2(math.e)
DA_Q_SCALE = DA_QK_DIM ** -0.5 * LOG2E
MLA_Q_SCALE = (MLA_NOPE_DIM + MLA_ROPE_DIM) ** -0.5 * LOG2E


def _cparams(*sem, vmem=VMEM_LIMIT):
    return pltpu.CompilerParams(dimension_semantics=sem, vmem_limit_bytes=vmem)


def _dot(a, b):
    return jnp.dot(a, b, preferred_element_type=F32)


def _dot_nt(a, b):
    return lax.dot_general(a, b, (((1,), (1,)), ((), ())), preferred_element_type=F32)


def _dot_tn(a, b):
    return lax.dot_general(a, b, (((0,), (0,)), ((), ())), preferred_element_type=F32)


def _ln_rows(y, g, b):
    mu = jnp.mean(y, axis=-1, keepdims=True)
    yc = y - mu
    var = jnp.mean(yc * yc, axis=-1, keepdims=True)
    return yc * lax.rsqrt(var + LN_EPS) * g + b


def _rms_rows(y, g):
    return y * lax.rsqrt(jnp.mean(y * y, axis=-1, keepdims=True) + RMS_EPS) * g


def _ln_kernel(x_ref, g_ref, b_ref, o_ref, ob_ref):
    y = _ln_rows(x_ref[...], g_ref[...], b_ref[...])
    o_ref[...] = y
    ob_ref[...] = y.astype(BF16)


def _layer_norm(x, g, b):
    n, d = x.shape
    tr = min(256, n)
    row = pl.BlockSpec((tr, d), lambda i: (i, 0))
    vec = pl.BlockSpec((1, d), lambda i: (0, 0))
    return pl.pallas_call(
        _ln_kernel, grid=(n // tr,), in_specs=[row, vec, vec], out_specs=[row, row],
        out_shape=[jax.ShapeDtypeStruct((n, d), F32), jax.ShapeDtypeStruct((n, d), BF16)],
        compiler_params=_cparams("parallel"), name="ln_emb",
    )(x, g.reshape(1, d), b.reshape(1, d))


LN_SLAB = 128


def _resid_copy(resid_hbm, dst_ref, sem, block):
    tm = dst_ref.shape[0]
    start = pl.multiple_of(block * tm, tm)
    return pltpu.make_async_copy(resid_hbm.at[pl.ds(start, tm), :], dst_ref, sem)


def _ln_epilogue(o_ref, ob_ref, g_ref, b_ref, out_scale):
    g = g_ref[...]
    b = b_ref[...]

    def body(r, carry):
        sl = pl.ds(pl.multiple_of(r * LN_SLAB, LN_SLAB), LN_SLAB)
        y = _ln_rows(o_ref[sl, :] * out_scale, g, b)
        o_ref[sl, :] = y
        ob_ref[sl, :] = y.astype(BF16)
        return carry

    lax.fori_loop(0, o_ref.shape[0] // LN_SLAB, body, 0)


def _ffn_kernel(resid_hbm, hb_ref, wgu_ref, wd_ref, g_ref, b_ref, o_ref, ob_ref, sem, *, res_scale):
    f = pl.program_id(1)
    tf = wd_ref.shape[0]

    @pl.when(f == 0)
    def _():
        cp = _resid_copy(resid_hbm, o_ref, sem, pl.program_id(0))
        cp.start()
        cp.wait()
        o_ref[...] = o_ref[...] * (2.0 * res_scale)

    rows = o_ref.shape[0] // FFN_ROW_GROUPS
    groups = [slice(r * rows, (r + 1) * rows) for r in range(FFN_ROW_GROUPS)]
    gus = [_dot(hb_ref[g, :], wgu_ref[...]) for g in groups]
    acts = [(gu[:, :tf] * jax.nn.sigmoid(gu[:, :tf]) * gu[:, tf:]).astype(BF16) for gu in gus]
    for g, act in zip(groups, acts):
        o_ref[g, :] += _dot(act, wd_ref[...])

    @pl.when(f == pl.num_programs(1) - 1)
    def _():
        _ln_epilogue(o_ref, ob_ref, g_ref, b_ref, 0.5)


def _ffn_ln(resid, hb, w_gu, w_down, layer, half, ln_g, ln_b, res_scale):
    n, d = hb.shape
    nf, tf = w_gu.shape[2], w_gu.shape[4] // 2
    tm = min(1024, n)
    vec = pl.BlockSpec((1, d), lambda i, f: (0, 0))
    row = pl.BlockSpec((tm, d), lambda i, f: (i, 0), pipeline_mode=pl.Buffered(1))
    return pl.pallas_call(
        functools.partial(_ffn_kernel, res_scale=res_scale),
        grid=(n // tm, nf),
        in_specs=[
            pl.BlockSpec(memory_space=pl.ANY),
            pl.BlockSpec((tm, d), lambda i, f: (i, 0)),
            pl.BlockSpec((None, None, None, d, 2 * tf), lambda i, f: (layer, half, f, 0, 0)),
            pl.BlockSpec((None, None, tf, d), lambda i, f: (layer, half, f, 0)),
            vec, vec,
        ],
        out_specs=[row, row],
        out_shape=[jax.ShapeDtypeStruct((n, d), F32), jax.ShapeDtypeStruct((n, d), BF16)],
        scratch_shapes=[pltpu.SemaphoreType.DMA(())],
        compiler_params=_cparams("parallel", "arbitrary", vmem=VMEM_LIMIT_FFN), name="ffn_ln",
    )(resid, hb, w_gu, w_down, ln_g.reshape(1, d), ln_b.reshape(1, d))


def _proj_ln_kernel(resid_hbm, x_ref, w_ref, g_ref, b_ref, o_ref, ob_ref, r_sc, sem, *, res_scale):
    i = pl.program_id(0)
    k = pl.program_id(1)

    @pl.when((i == 0) & (k == 0))
    def _():
        _resid_copy(resid_hbm, r_sc, sem, 0).start()

    @pl.when((k == 1) & (i + 1 < pl.num_programs(0)))
    def _():
        _resid_copy(resid_hbm, r_sc, sem, i + 1).start()

    @pl.when(k == 0)
    def _():
        _resid_copy(resid_hbm, r_sc, sem, i).wait()
        o_ref[...] = r_sc[...] * res_scale + _dot(x_ref[...], w_ref[...])

    @pl.when(k > 0)
    def _():
        o_ref[...] += _dot(x_ref[...], w_ref[...])

    @pl.when(k == pl.num_programs(1) - 1)
    def _():
        _ln_epilogue(o_ref, ob_ref, g_ref, b_ref, 1.0)


def _proj_ln(resid, x, w, layer, ln_g, ln_b, res_scale):
    n, kdim = x.shape
    d = w.shape[2]
    tm = min(512, n)
    tk = 1024
    assert kdim // tk >= 2
    vec = pl.BlockSpec((1, d), lambda i, k: (0, 0))
    row = pl.BlockSpec((tm, d), lambda i, k: (i, 0))
    return pl.pallas_call(
        functools.partial(_proj_ln_kernel, res_scale=res_scale),
        grid=(n // tm, kdim // tk),
        in_specs=[
            pl.BlockSpec(memory_space=pl.ANY),
            pl.BlockSpec((tm, tk), lambda i, k: (i, k)),
            pl.BlockSpec((None, tk, d), lambda i, k: (layer, k, 0)),
            vec, vec,
        ],
        out_specs=[row, row],
        out_shape=[jax.ShapeDtypeStruct((n, d), F32), jax.ShapeDtypeStruct((n, d), BF16)],
        scratch_shapes=[pltpu.VMEM((tm, d), F32), pltpu.SemaphoreType.DMA(())],
        compiler_params=_cparams("arbitrary", "arbitrary"), name="proj_ln",
    )(resid, x, w, ln_g.reshape(1, d), ln_b.reshape(1, d))


def _mm_kernel(x_ref, w_ref, o_ref):
    o_ref[...] = _dot(x_ref[...], w_ref[...]).astype(o_ref.dtype)


def _mm_colscale_kernel(x_ref, w_ref, s_ref, o_ref):
    o_ref[...] = (_dot(x_ref[...], w_ref[...]) * s_ref[...]).astype(o_ref.dtype)


def _matmul(x, w, layer, out_dtype, name, col_scale=None):
    n, kdim = x.shape
    m = w.shape[2]
    tm = min(1024, n)
    tn = 512
    in_specs = [pl.BlockSpec((tm, kdim), lambda i, j: (i, 0)),
                pl.BlockSpec((None, kdim, tn), lambda i, j: (layer, 0, j))]
    args = (x, w)
    if col_scale is not None:
        in_specs.append(pl.BlockSpec((1, tn), lambda i, j: (0, j)))
        args += (col_scale,)
    return pl.pallas_call(
        _mm_kernel if col_scale is None else _mm_colscale_kernel, grid=(n // tm, m // tn),
        in_specs=in_specs,
        out_specs=pl.BlockSpec((tm, tn), lambda i, j: (i, j)),
        out_shape=jax.ShapeDtypeStruct((n, m), out_dtype),
        compiler_params=_cparams("parallel", "parallel"), name=name,
    )(*args)


PLE_ROW_GROUPS = 2


def _ple_kernel(hb_ref, wg_ref, p_ref, wp_ref, h_ref, o_ref, *, alpha):
    rows = o_ref.shape[0] // PLE_ROW_GROUPS
    groups = [slice(r * rows, (r + 1) * rows) for r in range(PLE_ROW_GROUPS)]
    dots = [(_dot(hb_ref[g, :], wg_ref[...]), _dot(p_ref[g, :].astype(BF16), wp_ref[...])) for g in groups]
    for g, (gate, proj) in zip(groups, dots):
        o_ref[g, :] = alpha * h_ref[g, :] + jax.nn.sigmoid(gate) * proj


def _ple_resid(hb, w_gate, p, w_proj, layer, h, alpha):
    n, d = hb.shape
    pd = p.shape[2]
    tm = min(1024, n)
    tn = 512
    tile = pl.BlockSpec((tm, tn), lambda i, j: (i, j))
    return pl.pallas_call(
        functools.partial(_ple_kernel, alpha=alpha), grid=(n // tm, d // tn),
        in_specs=[
            pl.BlockSpec((tm, d), lambda i, j: (i, 0)),
            pl.BlockSpec((None, d, tn), lambda i, j: (layer, 0, j)),
            pl.BlockSpec((None, tm, pd), lambda i, j: (layer, i, 0)),
            pl.BlockSpec((None, pd, tn), lambda i, j: (layer, 0, j)),
            tile,
        ],
        out_specs=tile,
        out_shape=jax.ShapeDtypeStruct((n, d), F32),
        compiler_params=_cparams("parallel", "parallel"), name="ple_resid",
    )(hb, w_gate, p, w_proj, h)


def _merge_kernel(hb_ref, oa_ref, ob_ref, oc_ref, g0_ref, g1_ref, g2_ref, wa_ref, wb_ref, wc_ref, o_ref):
    x = hb_ref[...]
    acc = jax.nn.sigmoid(_dot(x, g0_ref[...])) * _dot(oa_ref[...], wa_ref[...])
    acc += jax.nn.sigmoid(_dot(x, g1_ref[...])) * _dot(ob_ref[...], wb_ref[...])
    acc += jax.nn.sigmoid(_dot(x, g2_ref[...])) * _dot(oc_ref[...], wc_ref[...])
    o_ref[...] = acc.astype(o_ref.dtype)


def _merge(hb, o_a, o_b, o_c, w_bg, w_oa, w_ob, w_oc, layer):
    n, d = hb.shape
    tm = min(512, n)
    tn = 512
    nj = d // tn

    def rows(w):
        return pl.BlockSpec((tm, w), lambda i, j: (i, 0))

    def cols(kdim, off):
        return pl.BlockSpec((None, kdim, tn), lambda i, j: (layer, 0, off + j))

    return pl.pallas_call(
        _merge_kernel, grid=(n // tm, nj),
        in_specs=[rows(d), rows(o_a.shape[1]), rows(o_b.shape[1]), rows(o_c.shape[1]),
                  cols(d, 0), cols(d, nj), cols(d, 2 * nj),
                  cols(w_oa.shape[1], 0), cols(w_ob.shape[1], 0), cols(w_oc.shape[1], 0)],
        out_specs=pl.BlockSpec((tm, tn), lambda i, j: (i, j)),
        out_shape=jax.ShapeDtypeStruct((n, d), BF16),
        compiler_params=_cparams("parallel", "parallel"), name="merge",
    )(hb, o_a, o_b, o_c, w_bg, w_bg, w_bg, w_oa, w_ob, w_oc)


def _rope_tab_kernel(pos_ref, inv_ref, c_ref, s_ref):
    ang = pos_ref[...].astype(F32) * inv_ref[...]
    lane = lax.broadcasted_iota(jnp.int32, ang.shape, 1)
    rope = (lane >= MLA_NOPE_DIM) & (lane < MLA_NOPE_DIM + MLA_ROPE_DIM)
    c_ref[...] = jnp.where(lane < MLA_NOPE_DIM, 1.0, jnp.where(rope, jnp.cos(ang), 0.0))
    s_ref[...] = jnp.where(rope, jnp.sin(ang), 0.0)


def _rope_tables(positions):
    n = positions.size
    half = MLA_ROPE_DIM // 2
    inv = ROPE_THETA ** (-np.arange(half, dtype=np.float32) / half)
    inv_row = np.zeros((1, MLA_HEAD_PAD), np.float32)
    inv_row[0, MLA_NOPE_DIM:MLA_NOPE_DIM + MLA_ROPE_DIM] = np.concatenate([inv, inv])
    tr = min(512, n)
    tab = pl.BlockSpec((tr, MLA_HEAD_PAD), lambda i: (i, 0))
    return pl.pallas_call(
        _rope_tab_kernel, grid=(n // tr,),
        in_specs=[pl.BlockSpec((tr, 1), lambda i: (i, 0)), pl.BlockSpec((1, MLA_HEAD_PAD), lambda i: (0, 0))],
        out_specs=[tab, tab],
        out_shape=[jax.ShapeDtypeStruct((n, MLA_HEAD_PAD), F32)] * 2,
        compiler_params=_cparams("parallel"), name="rope_tables",
    )(positions.reshape(n, 1), jnp.asarray(inv_row))


def _mla_proj_kernel(cqa_ref, ckv_ref, kpe_ref, kper_ref, c_ref, s_ref, gq_ref, gkv_ref,
                     wq_ref, wqr_ref, wkn_ref, wv_ref, q_out, k_out, v_out):
    cos = c_ref[...]
    sin = s_ref[...]
    xq = _rms_rows(cqa_ref[...], gq_ref[...]).astype(BF16)
    q = _dot(xq, wq_ref[...])
    q_rot = _dot(xq, wqr_ref[...])
    xkv = _rms_rows(ckv_ref[...], gkv_ref[...]).astype(BF16)
    k_nope = _dot(xkv, wkn_ref[...])
    k_rope = kpe_ref[...] * cos + kper_ref[...] * sin
    for h in range(MLA_HEADS):
        sl = slice(h * MLA_HEAD_PAD, (h + 1) * MLA_HEAD_PAD)
        q_out[:, sl] = ((q[:, sl] * cos + q_rot[:, sl] * sin) * MLA_Q_SCALE).astype(BF16)
        k_out[:, sl] = (k_nope[:, sl] + k_rope).astype(BF16)
    v_out[...] = _dot(xkv, wv_ref[...]).astype(BF16)


def _mla_proj(z_f, cos_tab, sin_tab, g_q, g_kv, w_q, w_qrot, w_kn, w_v, layer):
    n = z_f.shape[0]
    tm = min(256, n)

    def zcols(width, off):
        return pl.BlockSpec((tm, width), lambda i: (i, off // width))

    def whole(a):
        return pl.BlockSpec((None,) + a.shape[1:], lambda i: (layer, 0, 0))

    def rows(width):
        return pl.BlockSpec((tm, width), lambda i: (i, 0))

    g_q = g_q.reshape(g_q.shape[0], 1, -1)
    g_kv = g_kv.reshape(g_kv.shape[0], 1, -1)
    return pl.pallas_call(
        _mla_proj_kernel, grid=(n // tm,),
        in_specs=[zcols(MLA_Q_RANK, ZF_CQA), zcols(MLA_KV_RANK, ZF_CKV), zcols(MLA_HEAD_PAD, ZF_KPE),
                  zcols(MLA_HEAD_PAD, ZF_KPER), rows(MLA_HEAD_PAD), rows(MLA_HEAD_PAD),
                  whole(g_q), whole(g_kv), whole(w_q), whole(w_qrot), whole(w_kn), whole(w_v)],
        out_specs=[rows(MLA_QK_W), rows(MLA_QK_W), rows(C_V_W)],
        out_shape=[jax.ShapeDtypeStruct((n, MLA_QK_W), BF16), jax.ShapeDtypeStruct((n, MLA_QK_W), BF16),
                   jax.ShapeDtypeStruct((n, C_V_W), BF16)],
        compiler_params=_cparams("parallel"), name="mla_proj",
    )(z_f, z_f, z_f, z_f, cos_tab, sin_tab, g_q, g_kv, w_q, w_qrot, w_kn, w_v)


ATTN_TILE = 512
ATTN_MAPS_PER_STEP = 6


ATTN_DENOM_ROWS = 16


def _attn_kernel(slopes_ref, lam_ref, q_ref, k_ref, v_ref, g_ref, o_ref, vt_sc, bias_sc, m_sc, acc_sc,
                 *, diff, lam_init, dqk, dv):
    t = ATTN_TILE
    nmap = 2 if diff else 1
    hp = ATTN_MAPS_PER_STEP // nmap
    qi = pl.program_id(2)
    nt = k_ref.shape[1] // t
    if diff:
        slope2 = [slopes_ref[pl.program_id(1) * hp + hh] * LOG2E for hh in range(hp)]

    @pl.when(qi == 0)
    def _():
        for hh in range(hp):
            for j in range(nt):
                vt_sc[hh, j, :dv] = v_ref[0, j * t:(j + 1) * t, hh * dv:(hh + 1) * dv].astype(F32).T.astype(BF16)
                vt_sc[hh, j, dv:] = jnp.ones((ATTN_DENOM_ROWS, t), BF16)
        key = lax.broadcasted_iota(jnp.int32, (t, t), 0)
        qry = lax.broadcasted_iota(jnp.int32, (t, t), 1)
        allowed = (key // CHUNK) <= (qry // CHUNK)
        if diff:
            for hh in range(hp):
                bias_sc[hh, 0] = slope2[hh] * (qry - key).astype(F32)
                bias_sc[hh, 1] = jnp.where(allowed, slope2[hh] * jnp.abs(qry - key).astype(F32), -NEG_BIG)
        else:
            bias_sc[0, 0] = jnp.where(allowed, 0.0, -NEG_BIG)

    qts = []
    for hh in range(hp):
        qt = q_ref[0, :, hh * dqk:(hh + 1) * dqk].astype(F32).T
        if diff:
            feat = lax.broadcasted_iota(jnp.int32, qt.shape, 0)
            qts.append(jnp.where(feat < DA_QK_DIM, qt, 0.0).astype(BF16))
            qts.append(jnp.where(feat >= DA_QK_DIM, qt, 0.0).astype(BF16))
        else:
            qts.append(qt.astype(BF16))

    m_sc[...] = jnp.full(m_sc.shape, NEG_BIG, F32)
    acc_sc[...] = jnp.zeros(acc_sc.shape, F32)

    def step(kj, diag):
        ks = pl.ds(pl.multiple_of(kj * t, t), t)
        scores = []
        for hh in range(hp):
            k = k_ref[0, ks, hh * dqk:(hh + 1) * dqk]
            shift = None
            if diff:
                bias = bias_sc[hh, 1] if diag else bias_sc[hh, 0]
                if not diag:
                    shift = ((qi - kj) * t).astype(F32) * slope2[hh]
            else:
                bias = bias_sc[0, 0] if diag else None
            for i in range(hh * nmap, (hh + 1) * nmap):
                s = _dot(k, qts[i])
                scores.append((s if bias is None else s - bias, shift))
        probs, alphas = [], []
        for i, (s, shift) in enumerate(scores):
            m_prev = m_sc[i]
            m_tile = jnp.max(s, axis=0, keepdims=True)
            m_new = jnp.maximum(m_prev, m_tile if shift is None else m_tile - shift)
            alphas.append(jnp.exp2(m_prev - m_new))
            probs.append(jnp.exp2(s - (m_new if shift is None else m_new + shift)).astype(BF16))
            m_sc[i] = m_new
        for i, p in enumerate(probs):
            acc_sc[i] = alphas[i] * acc_sc[i] + _dot(vt_sc[i // nmap, kj], p)

    def full_step(kj, carry):
        step(kj, False)
        return carry

    lax.fori_loop(0, qi, full_step, 0)
    step(qi, True)

    if diff:
        lp = lam_ref[...]
        lam = (jnp.exp(jnp.sum(lp[0:1] * lp[1:2], axis=1, keepdims=True))
               - jnp.exp(jnp.sum(lp[2:3] * lp[3:4], axis=1, keepdims=True)) + lam_init)
    def normalised(i):
        acc = acc_sc[i]
        return acc[:dv] / acc[dv:dv + 1]

    for hh in range(hp):
        o = normalised(hh * nmap)
        if diff:
            o = o - lam * normalised(hh * nmap + 1)
            o = o * lax.rsqrt(jnp.mean(o * o, axis=0, keepdims=True) + RMS_EPS) * g_ref[...] * (1.0 - lam_init)
        o_ref[0, :, hh * dv:(hh + 1) * dv] = o.T.astype(o_ref.dtype)


def _attention(q_arr, k_arr, v_arr, q_off, k_off, v_off, dqk, dv, heads, *, diff,
               slopes, lam_params, gain_col, lam_init, name):
    b, t, _ = q_arr.shape
    tq = ATTN_TILE
    nmap = 2 if diff else 1
    hp = ATTN_MAPS_PER_STEP // nmap
    assert heads % hp == 0 and q_off % hp == 0 and k_off % hp == 0 and v_off % hp == 0
    nbias = (hp, 2) if diff else (1, 1)
    return pl.pallas_call(
        functools.partial(_attn_kernel, diff=diff, lam_init=lam_init, dqk=dqk, dv=dv),
        grid=(b, heads // hp, t // tq),
        in_specs=[
            pl.BlockSpec(memory_space=pltpu.SMEM),
            pl.BlockSpec(lam_params.shape, lambda bi, g, i: (0, 0)),
            pl.BlockSpec((1, tq, hp * dqk), lambda bi, g, i: (bi, i, q_off // hp + g)),
            pl.BlockSpec((1, t, hp * dqk), lambda bi, g, i: (bi, 0, k_off // hp + g)),
            pl.BlockSpec((1, t, hp * dv), lambda bi, g, i: (bi, 0, v_off // hp + g)),
            pl.BlockSpec(gain_col.shape, lambda bi, g, i: (0, 0)),
        ],
        out_specs=pl.BlockSpec((1, tq, hp * dv), lambda bi, g, i: (bi, i, g)),
        out_shape=jax.ShapeDtypeStruct((b, t, heads * dv), BF16),
        scratch_shapes=[pltpu.VMEM((hp, t // tq, dv + ATTN_DENOM_ROWS, tq), BF16), pltpu.VMEM(nbias + (tq, tq), F32),
                        pltpu.VMEM((hp * nmap, 1, tq), F32),
                        pltpu.VMEM((hp * nmap, dv + ATTN_DENOM_ROWS, tq), F32)],
        compiler_params=_cparams("parallel", "parallel", "arbitrary"), name=name,
    )(slopes, lam_params, q_arr, k_arr, v_arr, gain_col)


def _alibi_slopes(n):
    def pow2_slopes(m):
        start = 2.0 ** (-8.0 / m)
        return [start ** (i + 1) for i in range(m)]
    c = 2 ** int(math.floor(math.log2(n)))
    s = pow2_slopes(c)
    if c < n:
        s = s + pow2_slopes(2 * c)[0::2][: n - c]
    return np.array(s, dtype=np.float32)


GLA_GROUP = 8


def _gla_kernel(q_ref, k_ref, v_ref, r_ref, glr_ref, w2_ref, bg_ref, ng_ref, o_ref):
    c = CHUNK
    gc = GLA_GROUP * c
    scale = GLA_DK ** -0.5
    ri = lax.broadcasted_iota(jnp.int32, (c, c), 0)
    ci = lax.broadcasted_iota(jnp.int32, (c, c), 1)
    causal = ci <= ri
    tri = causal.astype(BF16)
    w2 = w2_ref[...]
    bg = bg_ref[...]
    ng = ng_ref[...]
    chunks = [slice(g * c, (g + 1) * c) for g in range(GLA_GROUP)]

    def body(n, state_t):
        rows = pl.ds(pl.multiple_of(n * gc, gc), gc)
        pre = _dot(glr_ref[0, rows, :].astype(BF16), w2) + bg
        log_a = (jnp.minimum(pre, 0.0) - jnp.log1p(jnp.exp(-jnp.abs(pre)))) / GLA_GATE_NORMALIZER
        hi = log_a.astype(BF16)
        rem = log_a - hi.astype(F32)
        mid = rem.astype(BF16)
        lo = (rem - mid.astype(F32)).astype(BF16)
        bcum = [_dot(tri, hi[s]) + _dot(tri, mid[s]) + _dot(tri, lo[s]) for s in chunks]
        b_mid = [b[c // 2:c // 2 + 1, :] for b in bcum]
        b_last = [b[c - 1:c, :] for b in bcum]
        q = q_ref[0, rows, :] * scale
        k = k_ref[0, rows, :]
        v = v_ref[0, rows, :].astype(BF16)
        att = [_dot_nt((q[s] * jnp.exp(b - bm)).astype(BF16), (k[s] * jnp.exp(bm - b)).astype(BF16))
               for s, b, bm in zip(chunks, bcum, b_mid)]
        att = [jnp.where(causal, a, 0.0).astype(BF16) for a in att]
        o_intra = [_dot(a, v[s]) for a, s in zip(att, chunks)]
        upd_t = [_dot_tn(v[s], (k[s] * jnp.exp(bl - b)).astype(BF16)) for s, b, bl in zip(chunks, bcum, b_last)]
        q_dec = [(q[s] * jnp.exp(b)).astype(BF16) for s, b in zip(chunks, bcum)]
        outs = []
        for g in range(GLA_GROUP):
            outs.append(o_intra[g] + _dot_nt(q_dec[g], state_t.astype(BF16)))
            state_t = state_t * jnp.exp(b_last[g]) + upd_t[g]
        o = jnp.concatenate(outs, axis=0)
        r = r_ref[0, rows, :]
        o_ref[0, rows, :] = (_rms_rows(o, ng) * (r * jax.nn.sigmoid(r))).astype(o_ref.dtype)
        return state_t

    lax.fori_loop(0, q_ref.shape[1] // gc, body, jnp.zeros((GLA_DV, GLA_DK), F32))


def _gla(z_f3, w_gate2_pad, b_gate, norm_g, layer):
    b, t, _ = z_f3.shape

    def zcols(width, off):
        return pl.BlockSpec((1, t, width), lambda bi, h: (bi, 0, off // width + h))

    return pl.pallas_call(
        _gla_kernel, grid=(b, GLA_HEADS),
        in_specs=[zcols(GLA_DK, ZF_BQ), zcols(GLA_DK, ZF_BK), zcols(GLA_DV, ZF_BV), zcols(GLA_DV, ZF_BR),
                  pl.BlockSpec((1, t, ZF_GLR_W), lambda bi, h: (bi, 0, ZF_GLR // ZF_GLR_W)),
                  pl.BlockSpec((None, ZF_GLR_W, GLA_DK), lambda bi, h: (layer, 0, h)),
                  pl.BlockSpec((None, 1, GLA_DK), lambda bi, h: (layer, 0, h)),
                  pl.BlockSpec((None, 1, GLA_DV), lambda bi, h: (layer, 0, 0))],
        out_specs=pl.BlockSpec((1, t, GLA_DV), lambda bi, h: (bi, 0, h)),
        out_shape=jax.ShapeDtypeStruct((b, t, B_V_W), BF16),
        compiler_params=_cparams("parallel", "parallel"), name="gla",
    )(z_f3, z_f3, z_f3, z_f3, z_f3, w_gate2_pad, b_gate.reshape(b_gate.shape[0], 1, -1),
      norm_g.reshape(norm_g.shape[0], 1, -1))


def _rot_half_cols(w):
    half = w.shape[-1] // 2
    return jnp.concatenate([-w[..., half:], w[..., :half]], axis=-1)


IN_SPLITS = (A_QK_W, A_QK_W, A_V_W, B_K_W, B_K_W, B_V_W, GLA_GATE_RANK, B_V_W, MLA_Q_RANK, MLA_KV_RANK + MLA_ROPE_DIM)


def _prep_in_kernel(w_ref, wa_ref, wf_ref):
    offs = [0] + list(np.cumsum(IN_SPLITS))
    (b_q, b_k, b_v, b_glr, b_r, c_qa, c_kva) = [(int(offs[i]), int(offs[i + 1])) for i in range(3, 10)]
    half = MLA_ROPE_DIM // 2

    def cols(lo, hi):
        return w_ref[:, lo:hi].astype(BF16)

    wa_ref[...] = cols(0, int(offs[3]))
    wf_ref[...] = jnp.zeros(wf_ref.shape, BF16)
    for dst, (lo, hi) in ((ZF_CQA, c_qa), (ZF_BQ, b_q), (ZF_BK, b_k), (ZF_BV, b_v), (ZF_BR, b_r),
                          (ZF_CKV, (c_kva[0], c_kva[0] + MLA_KV_RANK)), (ZF_GLR, b_glr)):
        wf_ref[:, dst:dst + hi - lo] = cols(lo, hi)
    pe = c_kva[0] + MLA_KV_RANK
    wf_ref[:, ZF_KPE + MLA_NOPE_DIM:ZF_KPE + MLA_NOPE_DIM + MLA_ROPE_DIM] = cols(pe, pe + MLA_ROPE_DIM)
    wf_ref[:, ZF_KPER + MLA_NOPE_DIM:ZF_KPER + MLA_NOPE_DIM + half] = -cols(pe + half, pe + MLA_ROPE_DIM)
    wf_ref[:, ZF_KPER + MLA_NOPE_DIM + half:ZF_KPER + MLA_NOPE_DIM + MLA_ROPE_DIM] = cols(pe, pe + half)


def _prep_in_proj(w_in):
    nl, d, width = w_in.shape
    assert width == sum(IN_SPLITS)
    a_w = sum(IN_SPLITS[:3])
    tr = min(256, d)
    return pl.pallas_call(
        _prep_in_kernel, grid=(nl, d // tr),
        in_specs=[pl.BlockSpec((None, tr, width), lambda l, i: (l, i, 0))],
        out_specs=[pl.BlockSpec((None, tr, a_w), lambda l, i: (l, i, 0)),
                   pl.BlockSpec((None, tr, ZF_W), lambda l, i: (l, i, 0))],
        out_shape=[jax.ShapeDtypeStruct((nl, d, a_w), BF16), jax.ShapeDtypeStruct((nl, d, ZF_W), BF16)],
        compiler_params=_cparams("parallel", "parallel"), name="prep_in_proj",
    )(w_in)


def _prep_mla(w_qb, w_kvb):
    lead = w_qb.shape[:-1]
    wq = w_qb.reshape(lead + (MLA_HEADS, MLA_NOPE_DIM + MLA_ROPE_DIM))
    nope, rope = wq[..., :MLA_NOPE_DIM], wq[..., MLA_NOPE_DIM:]
    pad = jnp.zeros(lead + (MLA_HEADS, MLA_HEAD_PAD - MLA_NOPE_DIM - MLA_ROPE_DIM), w_qb.dtype)
    w_q = jnp.concatenate([nope, rope, pad], axis=-1).reshape(lead + (MLA_QK_W,))
    w_qrot = jnp.concatenate([jnp.zeros_like(nope), _rot_half_cols(rope), pad], axis=-1).reshape(lead + (MLA_QK_W,))
    lead = w_kvb.shape[:-1]
    wkv = w_kvb.reshape(lead + (MLA_HEADS, MLA_NOPE_DIM + MLA_V_DIM))
    k_nope, v = wkv[..., :MLA_NOPE_DIM], wkv[..., MLA_NOPE_DIM:]
    w_kn = jnp.concatenate([k_nope, jnp.zeros(lead + (MLA_HEADS, MLA_HEAD_PAD - MLA_NOPE_DIM), w_kvb.dtype)],
                           axis=-1).reshape(lead + (MLA_QK_W,))
    return w_q.astype(BF16), w_qrot.astype(BF16), w_kn.astype(BF16), v.reshape(lead + (C_V_W,)).astype(BF16)


FFN_SRC_BLOCK = 256


def _cast_gu_kernel(*refs, nb, k):
    srcs, o_ref = refs[:-1], refs[-1]
    f = pl.program_id(1)
    sb = srcs[0].shape[1]
    for j, src in enumerate(srcs):
        x = src[...].astype(BF16)
        s = j % k
        if nb % k and s >= nb % k:
            x = jnp.where(f * k + s < nb, x, jnp.zeros_like(x))
        o_ref[:, j * sb:(j + 1) * sb] = x


def _prep_ffn_gu(w_gu, tf):
    nl, two, d, f2 = w_gu.shape
    ff = f2 // 2
    sb = FFN_SRC_BLOCK
    assert ff % sb == 0 and tf % sb == 0
    nb, k = ff // sb, tf // sb
    nf = -(-nb // k)

    def cols(off, s):
        return pl.BlockSpec((None, None, d, sb),
                            lambda lj, f: (lj // two, lj % two, 0, off + jnp.minimum(f * k + s, nb - 1)))

    return pl.pallas_call(
        functools.partial(_cast_gu_kernel, nb=nb, k=k), grid=(nl * two, nf),
        in_specs=[cols(off, s) for off in (0, nb) for s in range(k)],
        out_specs=pl.BlockSpec((None, None, None, d, 2 * tf), lambda lj, f: (lj // two, lj % two, f, 0, 0)),
        out_shape=jax.ShapeDtypeStruct((nl, two, nf, d, 2 * tf), BF16),
        compiler_params=_cparams("parallel", "parallel"), name="cast_gu",
    )(*([w_gu] * (2 * k)))


def kernel(x, p, positions, emb_ln_g, emb_ln_b, w_in, w_branch_gate, da_lambda_q1, da_lambda_k1, da_lambda_q2,
           da_lambda_k2, da_subln_g, gla_w_gate2, gla_b_gate, gla_norm_g, mla_q_norm_g, mla_w_qb, mla_kv_norm_g,
           mla_w_kvb, w_o_a, w_o_b, w_o_c, w_out, ffn_w_gu, ffn_w_down, ln_g, ln_b, ple_w_proj, ple_w_gate):
    b, t, d = x.shape
    n = b * t
    depth = w_in.shape[0]
    alpha = float(DEEPNORM_ALPHA)

    w_gu = _prep_ffn_gu(ffn_w_gu, FFN_CHUNK)
    ff_pad = w_gu.shape[2] * FFN_CHUNK - ffn_w_down.shape[2]
    w_down = jnp.pad(ffn_w_down.astype(BF16), ((0, 0), (0, 0), (0, ff_pad), (0, 0)))
    w_a, w_f = _prep_in_proj(w_in)
    w_q, w_qrot, w_kn, w_v = _prep_mla(mla_w_qb, mla_w_kvb)
    w_bg = w_branch_gate.astype(BF16)
    w_oa, w_ob, w_oc, w_o = (w.astype(BF16) for w in (w_o_a, w_o_b, w_o_c, w_out))
    w_pg, w_pp = ple_w_gate.astype(BF16), ple_w_proj.astype(BF16)
    w2_pad = jnp.pad(gla_w_gate2.astype(BF16), ((0, 0), (0, ZF_GLR_W - GLA_GATE_RANK), (0, 0)))
    p = p.reshape(depth, n, -1)

    cos_tab, sin_tab = _rope_tables(positions)
    slopes = jnp.asarray(_alibi_slopes(DA_HEADS))
    a_scale = np.ones((1, 2 * A_QK_W + A_V_W), np.float32)
    a_scale[0, :A_QK_W] = DA_Q_SCALE
    a_scale = jnp.asarray(a_scale)
    no_slopes = jnp.zeros((1,), F32)
    no_lam = jnp.zeros((4, DA_QK_DIM), F32)
    no_gain = jnp.ones((MLA_V_DIM, 1), F32)

    h, hb = _layer_norm(x.reshape(n, d), emb_ln_g, emb_ln_b)
    for i in range(depth):
        h, hb = _ffn_ln(h, hb, w_gu, w_down, i, 0, ln_g[i, 0], ln_b[i, 0], alpha)

        z_a = _matmul(hb, w_a, i, BF16, "in_proj_a", col_scale=a_scale).reshape(b, t, -1)
        z_f = _matmul(hb, w_f, i, F32, "in_proj_f")
        lam_init = 0.8 - 0.6 * math.exp(-0.3 * i)
        lam_params = jnp.stack([da_lambda_q1[i], da_lambda_k1[i], da_lambda_q2[i], da_lambda_k2[i]]).astype(F32)
        o_a = _attention(z_a, z_a, z_a, 0, DA_HEADS, 2 * DA_HEADS, 2 * DA_QK_DIM, DA_V_DIM, DA_HEADS,
                         diff=True, slopes=slopes, lam_params=lam_params,
                         gain_col=da_subln_g[i].reshape(-1, 1), lam_init=lam_init, name="diff_attn")
        o_b = _gla(z_f.reshape(b, t, ZF_W), w2_pad, gla_b_gate, gla_norm_g, i)
        q_c, k_c, v_c = _mla_proj(z_f, cos_tab, sin_tab, mla_q_norm_g, mla_kv_norm_g, w_q, w_qrot, w_kn, w_v, i)
        o_c = _attention(q_c.reshape(b, t, -1), k_c.reshape(b, t, -1), v_c.reshape(b, t, -1), 0, 0, 0,
                         MLA_HEAD_PAD, MLA_V_DIM, MLA_HEADS, diff=False,
                         slopes=no_slopes, lam_params=no_lam,
                         gain_col=no_gain, lam_init=0.0, name="mla_attn")
        merged = _merge(hb, o_a.reshape(n, -1), o_b.reshape(n, -1), o_c.reshape(n, -1), w_bg, w_oa, w_ob, w_oc, i)
        h, hb = _proj_ln(h, merged, w_o, i, ln_g[i, 1], ln_b[i, 1], alpha)

        resid = _ple_resid(hb, w_pg, p, w_pp, i, h, alpha)
        h, hb = _ffn_ln(resid, hb, w_gu, w_down, i, 1, ln_g[i, 2], ln_b[i, 2], 1.0)
    return h.reshape(b, t, d)
```

```python
import functools
import math

import numpy as np
import jax
import jax.numpy as jnp
from jax import lax
from jax.experimental import pallas as pl
from jax.experimental.pallas import tpu as pltpu

F32 = jnp.float32
BF16 = jnp.bfloat16

DEPTH = 2
CHUNK = 64
DA_HEADS = 12
DA_QK_DIM = 64
DA_V_DIM = 128
GLA_HEADS = 4
GLA_DK = 128
GLA_DV = 256
GLA_GATE_RANK = 16
GLA_GATE_NORMALIZER = 16.0
MLA_HEADS = 12
MLA_Q_RANK = 768
MLA_KV_RANK = 512
MLA_NOPE_DIM = 128
MLA_ROPE_DIM = 64
MLA_V_DIM = 128
ROPE_THETA = 10000.0
LN_EPS = 1e-5
RMS_EPS = 1e-6
DEEPNORM_ALPHA = (2 * DEPTH) ** 0.25

A_QK_W = DA_HEADS * 2 * DA_QK_DIM
A_V_W = DA_HEADS * DA_V_DIM
B_K_W = GLA_HEADS * GLA_DK
B_V_W = GLA_HEADS * GLA_DV
C_V_W = MLA_HEADS * MLA_V_DIM
MLA_HEAD_PAD = 256
MLA_QK_W = MLA_HEADS * MLA_HEAD_PAD

ZF_CQA = 0
ZF_KPE = 768
ZF_BQ = 1024
ZF_BK = 1536
ZF_BV = 2048
ZF_BR = 3072
ZF_CKV = 4096
ZF_KPER = 4608
ZF_GLR = 4864
ZF_GLR_W = 128
ZF_W = 5120

VMEM_CAP_V7X = 64 * 1024 * 1024
VMEM_LIMIT = VMEM_CAP_V7X - 8 * 1024 * 1024
VMEM_LIMIT_FFN = VMEM_CAP_V7X - 4 * 1024 * 1024
FFN_CHUNK = 256
FFN_ROW_GROUPS = 2
NEG_BIG = -1e30
LOG2E = math.log2(math.e)
DA_Q_SCALE = DA_QK_DIM ** -0.5 * LOG2E
MLA_Q_SCALE = (MLA_NOPE_DIM + MLA_ROPE_DIM) ** -0.5 * LOG2E


def _cparams(*sem, vmem=VMEM_LIMIT):
    return pltpu.CompilerParams(dimension_semantics=sem, vmem_limit_bytes=vmem)


def _dot(a, b):
    return jnp.dot(a, b, preferred_element_type=F32)


def _dot_nt(a, b):
    return lax.dot_general(a, b, (((1,), (1,)), ((), ())), preferred_element_type=F32)


def _dot_tn(a, b):
    return lax.dot_general(a, b, (((0,), (0,)), ((), ())), preferred_element_type=F32)


def _ln_rows(y, g, b):
    mu = jnp.mean(y, axis=-1, keepdims=True)
    yc = y - mu
    var = jnp.mean(yc * yc, axis=-1, keepdims=True)
    return yc * lax.rsqrt(var + LN_EPS) * g + b


def _rms_rows(y, g):
    return y * lax.rsqrt(jnp.mean(y * y, axis=-1, keepdims=True) + RMS_EPS) * g


def _ln_kernel(x_ref, g_ref, b_ref, o_ref, ob_ref):
    y = _ln_rows(x_ref[...], g_ref[...], b_ref[...])
    o_ref[...] = y
    ob_ref[...] = y.astype(BF16)


def _layer_norm(x, g, b):
    n, d = x.shape
    tr = min(256, n)
    row = pl.BlockSpec((tr, d), lambda i: (i, 0))
    vec = pl.BlockSpec((1, d), lambda i: (0, 0))
    return pl.pallas_call(
        _ln_kernel, grid=(n // tr,), in_specs=[row, vec, vec], out_specs=[row, row],
        out_shape=[jax.ShapeDtypeStruct((n, d), F32), jax.ShapeDtypeStruct((n, d), BF16)],
        compiler_params=_cparams("parallel"), name="ln_emb",
    )(x, g.reshape(1, d), b.reshape(1, d))


LN_SLAB = 128


def _resid_copy(resid_hbm, dst_ref, sem, block):
    tm = dst_ref.shape[0]
    start = pl.multiple_of(block * tm, tm)
    return pltpu.make_async_copy(resid_hbm.at[pl.ds(start, tm), :], dst_ref, sem)


def _ln_epilogue(o_ref, ob_ref, g_ref, b_ref, out_scale):
    g = g_ref[...]
    b = b_ref[...]

    def body(r, carry):
        sl = pl.ds(pl.multiple_of(r * LN_SLAB, LN_SLAB), LN_SLAB)
        y = _ln_rows(o_ref[sl, :] * out_scale, g, b)
        o_ref[sl, :] = y
        ob_ref[sl, :] = y.astype(BF16)
        return carry

    lax.fori_loop(0, o_ref.shape[0] // LN_SLAB, body, 0)


def _ln_epilogue_writeback(o_ref, ob_ref, out_hbm, outb_hbm, sem_o, sem_b, g_ref, b_ref, out_scale):
    g = g_ref[...]
    b = b_ref[...]
    tm = o_ref.shape[0]
    row0 = pl.program_id(0) * tm

    def copies(r):
        src = pl.ds(pl.multiple_of(r * LN_SLAB, LN_SLAB), LN_SLAB)
        dst = pl.ds(pl.multiple_of(row0 + r * LN_SLAB, LN_SLAB), LN_SLAB)
        return (pltpu.make_async_copy(o_ref.at[src, :], out_hbm.at[dst, :], sem_o),
                pltpu.make_async_copy(ob_ref.at[src, :], outb_hbm.at[dst, :], sem_b))

    def body(r, carry):
        sl = pl.ds(pl.multiple_of(r * LN_SLAB, LN_SLAB), LN_SLAB)
        y = _ln_rows(o_ref[sl, :] * out_scale, g, b)
        o_ref[sl, :] = y
        ob_ref[sl, :] = y.astype(BF16)
        for cp in copies(r):
            cp.start()
        return carry

    def drain(r, carry):
        for cp in copies(r):
            cp.wait()
        return carry

    lax.fori_loop(0, tm // LN_SLAB, body, 0)
    lax.fori_loop(0, tm // LN_SLAB, drain, 0)


def _ffn_kernel(resid_hbm, hb_ref, wgu_ref, wd_ref, g_ref, b_ref, out_hbm, outb_hbm, o_ref, ob_ref,
                sem, sem_o, sem_b, *, res_scale):
    f = pl.program_id(1)
    tf = wd_ref.shape[0]

    @pl.when(f == 0)
    def _():
        cp = _resid_copy(resid_hbm, o_ref, sem, pl.program_id(0))
        cp.start()
        cp.wait()
        o_ref[...] = o_ref[...] * (2.0 * res_scale)

    rows = o_ref.shape[0] // FFN_ROW_GROUPS
    groups = [slice(r * rows, (r + 1) * rows) for r in range(FFN_ROW_GROUPS)]
    gus = [_dot(hb_ref[g, :], wgu_ref[...]) for g in groups]
    acts = [(gu[:, :tf] * jax.nn.sigmoid(gu[:, :tf]) * gu[:, tf:]).astype(BF16) for gu in gus]
    for g, act in zip(groups, acts):
        o_ref[g, :] += _dot(act, wd_ref[...])

    @pl.when(f == pl.num_programs(1) - 1)
    def _():
        _ln_epilogue_writeback(o_ref, ob_ref, out_hbm, outb_hbm, sem_o, sem_b, g_ref, b_ref, 0.5)


def _ffn_ln(resid, hb, w_gu, w_down, layer, half, ln_g, ln_b, res_scale):
    n, d = hb.shape
    nf, tf = w_gu.shape[2], w_gu.shape[4] // 2
    tm = min(1024, n)
    vec = pl.BlockSpec((1, d), lambda i, f: (0, 0))
    hbm = pl.BlockSpec(memory_space=pl.ANY)
    return pl.pallas_call(
        functools.partial(_ffn_kernel, res_scale=res_scale),
        grid=(n // tm, nf),
        in_specs=[
            pl.BlockSpec(memory_space=pl.ANY),
            pl.BlockSpec((tm, d), lambda i, f: (i, 0)),
            pl.BlockSpec((None, None, None, d, 2 * tf), lambda i, f: (layer, half, f, 0, 0)),
            pl.BlockSpec((None, None, tf, d), lambda i, f: (layer, half, f, 0)),
            vec, vec,
        ],
        out_specs=[hbm, hbm],
        out_shape=[jax.ShapeDtypeStruct((n, d), F32), jax.ShapeDtypeStruct((n, d), BF16)],
        scratch_shapes=[pltpu.VMEM((tm, d), F32), pltpu.VMEM((tm, d), BF16), pltpu.SemaphoreType.DMA(()),
                        pltpu.SemaphoreType.DMA(()), pltpu.SemaphoreType.DMA(())],
        compiler_params=_cparams("arbitrary", "arbitrary", vmem=VMEM_LIMIT_FFN), name="ffn_ln",
    )(resid, hb, w_gu, w_down, ln_g.reshape(1, d), ln_b.reshape(1, d))


def _proj_ln_kernel(resid_hbm, x_ref, w_ref, g_ref, b_ref, o_ref, ob_ref, r_sc, sem, *, res_scale):
    i = pl.program_id(0)
    k = pl.program_id(1)

    @pl.when((i == 0) & (k == 0))
    def _():
        _resid_copy(resid_hbm, r_sc, sem, 0).start()

    @pl.when((k == 1) & (i + 1 < pl.num_programs(0)))
    def _():
        _resid_copy(resid_hbm, r_sc, sem, i + 1).start()

    @pl.when(k == 0)
    def _():
        _resid_copy(resid_hbm, r_sc, sem, i).wait()
        o_ref[...] = r_sc[...] * res_scale + _dot(x_ref[...], w_ref[...])

    @pl.when(k > 0)
    def _():
        o_ref[...] += _dot(x_ref[...], w_ref[...])

    @pl.when(k == pl.num_programs(1) - 1)
    def _():
        _ln_epilogue(o_ref, ob_ref, g_ref, b_ref, 1.0)


def _proj_ln(resid, x, w, layer, ln_g, ln_b, res_scale):
    n, kdim = x.shape
    d = w.shape[2]
    tm = min(512, n)
    tk = 1024
    assert kdim // tk >= 2
    vec = pl.BlockSpec((1, d), lambda i, k: (0, 0))
    row = pl.BlockSpec((tm, d), lambda i, k: (i, 0))
    return pl.pallas_call(
        functools.partial(_proj_ln_kernel, res_scale=res_scale),
        grid=(n // tm, kdim // tk),
        in_specs=[
            pl.BlockSpec(memory_space=pl.ANY),
            pl.BlockSpec((tm, tk), lambda i, k: (i, k)),
            pl.BlockSpec((None, tk, d), lambda i, k: (layer, k, 0)),
            vec, vec,
        ],
        out_specs=[row, row],
        out_shape=[jax.ShapeDtypeStruct((n, d), F32), jax.ShapeDtypeStruct((n, d), BF16)],
        scratch_shapes=[pltpu.VMEM((tm, d), F32), pltpu.SemaphoreType.DMA(())],
        compiler_params=_cparams("arbitrary", "arbitrary"), name="proj_ln",
    )(resid, x, w, ln_g.reshape(1, d), ln_b.reshape(1, d))


def _mm_kernel(x_ref, w_ref, o_ref):
    o_ref[...] = _dot(x_ref[...], w_ref[...]).astype(o_ref.dtype)


def _mm_colscale_kernel(x_ref, w_ref, s_ref, o_ref):
    o_ref[...] = (_dot(x_ref[...], w_ref[...]) * s_ref[...]).astype(o_ref.dtype)


def _matmul(x, w, layer, out_dtype, name, col_scale=None):
    n, kdim = x.shape
    m = w.shape[2]
    tm = min(1024, n)
    tn = 512
    in_specs = [pl.BlockSpec((tm, kdim), lambda i, j: (i, 0)),
                pl.BlockSpec((None, kdim, tn), lambda i, j: (layer, 0, j))]
    args = (x, w)
    if col_scale is not None:
        in_specs.append(pl.BlockSpec((1, tn), lambda i, j: (0, j)))
        args += (col_scale,)
    return pl.pallas_call(
        _mm_kernel if col_scale is None else _mm_colscale_kernel, grid=(n // tm, m // tn),
        in_specs=in_specs,
        out_specs=pl.BlockSpec((tm, tn), lambda i, j: (i, j)),
        out_shape=jax.ShapeDtypeStruct((n, m), out_dtype),
        compiler_params=_cparams("parallel", "parallel"), name=name,
    )(*args)


PLE_ROW_GROUPS = 2


def _ple_kernel(hb_ref, wg_ref, p_ref, wp_ref, h_ref, o_ref, *, alpha):
    rows = o_ref.shape[0] // PLE_ROW_GROUPS
    groups = [slice(r * rows, (r + 1) * rows) for r in range(PLE_ROW_GROUPS)]
    dots = [(_dot(hb_ref[g, :], wg_ref[...]), _dot(p_ref[g, :].astype(BF16), wp_ref[...])) for g in groups]
    for g, (gate, proj) in zip(groups, dots):
        o_ref[g, :] = alpha * h_ref[g, :] + jax.nn.sigmoid(gate) * proj


def _ple_resid(hb, w_gate, p, w_proj, layer, h, alpha):
    n, d = hb.shape
    pd = p.shape[2]
    tm = min(1024, n)
    tn = 512
    tile = pl.BlockSpec((tm, tn), lambda i, j: (i, j))
    return pl.pallas_call(
        functools.partial(_ple_kernel, alpha=alpha), grid=(n // tm, d // tn),
        in_specs=[
            pl.BlockSpec((tm, d), lambda i, j: (i, 0)),
            pl.BlockSpec((None, d, tn), lambda i, j: (layer, 0, j)),
            pl.BlockSpec((None, tm, pd), lambda i, j: (layer, i, 0)),
            pl.BlockSpec((None, pd, tn), lambda i, j: (layer, 0, j)),
            tile,
        ],
        out_specs=tile,
        out_shape=jax.ShapeDtypeStruct((n, d), F32),
        compiler_params=_cparams("parallel", "parallel"), name="ple_resid",
    )(hb, w_gate, p, w_proj, h)


def _merge_kernel(hb_ref, oa_ref, ob_ref, oc_ref, g0_ref, g1_ref, g2_ref, wa_ref, wb_ref, wc_ref, o_ref):
    x = hb_ref[...]
    acc = jax.nn.sigmoid(_dot(x, g0_ref[...])) * _dot(oa_ref[...], wa_ref[...])
    acc += jax.nn.sigmoid(_dot(x, g1_ref[...])) * _dot(ob_ref[...], wb_ref[...])
    acc += jax.nn.sigmoid(_dot(x, g2_ref[...])) * _dot(oc_ref[...], wc_ref[...])
    o_ref[...] = acc.astype(o_ref.dtype)


def _merge(hb, o_a, o_b, o_c, w_bg, w_oa, w_ob, w_oc, layer):
    n, d = hb.shape
    tm = min(512, n)
    tn = 512
    nj = d // tn

    def rows(w):
        return pl.BlockSpec((tm, w), lambda i, j: (i, 0))

    def cols(kdim, off):
        return pl.BlockSpec((None, kdim, tn), lambda i, j: (layer, 0, off + j))

    return pl.pallas_call(
        _merge_kernel, grid=(n // tm, nj),
        in_specs=[rows(d), rows(o_a.shape[1]), rows(o_b.shape[1]), rows(o_c.shape[1]),
                  cols(d, 0), cols(d, nj), cols(d, 2 * nj),
                  cols(w_oa.shape[1], 0), cols(w_ob.shape[1], 0), cols(w_oc.shape[1], 0)],
        out_specs=pl.BlockSpec((tm, tn), lambda i, j: (i, j)),
        out_shape=jax.ShapeDtypeStruct((n, d), BF16),
        compiler_params=_cparams("parallel", "parallel"), name="merge",
    )(hb, o_a, o_b, o_c, w_bg, w_bg, w_bg, w_oa, w_ob, w_oc)


def _rope_tab_kernel(pos_ref, inv_ref, c_ref, s_ref):
    ang = pos_ref[...].astype(F32) * inv_ref[...]
    lane = lax.broadcasted_iota(jnp.int32, ang.shape, 1)
    rope = (lane >= MLA_NOPE_DIM) & (lane < MLA_NOPE_DIM + MLA_ROPE_DIM)
    c_ref[...] = jnp.where(lane < MLA_NOPE_DIM, 1.0, jnp.where(rope, jnp.cos(ang), 0.0))
    s_ref[...] = jnp.where(rope, jnp.sin(ang), 0.0)


def _rope_tables(positions):
    n = positions.size
    half = MLA_ROPE_DIM // 2
    inv = ROPE_THETA ** (-np.arange(half, dtype=np.float32) / half)
    inv_row = np.zeros((1, MLA_HEAD_PAD), np.float32)
    inv_row[0, MLA_NOPE_DIM:MLA_NOPE_DIM + MLA_ROPE_DIM] = np.concatenate([inv, inv])
    tr = min(512, n)
    tab = pl.BlockSpec((tr, MLA_HEAD_PAD), lambda i: (i, 0))
    return pl.pallas_call(
        _rope_tab_kernel, grid=(n // tr,),
        in_specs=[pl.BlockSpec((tr, 1), lambda i: (i, 0)), pl.BlockSpec((1, MLA_HEAD_PAD), lambda i: (0, 0))],
        out_specs=[tab, tab],
        out_shape=[jax.ShapeDtypeStruct((n, MLA_HEAD_PAD), F32)] * 2,
        compiler_params=_cparams("parallel"), name="rope_tables",
    )(positions.reshape(n, 1), jnp.asarray(inv_row))


def _mla_proj_kernel(cqa_ref, ckv_ref, kpe_ref, kper_ref, c_ref, s_ref, gq_ref, gkv_ref,
                     wq_ref, wqr_ref, wkn_ref, wv_ref, q_out, k_out, v_out):
    cos = c_ref[...]
    sin = s_ref[...]
    xq = _rms_rows(cqa_ref[...], gq_ref[...]).astype(BF16)
    q = _dot(xq, wq_ref[...])
    q_rot = _dot(xq, wqr_ref[...])
    xkv = _rms_rows(ckv_ref[...], gkv_ref[...]).astype(BF16)
    k_nope = _dot(xkv, wkn_ref[...])
    k_rope = kpe_ref[...] * cos + kper_ref[...] * sin
    for h in range(MLA_HEADS):
        sl = slice(h * MLA_HEAD_PAD, (h + 1) * MLA_HEAD_PAD)
        q_out[:, sl] = ((q[:, sl] * cos + q_rot[:, sl] * sin) * MLA_Q_SCALE).astype(BF16)
        k_out[:, sl] = (k_nope[:, sl] + k_rope).astype(BF16)
    v_out[...] = _dot(xkv, wv_ref[...]).astype(BF16)


def _mla_proj(z_f, cos_tab, sin_tab, g_q, g_kv, w_q, w_qrot, w_kn, w_v, layer):
    n = z_f.shape[0]
    tm = min(256, n)

    def zcols(width, off):
        return pl.BlockSpec((tm, width), lambda i: (i, off // width))

    def whole(a):
        return pl.BlockSpec((None,) + a.shape[1:], lambda i: (layer, 0, 0))

    def rows(width):
        return pl.BlockSpec((tm, width), lambda i: (i, 0))

    g_q = g_q.reshape(g_q.shape[0], 1, -1)
    g_kv = g_kv.reshape(g_kv.shape[0], 1, -1)
    return pl.pallas_call(
        _mla_proj_kernel, grid=(n // tm,),
        in_specs=[zcols(MLA_Q_RANK, ZF_CQA), zcols(MLA_KV_RANK, ZF_CKV), zcols(MLA_HEAD_PAD, ZF_KPE),
                  zcols(MLA_HEAD_PAD, ZF_KPER), rows(MLA_HEAD_PAD), rows(MLA_HEAD_PAD),
                  whole(g_q), whole(g_kv), whole(w_q), whole(w_qrot), whole(w_kn), whole(w_v)],
        out_specs=[rows(MLA_QK_W), rows(MLA_QK_W), rows(C_V_W)],
        out_shape=[jax.ShapeDtypeStruct((n, MLA_QK_W), BF16), jax.ShapeDtypeStruct((n, MLA_QK_W), BF16),
                   jax.ShapeDtypeStruct((n, C_V_W), BF16)],
        compiler_params=_cparams("parallel"), name="mla_proj",
    )(z_f, z_f, z_f, z_f, cos_tab, sin_tab, g_q, g_kv, w_q, w_qrot, w_kn, w_v)


ATTN_TILE = 512
ATTN_MAPS_PER_STEP = 6


ATTN_DENOM_ROWS = 16


def _attn_kernel(slopes_ref, lam_ref, q_ref, k_ref, v_ref, g_ref, o_ref, vt_sc, bias_sc, m_sc, acc_sc,
                 *, diff, lam_init, dqk, dv):
    t = ATTN_TILE
    nmap = 2 if diff else 1
    hp = ATTN_MAPS_PER_STEP // nmap
    qi = pl.program_id(2)
    nt = k_ref.shape[1] // t
    if diff:
        slope2 = [slopes_ref[pl.program_id(1) * hp + hh] * LOG2E for hh in range(hp)]

    @pl.when(qi == 0)
    def _():
        for hh in range(hp):
            for j in range(nt):
                vt_sc[hh, j, :dv] = v_ref[0, j * t:(j + 1) * t, hh * dv:(hh + 1) * dv].astype(F32).T.astype(BF16)
                vt_sc[hh, j, dv:] = jnp.ones((ATTN_DENOM_ROWS, t), BF16)
        key = lax.broadcasted_iota(jnp.int32, (t, t), 0)
        qry = lax.broadcasted_iota(jnp.int32, (t, t), 1)
        allowed = (key // CHUNK) <= (qry // CHUNK)
        if diff:
            for hh in range(hp):
                bias_sc[hh, 0] = slope2[hh] * (qry - key).astype(F32)
                bias_sc[hh, 1] = jnp.where(allowed, slope2[hh] * jnp.abs(qry - key).astype(F32), -NEG_BIG)
        else:
            bias_sc[0, 0] = jnp.where(allowed, 0.0, -NEG_BIG)

    qts = []
    for hh in range(hp):
        qt = q_ref[0, :, hh * dqk:(hh + 1) * dqk].astype(F32).T
        if diff:
            feat = lax.broadcasted_iota(jnp.int32, qt.shape, 0)
            qts.append(jnp.where(feat < DA_QK_DIM, qt, 0.0).astype(BF16))
            qts.append(jnp.where(feat >= DA_QK_DIM, qt, 0.0).astype(BF16))
        else:
            qts.append(qt.astype(BF16))

    m_sc[...] = jnp.full(m_sc.shape, NEG_BIG, F32)
    acc_sc[...] = jnp.zeros(acc_sc.shape, F32)

    def step(kj, diag):
        ks = pl.ds(pl.multiple_of(kj * t, t), t)
        scores = []
        for hh in range(hp):
            k = k_ref[0, ks, hh * dqk:(hh + 1) * dqk]
            shift = None
            if diff:
                bias = bias_sc[hh, 1] if diag else bias_sc[hh, 0]
                if not diag:
                    shift = ((qi - kj) * t).astype(F32) * slope2[hh]
            else:
                bias = bias_sc[0, 0] if diag else None
            for i in range(hh * nmap, (hh + 1) * nmap):
                s = _dot(k, qts[i])
                scores.append((s if bias is None else s - bias, shift))
        probs, alphas = [], []
        for i, (s, shift) in enumerate(scores):
            m_prev = m_sc[i]
            m_tile = jnp.max(s, axis=0, keepdims=True)
            m_new = jnp.maximum(m_prev, m_tile if shift is None else m_tile - shift)
            alphas.append(jnp.exp2(m_prev - m_new))
            probs.append(jnp.exp2(s - (m_new if shift is None else m_new + shift)).astype(BF16))
            m_sc[i] = m_new
        for i, p in enumerate(probs):
            acc_sc[i] = alphas[i] * acc_sc[i] + _dot(vt_sc[i // nmap, kj], p)

    def full_step(kj, carry):
        step(kj, False)
        return carry

    lax.fori_loop(0, qi, full_step, 0)
    step(qi, True)

    if diff:
        lp = lam_ref[...]
        lam = (jnp.exp(jnp.sum(lp[0:1] * lp[1:2], axis=1, keepdims=True))
               - jnp.exp(jnp.sum(lp[2:3] * lp[3:4], axis=1, keepdims=True)) + lam_init)
    def normalised(i):
        acc = acc_sc[i]
        return acc[:dv] / acc[dv:dv + 1]

    for hh in range(hp):
        o = normalised(hh * nmap)
        if diff:
            o = o - lam * normalised(hh * nmap + 1)
            o = o * lax.rsqrt(jnp.mean(o * o, axis=0, keepdims=True) + RMS_EPS) * g_ref[...] * (1.0 - lam_init)
        o_ref[0, :, hh * dv:(hh + 1) * dv] = o.T.astype(o_ref.dtype)


def _attention(q_arr, k_arr, v_arr, q_off, k_off, v_off, dqk, dv, heads, *, diff,
               slopes, lam_params, gain_col, lam_init, name):
    b, t, _ = q_arr.shape
    tq = ATTN_TILE
    nmap = 2 if diff else 1
    hp = ATTN_MAPS_PER_STEP // nmap
    assert heads % hp == 0 and q_off % hp == 0 and k_off % hp == 0 and v_off % hp == 0
    nbias = (hp, 2) if diff else (1, 1)
    return pl.pallas_call(
        functools.partial(_attn_kernel, diff=diff, lam_init=lam_init, dqk=dqk, dv=dv),
        grid=(b, heads // hp, t // tq),
        in_specs=[
            pl.BlockSpec(memory_space=pltpu.SMEM),
            pl.BlockSpec(lam_params.shape, lambda bi, g, i: (0, 0)),
            pl.BlockSpec((1, tq, hp * dqk), lambda bi, g, i: (bi, i, q_off // hp + g)),
            pl.BlockSpec((1, t, hp * dqk), lambda bi, g, i: (bi, 0, k_off // hp + g)),
            pl.BlockSpec((1, t, hp * dv), lambda bi, g, i: (bi, 0, v_off // hp + g)),
            pl.BlockSpec(gain_col.shape, lambda bi, g, i: (0, 0)),
        ],
        out_specs=pl.BlockSpec((1, tq, hp * dv), lambda bi, g, i: (bi, i, g)),
        out_shape=jax.ShapeDtypeStruct((b, t, heads * dv), BF16),
        scratch_shapes=[pltpu.VMEM((hp, t // tq, dv + ATTN_DENOM_ROWS, tq), BF16), pltpu.VMEM(nbias + (tq, tq), F32),
                        pltpu.VMEM((hp * nmap, 1, tq), F32),
                        pltpu.VMEM((hp * nmap, dv + ATTN_DENOM_ROWS, tq), F32)],
        compiler_params=_cparams("parallel", "parallel", "arbitrary"), name=name,
    )(slopes, lam_params, q_arr, k_arr, v_arr, gain_col)


def _alibi_slopes(n):
    def pow2_slopes(m):
        start = 2.0 ** (-8.0 / m)
        return [start ** (i + 1) for i in range(m)]
    c = 2 ** int(math.floor(math.log2(n)))
    s = pow2_slopes(c)
    if c < n:
        s = s + pow2_slopes(2 * c)[0::2][: n - c]
    return np.array(s, dtype=np.float32)


GLA_GROUP = 8


def _gla_kernel(q_ref, k_ref, v_ref, r_ref, glr_ref, w2_ref, bg_ref, ng_ref, o_ref):
    c = CHUNK
    gc = GLA_GROUP * c
    scale = GLA_DK ** -0.5
    ri = lax.broadcasted_iota(jnp.int32, (c, c), 0)
    ci = lax.broadcasted_iota(jnp.int32, (c, c), 1)
    causal = ci <= ri
    tri = causal.astype(BF16)
    w2 = w2_ref[...]
    bg = bg_ref[...]
    ng = ng_ref[...]
    chunks = [slice(g * c, (g + 1) * c) for g in range(GLA_GROUP)]

    def body(n, state_t):
        rows = pl.ds(pl.multiple_of(n * gc, gc), gc)
        pre = _dot(glr_ref[0, rows, :].astype(BF16), w2) + bg
        log_a = (jnp.minimum(pre, 0.0) - jnp.log1p(jnp.exp(-jnp.abs(pre)))) / GLA_GATE_NORMALIZER
        hi = log_a.astype(BF16)
        rem = log_a - hi.astype(F32)
        mid = rem.astype(BF16)
        lo = (rem - mid.astype(F32)).astype(BF16)
        bcum = [_dot(tri, hi[s]) + _dot(tri, mid[s]) + _dot(tri, lo[s]) for s in chunks]
        b_mid = [b[c // 2:c // 2 + 1, :] for b in bcum]
        b_last = [b[c - 1:c, :] for b in bcum]
        q = q_ref[0, rows, :] * scale
        k = k_ref[0, rows, :]
        v = v_ref[0, rows, :].astype(BF16)
        att = [_dot_nt((q[s] * jnp.exp(b - bm)).astype(BF16), (k[s] * jnp.exp(bm - b)).astype(BF16))
               for s, b, bm in zip(chunks, bcum, b_mid)]
        att = [jnp.where(causal, a, 0.0).astype(BF16) for a in att]
        o_intra = [_dot(a, v[s]) for a, s in zip(att, chunks)]
        upd_t = [_dot_tn(v[s], (k[s] * jnp.exp(bl - b)).astype(BF16)) for s, b, bl in zip(chunks, bcum, b_last)]
        q_dec = [(q[s] * jnp.exp(b)).astype(BF16) for s, b in zip(chunks, bcum)]
        outs = []
        for g in range(GLA_GROUP):
            outs.append(o_intra[g] + _dot_nt(q_dec[g], state_t.astype(BF16)))
            state_t = state_t * jnp.exp(b_last[g]) + upd_t[g]
        o = jnp.concatenate(outs, axis=0)
        r = r_ref[0, rows, :]
        o_ref[0, rows, :] = (_rms_rows(o, ng) * (r * jax.nn.sigmoid(r))).astype(o_ref.dtype)
        return state_t

    lax.fori_loop(0, q_ref.shape[1] // gc, body, jnp.zeros((GLA_DV, GLA_DK), F32))


def _gla(z_f3, w_gate2_pad, b_gate, norm_g, layer):
    b, t, _ = z_f3.shape

    def zcols(width, off):
        return pl.BlockSpec((1, t, width), lambda bi, h: (bi, 0, off // width + h))

    return pl.pallas_call(
        _gla_kernel, grid=(b, GLA_HEADS),
        in_specs=[zcols(GLA_DK, ZF_BQ), zcols(GLA_DK, ZF_BK), zcols(GLA_DV, ZF_BV), zcols(GLA_DV, ZF_BR),
                  pl.BlockSpec((1, t, ZF_GLR_W), lambda bi, h: (bi, 0, ZF_GLR // ZF_GLR_W)),
                  pl.BlockSpec((None, ZF_GLR_W, GLA_DK), lambda bi, h: (layer, 0, h)),
                  pl.BlockSpec((None, 1, GLA_DK), lambda bi, h: (layer, 0, h)),
                  pl.BlockSpec((None, 1, GLA_DV), lambda bi, h: (layer, 0, 0))],
        out_specs=pl.BlockSpec((1, t, GLA_DV), lambda bi, h: (bi, 0, h)),
        out_shape=jax.ShapeDtypeStruct((b, t, B_V_W), BF16),
        compiler_params=_cparams("parallel", "parallel"), name="gla",
    )(z_f3, z_f3, z_f3, z_f3, z_f3, w_gate2_pad, b_gate.reshape(b_gate.shape[0], 1, -1),
      norm_g.reshape(norm_g.shape[0], 1, -1))


def _rot_half_cols(w):
    half = w.shape[-1] // 2
    return jnp.concatenate([-w[..., half:], w[..., :half]], axis=-1)


IN_SPLITS = (A_QK_W, A_QK_W, A_V_W, B_K_W, B_K_W, B_V_W, GLA_GATE_RANK, B_V_W, MLA_Q_RANK, MLA_KV_RANK + MLA_ROPE_DIM)


def _prep_in_kernel(w_ref, wa_ref, wf_ref):
    offs = [0] + list(np.cumsum(IN_SPLITS))
    (b_q, b_k, b_v, b_glr, b_r, c_qa, c_kva) = [(int(offs[i]), int(offs[i + 1])) for i in range(3, 10)]
    half = MLA_ROPE_DIM // 2

    def cols(lo, hi):
        return w_ref[:, lo:hi].astype(BF16)

    wa_ref[...] = cols(0, int(offs[3]))
    wf_ref[...] = jnp.zeros(wf_ref.shape, BF16)
    for dst, (lo, hi) in ((ZF_CQA, c_qa), (ZF_BQ, b_q), (ZF_BK, b_k), (ZF_BV, b_v), (ZF_BR, b_r),
                          (ZF_CKV, (c_kva[0], c_kva[0] + MLA_KV_RANK)), (ZF_GLR, b_glr)):
        wf_ref[:, dst:dst + hi - lo] = cols(lo, hi)
    pe = c_kva[0] + MLA_KV_RANK
    wf_ref[:, ZF_KPE + MLA_NOPE_DIM:ZF_KPE + MLA_NOPE_DIM + MLA_ROPE_DIM] = cols(pe, pe + MLA_ROPE_DIM)
    wf_ref[:, ZF_KPER + MLA_NOPE_DIM:ZF_KPER + MLA_NOPE_DIM + half] = -cols(pe + half, pe + MLA_ROPE_DIM)
    wf_ref[:, ZF_KPER + MLA_NOPE_DIM + half:ZF_KPER + MLA_NOPE_DIM + MLA_ROPE_DIM] = cols(pe, pe + half)


def _prep_in_proj(w_in):
    nl, d, width = w_in.shape
    assert width == sum(IN_SPLITS)
    a_w = sum(IN_SPLITS[:3])
    tr = min(256, d)
    return pl.pallas_call(
        _prep_in_kernel, grid=(nl, d // tr),
        in_specs=[pl.BlockSpec((None, tr, width), lambda l, i: (l, i, 0))],
        out_specs=[pl.BlockSpec((None, tr, a_w), lambda l, i: (l, i, 0)),
                   pl.BlockSpec((None, tr, ZF_W), lambda l, i: (l, i, 0))],
        out_shape=[jax.ShapeDtypeStruct((nl, d, a_w), BF16), jax.ShapeDtypeStruct((nl, d, ZF_W), BF16)],
        compiler_params=_cparams("parallel", "parallel"), name="prep_in_proj",
    )(w_in)


def _prep_mla(w_qb, w_kvb):
    lead = w_qb.shape[:-1]
    wq = w_qb.reshape(lead + (MLA_HEADS, MLA_NOPE_DIM + MLA_ROPE_DIM))
    nope, rope = wq[..., :MLA_NOPE_DIM], wq[..., MLA_NOPE_DIM:]
    pad = jnp.zeros(lead + (MLA_HEADS, MLA_HEAD_PAD - MLA_NOPE_DIM - MLA_ROPE_DIM), w_qb.dtype)
    w_q = jnp.concatenate([nope, rope, pad], axis=-1).reshape(lead + (MLA_QK_W,))
    w_qrot = jnp.concatenate([jnp.zeros_like(nope), _rot_half_cols(rope), pad], axis=-1).reshape(lead + (MLA_QK_W,))
    lead = w_kvb.shape[:-1]
    wkv = w_kvb.reshape(lead + (MLA_HEADS, MLA_NOPE_DIM + MLA_V_DIM))
    k_nope, v = wkv[..., :MLA_NOPE_DIM], wkv[..., MLA_NOPE_DIM:]
    w_kn = jnp.concatenate([k_nope, jnp.zeros(lead + (MLA_HEADS, MLA_HEAD_PAD - MLA_NOPE_DIM), w_kvb.dtype)],
                           axis=-1).reshape(lead + (MLA_QK_W,))
    return w_q.astype(BF16), w_qrot.astype(BF16), w_kn.astype(BF16), v.reshape(lead + (C_V_W,)).astype(BF16)


FFN_SRC_BLOCK = 256


def _cast_gu_kernel(*refs, nb, k):
    srcs, o_ref = refs[:-1], refs[-1]
    f = pl.program_id(1)
    sb = srcs[0].shape[1]
    for j, src in enumerate(srcs):
        x = src[...].astype(BF16)
        s = j % k
        if nb % k and s >= nb % k:
            x = jnp.where(f * k + s < nb, x, jnp.zeros_like(x))
        o_ref[:, j * sb:(j + 1) * sb] = x


def _prep_ffn_gu(w_gu, tf):
    nl, two, d, f2 = w_gu.shape
    ff = f2 // 2
    sb = FFN_SRC_BLOCK
    assert ff % sb == 0 and tf % sb == 0
    nb, k = ff // sb, tf // sb
    nf = -(-nb // k)

    def cols(off, s):
        return pl.BlockSpec((None, None, d, sb),
                            lambda lj, f: (lj // two, lj % two, 0, off + jnp.minimum(f * k + s, nb - 1)))

    return pl.pallas_call(
        functools.partial(_cast_gu_kernel, nb=nb, k=k), grid=(nl * two, nf),
        in_specs=[cols(off, s) for off in (0, nb) for s in range(k)],
        out_specs=pl.BlockSpec((None, None, None, d, 2 * tf), lambda lj, f: (lj // two, lj % two, f, 0, 0)),
        out_shape=jax.ShapeDtypeStruct((nl, two, nf, d, 2 * tf), BF16),
        compiler_params=_cparams("parallel", "parallel"), name="cast_gu",
    )(*([w_gu] * (2 * k)))


def kernel(x, p, positions, emb_ln_g, emb_ln_b, w_in, w_branch_gate, da_lambda_q1, da_lambda_k1, da_lambda_q2,
           da_lambda_k2, da_subln_g, gla_w_gate2, gla_b_gate, gla_norm_g, mla_q_norm_g, mla_w_qb, mla_kv_norm_g,
           mla_w_kvb, w_o_a, w_o_b, w_o_c, w_out, ffn_w_gu, ffn_w_down, ln_g, ln_b, ple_w_proj, ple_w_gate):
    b, t, d = x.shape
    n = b * t
    depth = w_in.shape[0]
    alpha = float(DEEPNORM_ALPHA)

    w_gu = _prep_ffn_gu(ffn_w_gu, FFN_CHUNK)
    ff_pad = w_gu.shape[2] * FFN_CHUNK - ffn_w_down.shape[2]
    w_down = jnp.pad(ffn_w_down.astype(BF16), ((0, 0), (0, 0), (0, ff_pad), (0, 0)))
    w_a, w_f = _prep_in_proj(w_in)
    w_q, w_qrot, w_kn, w_v = _prep_mla(mla_w_qb, mla_w_kvb)
    w_bg = w_branch_gate.astype(BF16)
    w_oa, w_ob, w_oc, w_o = (w.astype(BF16) for w in (w_o_a, w_o_b, w_o_c, w_out))
    w_pg, w_pp = ple_w_gate.astype(BF16), ple_w_proj.astype(BF16)
    w2_pad = jnp.pad(gla_w_gate2.astype(BF16), ((0, 0), (0, ZF_GLR_W - GLA_GATE_RANK), (0, 0)))
    p = p.reshape(depth, n, -1)

    cos_tab, sin_tab = _rope_tables(positions)
    slopes = jnp.asarray(_alibi_slopes(DA_HEADS))
    a_scale = np.ones((1, 2 * A_QK_W + A_V_W), np.float32)
    a_scale[0, :A_QK_W] = DA_Q_SCALE
    a_scale = jnp.asarray(a_scale)
    no_slopes = jnp.zeros((1,), F32)
    no_lam = jnp.zeros((4, DA_QK_DIM), F32)
    no_gain = jnp.ones((MLA_V_DIM, 1), F32)

    h, hb = _layer_norm(x.reshape(n, d), emb_ln_g, emb_ln_b)
    for i in range(depth):
        h, hb = _ffn_ln(h, hb, w_gu, w_down, i, 0, ln_g[i, 0], ln_b[i, 0], alpha)

        z_a = _matmul(hb, w_a, i, BF16, "in_proj_a", col_scale=a_scale).reshape(b, t, -1)
        z_f = _matmul(hb, w_f, i, F32, "in_proj_f")
        lam_init = 0.8 - 0.6 * math.exp(-0.3 * i)
        lam_params = jnp.stack([da_lambda_q1[i], da_lambda_k1[i], da_lambda_q2[i], da_lambda_k2[i]]).astype(F32)
        o_a = _attention(z_a, z_a, z_a, 0, DA_HEADS, 2 * DA_HEADS, 2 * DA_QK_DIM, DA_V_DIM, DA_HEADS,
                         diff=True, slopes=slopes, lam_params=lam_params,
                         gain_col=da_subln_g[i].reshape(-1, 1), lam_init=lam_init, name="diff_attn")
        o_b = _gla(z_f.reshape(b, t, ZF_W), w2_pad, gla_b_gate, gla_norm_g, i)
        q_c, k_c, v_c = _mla_proj(z_f, cos_tab, sin_tab, mla_q_norm_g, mla_kv_norm_g, w_q, w_qrot, w_kn, w_v, i)
        o_c = _attention(q_c.reshape(b, t, -1), k_c.reshape(b, t, -1), v_c.reshape(b, t, -1), 0, 0, 0,
                         MLA_HEAD_PAD, MLA_V_DIM, MLA_HEADS, diff=False,
                         slopes=no_slopes, lam_params=no_lam,
                         gain_col=no_gain, lam_init=0.0, name="mla_attn")
        merged = _merge(hb, o_a.reshape(n, -1), o_b.reshape(n, -1), o_c.reshape(n, -1), w_bg, w_oa, w_ob, w_oc, i)
        h, hb = _proj_ln(h, merged, w_o, i, ln_g[i, 1], ln_b[i, 1], alpha)

        resid = _ple_resid(hb, w_pg, p, w_pp, i, h, alpha)
        h, hb = _ffn_ln(resid, hb, w_gu, w_down, i, 1, ln_g[i, 2], ln_b[i, 2], 1.0)
    return h.reshape(b, t, d)
```

```python
import functools
import math

import numpy as np
import jax
import jax.numpy as jnp
from jax import lax
from jax.experimental import pallas as pl
from jax.experimental.pallas import tpu as pltpu

F32 = jnp.float32
BF16 = jnp.bfloat16

DEPTH = 2
CHUNK = 64
DA_HEADS = 12
DA_QK_DIM = 64
DA_V_DIM = 128
GLA_HEADS = 4
GLA_DK = 128
GLA_DV = 256
GLA_GATE_RANK = 16
GLA_GATE_NORMALIZER = 16.0
MLA_HEADS = 12
MLA_Q_RANK = 768
MLA_KV_RANK = 512
MLA_NOPE_DIM = 128
MLA_ROPE_DIM = 64
MLA_V_DIM = 128
ROPE_THETA = 10000.0
LN_EPS = 1e-5
RMS_EPS = 1e-6
DEEPNORM_ALPHA = (2 * DEPTH) ** 0.25

A_QK_W = DA_HEADS * 2 * DA_QK_DIM
A_V_W = DA_HEADS * DA_V_DIM
B_K_W = GLA_HEADS * GLA_DK
B_V_W = GLA_HEADS * GLA_DV
C_V_W = MLA_HEADS * MLA_V_DIM
MLA_HEAD_PAD = 256
MLA_QK_W = MLA_HEADS * MLA_HEAD_PAD

ZF_CQA = 0
ZF_KPE = 768
ZF_BQ = 1024
ZF_BK = 1536
ZF_BV = 2048
ZF_BR = 3072
ZF_CKV = 4096
ZF_KPER = 4608
ZF_GLR = 4864
ZF_GLR_W = 128
ZF_W = 5120

VMEM_CAP_V7X = 64 * 1024 * 1024
VMEM_LIMIT = VMEM_CAP_V7X - 8 * 1024 * 1024
VMEM_LIMIT_FFN = VMEM_CAP_V7X - 4 * 1024 * 1024
FFN_CHUNK = 256
FFN_ROW_GROUPS = 2
NEG_BIG = -1e30
LOG2E = math.log2(math.e)
DA_Q_SCALE = DA_QK_DIM ** -0.5 * LOG2E
MLA_Q_SCALE = (MLA_NOPE_DIM + MLA_ROPE_DIM) ** -0.5 * LOG2E


def _cparams(*sem, vmem=VMEM_LIMIT):
    return pltpu.CompilerParams(dimension_semantics=sem, vmem_limit_bytes=vmem)


def _dot(a, b):
    return jnp.dot(a, b, preferred_element_type=F32)


def _dot_nt(a, b):
    return lax.dot_general(a, b, (((1,), (1,)), ((), ())), preferred_element_type=F32)


def _dot_tn(a, b):
    return lax.dot_general(a, b, (((0,), (0,)), ((), ())), preferred_element_type=F32)


def _ln_rows(y, g, b):
    mu = jnp.mean(y, axis=-1, keepdims=True)
    yc = y - mu
    var = jnp.mean(yc * yc, axis=-1, keepdims=True)
    return yc * lax.rsqrt(var + LN_EPS) * g + b


def _rms_rows(y, g):
    return y * lax.rsqrt(jnp.mean(y * y, axis=-1, keepdims=True) + RMS_EPS) * g


def _ln_kernel(x_ref, g_ref, b_ref, o_ref, ob_ref):
    y = _ln_rows(x_ref[...], g_ref[...], b_ref[...])
    o_ref[...] = y
    ob_ref[...] = y.astype(BF16)


def _layer_norm(x, g, b):
    n, d = x.shape
    tr = min(256, n)
    row = pl.BlockSpec((tr, d), lambda i: (i, 0))
    vec = pl.BlockSpec((1, d), lambda i: (0, 0))
    return pl.pallas_call(
        _ln_kernel, grid=(n // tr,), in_specs=[row, vec, vec], out_specs=[row, row],
        out_shape=[jax.ShapeDtypeStruct((n, d), F32), jax.ShapeDtypeStruct((n, d), BF16)],
        compiler_params=_cparams("parallel"), name="ln_emb",
    )(x, g.reshape(1, d), b.reshape(1, d))


LN_SLAB = 128


def _resid_copy(resid_hbm, dst_ref, sem, block):
    tm = dst_ref.shape[0]
    start = pl.multiple_of(block * tm, tm)
    return pltpu.make_async_copy(resid_hbm.at[pl.ds(start, tm), :], dst_ref, sem)


def _ln_epilogue(o_ref, ob_ref, g_ref, b_ref, out_scale):
    g = g_ref[...]
    b = b_ref[...]

    def body(r, carry):
        sl = pl.ds(pl.multiple_of(r * LN_SLAB, LN_SLAB), LN_SLAB)
        y = _ln_rows(o_ref[sl, :] * out_scale, g, b)
        o_ref[sl, :] = y
        ob_ref[sl, :] = y.astype(BF16)
        return carry

    lax.fori_loop(0, o_ref.shape[0] // LN_SLAB, body, 0)


def _ln_epilogue_writeback(o_ref, ob_ref, out_hbm, outb_hbm, sem_o, sem_b, g_ref, b_ref, out_scale,
                           resid_hbm, r_sc, sem_r, res_scale):
    g = g_ref[...]
    b = b_ref[...]
    tm = o_ref.shape[0]
    row0 = pl.program_id(0) * tm

    def copies(r):
        src = pl.ds(pl.multiple_of(r * LN_SLAB, LN_SLAB), LN_SLAB)
        dst = pl.ds(pl.multiple_of(row0 + r * LN_SLAB, LN_SLAB), LN_SLAB)
        return (pltpu.make_async_copy(o_ref.at[src, :], out_hbm.at[dst, :], sem_o),
                pltpu.make_async_copy(ob_ref.at[src, :], outb_hbm.at[dst, :], sem_b))

    def resid_copy(r):
        src = pl.ds(pl.multiple_of(row0 + r * LN_SLAB, LN_SLAB), LN_SLAB)
        return pltpu.make_async_copy(resid_hbm.at[src, :], r_sc.at[r % 2], sem_r.at[r % 2])

    nslab = tm // LN_SLAB
    resid_copy(0).start()

    def body(r, carry):
        sl = pl.ds(pl.multiple_of(r * LN_SLAB, LN_SLAB), LN_SLAB)

        @pl.when(r + 1 < nslab)
        def _():
            resid_copy(r + 1).start()

        resid_copy(r).wait()
        y = _ln_rows(r_sc[r % 2] * res_scale + o_ref[sl, :] * out_scale, g, b)
        o_ref[sl, :] = y
        ob_ref[sl, :] = y.astype(BF16)
        for cp in copies(r):
            cp.start()
        return carry

    def drain(r, carry):
        for cp in copies(r):
            cp.wait()
        return carry

    lax.fori_loop(0, nslab, body, 0)
    lax.fori_loop(0, nslab, drain, 0)


def _ffn_kernel(resid_hbm, hb_ref, wgu_ref, wd_ref, g_ref, b_ref, out_hbm, outb_hbm, o_ref, ob_ref, r_sc,
                sem_r, sem_o, sem_b, *, res_scale):
    f = pl.program_id(1)
    tf = wd_ref.shape[0]

    @pl.when(f == 0)
    def _():
        o_ref[...] = jnp.zeros(o_ref.shape, F32)

    rows = o_ref.shape[0] // FFN_ROW_GROUPS
    groups = [slice(r * rows, (r + 1) * rows) for r in range(FFN_ROW_GROUPS)]
    gus = [_dot(hb_ref[g, :], wgu_ref[...]) for g in groups]
    acts = [(gu[:, :tf] * jax.nn.sigmoid(gu[:, :tf]) * gu[:, tf:]).astype(BF16) for gu in gus]
    for g, act in zip(groups, acts):
        o_ref[g, :] += _dot(act, wd_ref[...])

    @pl.when(f == pl.num_programs(1) - 1)
    def _():
        _ln_epilogue_writeback(o_ref, ob_ref, out_hbm, outb_hbm, sem_o, sem_b, g_ref, b_ref, 0.5,
                               resid_hbm, r_sc, sem_r, res_scale)


def _ffn_ln(resid, hb, w_gu, w_down, layer, half, ln_g, ln_b, res_scale):
    n, d = hb.shape
    nf, tf = w_gu.shape[2], w_gu.shape[4] // 2
    tm = min(1024, n)
    vec = pl.BlockSpec((1, d), lambda i, f: (0, 0))
    hbm = pl.BlockSpec(memory_space=pl.ANY)
    return pl.pallas_call(
        functools.partial(_ffn_kernel, res_scale=res_scale),
        grid=(n // tm, nf),
        in_specs=[
            pl.BlockSpec(memory_space=pl.ANY),
            pl.BlockSpec((tm, d), lambda i, f: (i, 0)),
            pl.BlockSpec((None, None, None, d, 2 * tf), lambda i, f: (layer, half, f, 0, 0)),
            pl.BlockSpec((None, None, tf, d), lambda i, f: (layer, half, f, 0)),
            vec, vec,
        ],
        out_specs=[hbm, hbm],
        out_shape=[jax.ShapeDtypeStruct((n, d), F32), jax.ShapeDtypeStruct((n, d), BF16)],
        scratch_shapes=[pltpu.VMEM((tm, d), F32), pltpu.VMEM((tm, d), BF16), pltpu.VMEM((2, LN_SLAB, d), F32),
                        pltpu.SemaphoreType.DMA((2,)), pltpu.SemaphoreType.DMA(()), pltpu.SemaphoreType.DMA(())],
        compiler_params=_cparams("arbitrary", "arbitrary", vmem=VMEM_LIMIT_FFN), name="ffn_ln",
    )(resid, hb, w_gu, w_down, ln_g.reshape(1, d), ln_b.reshape(1, d))


def _proj_ln_kernel(resid_hbm, x_ref, w_ref, g_ref, b_ref, o_ref, ob_ref, r_sc, sem, *, res_scale):
    i = pl.program_id(0)
    k = pl.program_id(1)

    @pl.when((i == 0) & (k == 0))
    def _():
        _resid_copy(resid_hbm, r_sc, sem, 0).start()

    @pl.when((k == 1) & (i + 1 < pl.num_programs(0)))
    def _():
        _resid_copy(resid_hbm, r_sc, sem, i + 1).start()

    @pl.when(k == 0)
    def _():
        _resid_copy(resid_hbm, r_sc, sem, i).wait()
        o_ref[...] = r_sc[...] * res_scale + _dot(x_ref[...], w_ref[...])

    @pl.when(k > 0)
    def _():
        o_ref[...] += _dot(x_ref[...], w_ref[...])

    @pl.when(k == pl.num_programs(1) - 1)
    def _():
        _ln_epilogue(o_ref, ob_ref, g_ref, b_ref, 1.0)


def _proj_ln(resid, x, w, layer, ln_g, ln_b, res_scale):
    n, kdim = x.shape
    d = w.shape[2]
    tm = min(512, n)
    tk = 1024
    assert kdim // tk >= 2
    vec = pl.BlockSpec((1, d), lambda i, k: (0, 0))
    row = pl.BlockSpec((tm, d), lambda i, k: (i, 0))
    return pl.pallas_call(
        functools.partial(_proj_ln_kernel, res_scale=res_scale),
        grid=(n // tm, kdim // tk),
        in_specs=[
            pl.BlockSpec(memory_space=pl.ANY),
            pl.BlockSpec((tm, tk), lambda i, k: (i, k)),
            pl.BlockSpec((None, tk, d), lambda i, k: (layer, k, 0)),
            vec, vec,
        ],
        out_specs=[row, row],
        out_shape=[jax.ShapeDtypeStruct((n, d), F32), jax.ShapeDtypeStruct((n, d), BF16)],
        scratch_shapes=[pltpu.VMEM((tm, d), F32), pltpu.SemaphoreType.DMA(())],
        compiler_params=_cparams("arbitrary", "arbitrary"), name="proj_ln",
    )(resid, x, w, ln_g.reshape(1, d), ln_b.reshape(1, d))


def _mm_kernel(x_ref, w_ref, o_ref):
    o_ref[...] = _dot(x_ref[...], w_ref[...]).astype(o_ref.dtype)


def _mm_colscale_kernel(x_ref, w_ref, s_ref, o_ref):
    o_ref[...] = (_dot(x_ref[...], w_ref[...]) * s_ref[...]).astype(o_ref.dtype)


def _matmul(x, w, layer, out_dtype, name, col_scale=None):
    n, kdim = x.shape
    m = w.shape[2]
    tm = min(1024, n)
    tn = 512
    in_specs = [pl.BlockSpec((tm, kdim), lambda i, j: (i, 0)),
                pl.BlockSpec((None, kdim, tn), lambda i, j: (layer, 0, j))]
    args = (x, w)
    if col_scale is not None:
        in_specs.append(pl.BlockSpec((1, tn), lambda i, j: (0, j)))
        args += (col_scale,)
    return pl.pallas_call(
        _mm_kernel if col_scale is None else _mm_colscale_kernel, grid=(n // tm, m // tn),
        in_specs=in_specs,
        out_specs=pl.BlockSpec((tm, tn), lambda i, j: (i, j)),
        out_shape=jax.ShapeDtypeStruct((n, m), out_dtype),
        compiler_params=_cparams("parallel", "parallel"), name=name,
    )(*args)


PLE_ROW_GROUPS = 2


def _ple_kernel(hb_ref, wg_ref, p_ref, wp_ref, h_ref, o_ref, *, alpha):
    rows = o_ref.shape[0] // PLE_ROW_GROUPS
    groups = [slice(r * rows, (r + 1) * rows) for r in range(PLE_ROW_GROUPS)]
    dots = [(_dot(hb_ref[g, :], wg_ref[...]), _dot(p_ref[g, :].astype(BF16), wp_ref[...])) for g in groups]
    for g, (gate, proj) in zip(groups, dots):
        o_ref[g, :] = alpha * h_ref[g, :] + jax.nn.sigmoid(gate) * proj


def _ple_resid(hb, w_gate, p, w_proj, layer, h, alpha):
    n, d = hb.shape
    pd = p.shape[2]
    tm = min(1024, n)
    tn = 512
    tile = pl.BlockSpec((tm, tn), lambda i, j: (i, j))
    return pl.pallas_call(
        functools.partial(_ple_kernel, alpha=alpha), grid=(n // tm, d // tn),
        in_specs=[
            pl.BlockSpec((tm, d), lambda i, j: (i, 0)),
            pl.BlockSpec((None, d, tn), lambda i, j: (layer, 0, j)),
            pl.BlockSpec((None, tm, pd), lambda i, j: (layer, i, 0)),
            pl.BlockSpec((None, pd, tn), lambda i, j: (layer, 0, j)),
            tile,
        ],
        out_specs=tile,
        out_shape=jax.ShapeDtypeStruct((n, d), F32),
        compiler_params=_cparams("parallel", "parallel"), name="ple_resid",
    )(hb, w_gate, p, w_proj, h)


def _merge_kernel(hb_ref, oa_ref, ob_ref, oc_ref, g0_ref, g1_ref, g2_ref, wa_ref, wb_ref, wc_ref, o_ref):
    x = hb_ref[...]
    acc = jax.nn.sigmoid(_dot(x, g0_ref[...])) * _dot(oa_ref[...], wa_ref[...])
    acc += jax.nn.sigmoid(_dot(x, g1_ref[...])) * _dot(ob_ref[...], wb_ref[...])
    acc += jax.nn.sigmoid(_dot(x, g2_ref[...])) * _dot(oc_ref[...], wc_ref[...])
    o_ref[...] = acc.astype(o_ref.dtype)


def _merge(hb, o_a, o_b, o_c, w_bg, w_oa, w_ob, w_oc, layer):
    n, d = hb.shape
    tm = min(512, n)
    tn = 512
    nj = d // tn

    def rows(w):
        return pl.BlockSpec((tm, w), lambda i, j: (i, 0))

    def cols(kdim, off):
        return pl.BlockSpec((None, kdim, tn), lambda i, j: (layer, 0, off + j))

    return pl.pallas_call(
        _merge_kernel, grid=(n // tm, nj),
        in_specs=[rows(d), rows(o_a.shape[1]), rows(o_b.shape[1]), rows(o_c.shape[1]),
                  cols(d, 0), cols(d, nj), cols(d, 2 * nj),
                  cols(w_oa.shape[1], 0), cols(w_ob.shape[1], 0), cols(w_oc.shape[1], 0)],
        out_specs=pl.BlockSpec((tm, tn), lambda i, j: (i, j)),
        out_shape=jax.ShapeDtypeStruct((n, d), BF16),
        compiler_params=_cparams("parallel", "parallel"), name="merge",
    )(hb, o_a, o_b, o_c, w_bg, w_bg, w_bg, w_oa, w_ob, w_oc)


def _rope_tab_kernel(pos_ref, inv_ref, c_ref, s_ref):
    ang = pos_ref[...].astype(F32) * inv_ref[...]
    lane = lax.broadcasted_iota(jnp.int32, ang.shape, 1)
    rope = (lane >= MLA_NOPE_DIM) & (lane < MLA_NOPE_DIM + MLA_ROPE_DIM)
    c_ref[...] = jnp.where(lane < MLA_NOPE_DIM, 1.0, jnp.where(rope, jnp.cos(ang), 0.0))
    s_ref[...] = jnp.where(rope, jnp.sin(ang), 0.0)


def _rope_tables(positions):
    n = positions.size
    half = MLA_ROPE_DIM // 2
    inv = ROPE_THETA ** (-np.arange(half, dtype=np.float32) / half)
    inv_row = np.zeros((1, MLA_HEAD_PAD), np.float32)
    inv_row[0, MLA_NOPE_DIM:MLA_NOPE_DIM + MLA_ROPE_DIM] = np.concatenate([inv, inv])
    tr = min(512, n)
    tab = pl.BlockSpec((tr, MLA_HEAD_PAD), lambda i: (i, 0))
    return pl.pallas_call(
        _rope_tab_kernel, grid=(n // tr,),
        in_specs=[pl.BlockSpec((tr, 1), lambda i: (i, 0)), pl.BlockSpec((1, MLA_HEAD_PAD), lambda i: (0, 0))],
        out_specs=[tab, tab],
        out_shape=[jax.ShapeDtypeStruct((n, MLA_HEAD_PAD), F32)] * 2,
        compiler_params=_cparams("parallel"), name="rope_tables",
    )(positions.reshape(n, 1), jnp.asarray(inv_row))


def _mla_proj_kernel(cqa_ref, ckv_ref, kpe_ref, kper_ref, c_ref, s_ref, gq_ref, gkv_ref,
                     wq_ref, wqr_ref, wkn_ref, wv_ref, q_out, k_out, v_out):
    cos = c_ref[...]
    sin = s_ref[...]
    xq = _rms_rows(cqa_ref[...], gq_ref[...]).astype(BF16)
    q = _dot(xq, wq_ref[...])
    q_rot = _dot(xq, wqr_ref[...])
    xkv = _rms_rows(ckv_ref[...], gkv_ref[...]).astype(BF16)
    k_nope = _dot(xkv, wkn_ref[...])
    k_rope = kpe_ref[...] * cos + kper_ref[...] * sin
    for h in range(MLA_HEADS):
        sl = slice(h * MLA_HEAD_PAD, (h + 1) * MLA_HEAD_PAD)
        q_out[:, sl] = ((q[:, sl] * cos + q_rot[:, sl] * sin) * MLA_Q_SCALE).astype(BF16)
        k_out[:, sl] = (k_nope[:, sl] + k_rope).astype(BF16)
    v_out[...] = _dot(xkv, wv_ref[...]).astype(BF16)


def _mla_proj(z_f, cos_tab, sin_tab, g_q, g_kv, w_q, w_qrot, w_kn, w_v, layer):
    n = z_f.shape[0]
    tm = min(256, n)

    def zcols(width, off):
        return pl.BlockSpec((tm, width), lambda i: (i, off // width))

    def whole(a):
        return pl.BlockSpec((None,) + a.shape[1:], lambda i: (layer, 0, 0))

    def rows(width):
        return pl.BlockSpec((tm, width), lambda i: (i, 0))

    g_q = g_q.reshape(g_q.shape[0], 1, -1)
    g_kv = g_kv.reshape(g_kv.shape[0], 1, -1)
    return pl.pallas_call(
        _mla_proj_kernel, grid=(n // tm,),
        in_specs=[zcols(MLA_Q_RANK, ZF_CQA), zcols(MLA_KV_RANK, ZF_CKV), zcols(MLA_HEAD_PAD, ZF_KPE),
                  zcols(MLA_HEAD_PAD, ZF_KPER), rows(MLA_HEAD_PAD), rows(MLA_HEAD_PAD),
                  whole(g_q), whole(g_kv), whole(w_q), whole(w_qrot), whole(w_kn), whole(w_v)],
        out_specs=[rows(MLA_QK_W), rows(MLA_QK_W), rows(C_V_W)],
        out_shape=[jax.ShapeDtypeStruct((n, MLA_QK_W), BF16), jax.ShapeDtypeStruct((n, MLA_QK_W), BF16),
                   jax.ShapeDtypeStruct((n, C_V_W), BF16)],
        compiler_params=_cparams("parallel"), name="mla_proj",
    )(z_f, z_f, z_f, z_f, cos_tab, sin_tab, g_q, g_kv, w_q, w_qrot, w_kn, w_v)


ATTN_TILE = 512
ATTN_MAPS_PER_STEP = 6


ATTN_DENOM_ROWS = 16


def _attn_kernel(slopes_ref, lam_ref, q_ref, k_ref, v_ref, g_ref, o_ref, vt_sc, bias_sc, m_sc, acc_sc,
                 *, diff, lam_init, dqk, dv):
    t = ATTN_TILE
    nmap = 2 if diff else 1
    hp = ATTN_MAPS_PER_STEP // nmap
    qi = pl.program_id(2)
    nt = k_ref.shape[1] // t
    if diff:
        slope2 = [slopes_ref[pl.program_id(1) * hp + hh] * LOG2E for hh in range(hp)]

    @pl.when(qi == 0)
    def _():
        for hh in range(hp):
            for j in range(nt):
                vt_sc[hh, j, :dv] = v_ref[0, j * t:(j + 1) * t, hh * dv:(hh + 1) * dv].astype(F32).T.astype(BF16)
                vt_sc[hh, j, dv:] = jnp.ones((ATTN_DENOM_ROWS, t), BF16)
        key = lax.broadcasted_iota(jnp.int32, (t, t), 0)
        qry = lax.broadcasted_iota(jnp.int32, (t, t), 1)
        allowed = (key // CHUNK) <= (qry // CHUNK)
        if diff:
            for hh in range(hp):
                bias_sc[hh, 0] = slope2[hh] * (qry - key).astype(F32)
                bias_sc[hh, 1] = jnp.where(allowed, slope2[hh] * jnp.abs(qry - key).astype(F32), -NEG_BIG)
        else:
            bias_sc[0, 0] = jnp.where(allowed, 0.0, -NEG_BIG)

    qts = []
    for hh in range(hp):
        qt = q_ref[0, :, hh * dqk:(hh + 1) * dqk].astype(F32).T
        if diff:
            feat = lax.broadcasted_iota(jnp.int32, qt.shape, 0)
            qts.append(jnp.where(feat < DA_QK_DIM, qt, 0.0).astype(BF16))
            qts.append(jnp.where(feat >= DA_QK_DIM, qt, 0.0).astype(BF16))
        else:
            qts.append(qt.astype(BF16))

    m_sc[...] = jnp.full(m_sc.shape, NEG_BIG, F32)
    acc_sc[...] = jnp.zeros(acc_sc.shape, F32)

    def step(kj, diag):
        ks = pl.ds(pl.multiple_of(kj * t, t), t)
        scores = []
        for hh in range(hp):
            k = k_ref[0, ks, hh * dqk:(hh + 1) * dqk]
            shift = None
            if diff:
                bias = bias_sc[hh, 1] if diag else bias_sc[hh, 0]
                if not diag:
                    shift = ((qi - kj) * t).astype(F32) * slope2[hh]
            else:
                bias = bias_sc[0, 0] if diag else None
            for i in range(hh * nmap, (hh + 1) * nmap):
                s = _dot(k, qts[i])
                scores.append((s if bias is None else s - bias, shift))
        probs, alphas = [], []
        for i, (s, shift) in enumerate(scores):
            m_prev = m_sc[i]
            m_tile = jnp.max(s, axis=0, keepdims=True)
            m_new = jnp.maximum(m_prev, m_tile if shift is None else m_tile - shift)
            alphas.append(jnp.exp2(m_prev - m_new))
            probs.append(jnp.exp2(s - (m_new if shift is None else m_new + shift)).astype(BF16))
            m_sc[i] = m_new
        for i, p in enumerate(probs):
            acc_sc[i] = alphas[i] * acc_sc[i] + _dot(vt_sc[i // nmap, kj], p)

    def full_step(kj, carry):
        step(kj, False)
        return carry

    lax.fori_loop(0, qi, full_step, 0)
    step(qi, True)

    if diff:
        lp = lam_ref[...]
        lam = (jnp.exp(jnp.sum(lp[0:1] * lp[1:2], axis=1, keepdims=True))
               - jnp.exp(jnp.sum(lp[2:3] * lp[3:4], axis=1, keepdims=True)) + lam_init)
    def normalised(i):
        acc = acc_sc[i]
        return acc[:dv] / acc[dv:dv + 1]

    for hh in range(hp):
        o = normalised(hh * nmap)
        if diff:
            o = o - lam * normalised(hh * nmap + 1)
            o = o * lax.rsqrt(jnp.mean(o * o, axis=0, keepdims=True) + RMS_EPS) * g_ref[...] * (1.0 - lam_init)
        o_ref[0, :, hh * dv:(hh + 1) * dv] = o.T.astype(o_ref.dtype)


def _attention(q_arr, k_arr, v_arr, q_off, k_off, v_off, dqk, dv, heads, *, diff,
               slopes, lam_params, gain_col, lam_init, name):
    b, t, _ = q_arr.shape
    tq = ATTN_TILE
    nmap = 2 if diff else 1
    hp = ATTN_MAPS_PER_STEP // nmap
    assert heads % hp == 0 and q_off % hp == 0 and k_off % hp == 0 and v_off % hp == 0
    nbias = (hp, 2) if diff else (1, 1)
    return pl.pallas_call(
        functools.partial(_attn_kernel, diff=diff, lam_init=lam_init, dqk=dqk, dv=dv),
        grid=(b, heads // hp, t // tq),
        in_specs=[
            pl.BlockSpec(memory_space=pltpu.SMEM),
            pl.BlockSpec(lam_params.shape, lambda bi, g, i: (0, 0)),
            pl.BlockSpec((1, tq, hp * dqk), lambda bi, g, i: (bi, i, q_off // hp + g)),
            pl.BlockSpec((1, t, hp * dqk), lambda bi, g, i: (bi, 0, k_off // hp + g)),
            pl.BlockSpec((1, t, hp * dv), lambda bi, g, i: (bi, 0, v_off // hp + g)),
            pl.BlockSpec(gain_col.shape, lambda bi, g, i: (0, 0)),
        ],
        out_specs=pl.BlockSpec((1, tq, hp * dv), lambda bi, g, i: (bi, i, g)),
        out_shape=jax.ShapeDtypeStruct((b, t, heads * dv), BF16),
        scratch_shapes=[pltpu.VMEM((hp, t // tq, dv + ATTN_DENOM_ROWS, tq), BF16), pltpu.VMEM(nbias + (tq, tq), F32),
                        pltpu.VMEM((hp * nmap, 1, tq), F32),
                        pltpu.VMEM((hp * nmap, dv + ATTN_DENOM_ROWS, tq), F32)],
        compiler_params=_cparams("parallel", "parallel", "arbitrary"), name=name,
    )(slopes, lam_params, q_arr, k_arr, v_arr, gain_col)


def _alibi_slopes(n):
    def pow2_slopes(m):
        start = 2.0 ** (-8.0 / m)
        return [start ** (i + 1) for i in range(m)]
    c = 2 ** int(math.floor(math.log2(n)))
    s = pow2_slopes(c)
    if c < n:
        s = s + pow2_slopes(2 * c)[0::2][: n - c]
    return np.array(s, dtype=np.float32)


GLA_GROUP = 8


def _gla_kernel(q_ref, k_ref, v_ref, r_ref, glr_ref, w2_ref, bg_ref, ng_ref, o_ref):
    c = CHUNK
    gc = GLA_GROUP * c
    scale = GLA_DK ** -0.5
    ri = lax.broadcasted_iota(jnp.int32, (c, c), 0)
    ci = lax.broadcasted_iota(jnp.int32, (c, c), 1)
    causal = ci <= ri
    tri = causal.astype(BF16)
    w2 = w2_ref[...]
    bg = bg_ref[...]
    ng = ng_ref[...]
    chunks = [slice(g * c, (g + 1) * c) for g in range(GLA_GROUP)]

    def body(n, state_t):
        rows = pl.ds(pl.multiple_of(n * gc, gc), gc)
        pre = _dot(glr_ref[0, rows, :].astype(BF16), w2) + bg
        log_a = (jnp.minimum(pre, 0.0) - jnp.log1p(jnp.exp(-jnp.abs(pre)))) / GLA_GATE_NORMALIZER
        hi = log_a.astype(BF16)
        rem = log_a - hi.astype(F32)
        mid = rem.astype(BF16)
        lo = (rem - mid.astype(F32)).astype(BF16)
        bcum = [_dot(tri, hi[s]) + _dot(tri, mid[s]) + _dot(tri, lo[s]) for s in chunks]
        b_mid = [b[c // 2:c // 2 + 1, :] for b in bcum]
        b_last = [b[c - 1:c, :] for b in bcum]
        q = q_ref[0, rows, :] * scale
        k = k_ref[0, rows, :]
        v = v_ref[0, rows, :].astype(BF16)
        att = [_dot_nt((q[s] * jnp.exp(b - bm)).astype(BF16), (k[s] * jnp.exp(bm - b)).astype(BF16))
               for s, b, bm in zip(chunks, bcum, b_mid)]
        att = [jnp.where(causal, a, 0.0).astype(BF16) for a in att]
        o_intra = [_dot(a, v[s]) for a, s in zip(att, chunks)]
        upd_t = [_dot_tn(v[s], (k[s] * jnp.exp(bl - b)).astype(BF16)) for s, b, bl in zip(chunks, bcum, b_last)]
        q_dec = [(q[s] * jnp.exp(b)).astype(BF16) for s, b in zip(chunks, bcum)]
        outs = []
        for g in range(GLA_GROUP):
            outs.append(o_intra[g] + _dot_nt(q_dec[g], state_t.astype(BF16)))
            state_t = state_t * jnp.exp(b_last[g]) + upd_t[g]
        o = jnp.concatenate(outs, axis=0)
        r = r_ref[0, rows, :]
        o_ref[0, rows, :] = (_rms_rows(o, ng) * (r * jax.nn.sigmoid(r))).astype(o_ref.dtype)
        return state_t

    lax.fori_loop(0, q_ref.shape[1] // gc, body, jnp.zeros((GLA_DV, GLA_DK), F32))


def _gla(z_f3, w_gate2_pad, b_gate, norm_g, layer):
    b, t, _ = z_f3.shape

    def zcols(width, off):
        return pl.BlockSpec((1, t, width), lambda bi, h: (bi, 0, off // width + h))

    return pl.pallas_call(
        _gla_kernel, grid=(b, GLA_HEADS),
        in_specs=[zcols(GLA_DK, ZF_BQ), zcols(GLA_DK, ZF_BK), zcols(GLA_DV, ZF_BV), zcols(GLA_DV, ZF_BR),
                  pl.BlockSpec((1, t, ZF_GLR_W), lambda bi, h: (bi, 0, ZF_GLR // ZF_GLR_W)),
                  pl.BlockSpec((None, ZF_GLR_W, GLA_DK), lambda bi, h: (layer, 0, h)),
                  pl.BlockSpec((None, 1, GLA_DK), lambda bi, h: (layer, 0, h)),
                  pl.BlockSpec((None, 1, GLA_DV), lambda bi, h: (layer, 0, 0))],
        out_specs=pl.BlockSpec((1, t, GLA_DV), lambda bi, h: (bi, 0, h)),
        out_shape=jax.ShapeDtypeStruct((b, t, B_V_W), BF16),
        compiler_params=_cparams("parallel", "parallel"), name="gla",
    )(z_f3, z_f3, z_f3, z_f3, z_f3, w_gate2_pad, b_gate.reshape(b_gate.shape[0], 1, -1),
      norm_g.reshape(norm_g.shape[0], 1, -1))


def _rot_half_cols(w):
    half = w.shape[-1] // 2
    return jnp.concatenate([-w[..., half:], w[..., :half]], axis=-1)


IN_SPLITS = (A_QK_W, A_QK_W, A_V_W, B_K_W, B_K_W, B_V_W, GLA_GATE_RANK, B_V_W, MLA_Q_RANK, MLA_KV_RANK + MLA_ROPE_DIM)


def _prep_in_kernel(w_ref, wa_ref, wf_ref):
    offs = [0] + list(np.cumsum(IN_SPLITS))
    (b_q, b_k, b_v, b_glr, b_r, c_qa, c_kva) = [(int(offs[i]), int(offs[i + 1])) for i in range(3, 10)]
    half = MLA_ROPE_DIM // 2

    def cols(lo, hi):
        return w_ref[:, lo:hi].astype(BF16)

    wa_ref[...] = cols(0, int(offs[3]))
    wf_ref[...] = jnp.zeros(wf_ref.shape, BF16)
    for dst, (lo, hi) in ((ZF_CQA, c_qa), (ZF_BQ, b_q), (ZF_BK, b_k), (ZF_BV, b_v), (ZF_BR, b_r),
                          (ZF_CKV, (c_kva[0], c_kva[0] + MLA_KV_RANK)), (ZF_GLR, b_glr)):
        wf_ref[:, dst:dst + hi - lo] = cols(lo, hi)
    pe = c_kva[0] + MLA_KV_RANK
    wf_ref[:, ZF_KPE + MLA_NOPE_DIM:ZF_KPE + MLA_NOPE_DIM + MLA_ROPE_DIM] = cols(pe, pe + MLA_ROPE_DIM)
    wf_ref[:, ZF_KPER + MLA_NOPE_DIM:ZF_KPER + MLA_NOPE_DIM + half] = -cols(pe + half, pe + MLA_ROPE_DIM)
    wf_ref[:, ZF_KPER + MLA_NOPE_DIM + half:ZF_KPER + MLA_NOPE_DIM + MLA_ROPE_DIM] = cols(pe, pe + half)


def _prep_in_proj(w_in):
    nl, d, width = w_in.shape
    assert width == sum(IN_SPLITS)
    a_w = sum(IN_SPLITS[:3])
    tr = min(256, d)
    return pl.pallas_call(
        _prep_in_kernel, grid=(nl, d // tr),
        in_specs=[pl.BlockSpec((None, tr, width), lambda l, i: (l, i, 0))],
        out_specs=[pl.BlockSpec((None, tr, a_w), lambda l, i: (l, i, 0)),
                   pl.BlockSpec((None, tr, ZF_W), lambda l, i: (l, i, 0))],
        out_shape=[jax.ShapeDtypeStruct((nl, d, a_w), BF16), jax.ShapeDtypeStruct((nl, d, ZF_W), BF16)],
        compiler_params=_cparams("parallel", "parallel"), name="prep_in_proj",
    )(w_in)


def _prep_mla(w_qb, w_kvb):
    lead = w_qb.shape[:-1]
    wq = w_qb.reshape(lead + (MLA_HEADS, MLA_NOPE_DIM + MLA_ROPE_DIM))
    nope, rope = wq[..., :MLA_NOPE_DIM], wq[..., MLA_NOPE_DIM:]
    pad = jnp.zeros(lead + (MLA_HEADS, MLA_HEAD_PAD - MLA_NOPE_DIM - MLA_ROPE_DIM), w_qb.dtype)
    w_q = jnp.concatenate([nope, rope, pad], axis=-1).reshape(lead + (MLA_QK_W,))
    w_qrot = jnp.concatenate([jnp.zeros_like(nope), _rot_half_cols(rope), pad], axis=-1).reshape(lead + (MLA_QK_W,))
    lead = w_kvb.shape[:-1]
    wkv = w_kvb.reshape(lead + (MLA_HEADS, MLA_NOPE_DIM + MLA_V_DIM))
    k_nope, v = wkv[..., :MLA_NOPE_DIM], wkv[..., MLA_NOPE_DIM:]
    w_kn = jnp.concatenate([k_nope, jnp.zeros(lead + (MLA_HEADS, MLA_HEAD_PAD - MLA_NOPE_DIM), w_kvb.dtype)],
                           axis=-1).reshape(lead + (MLA_QK_W,))
    return w_q.astype(BF16), w_qrot.astype(BF16), w_kn.astype(BF16), v.reshape(lead + (C_V_W,)).astype(BF16)


FFN_SRC_BLOCK = 256


def _cast_gu_kernel(*refs, nb, k):
    srcs, o_ref = refs[:-1], refs[-1]
    f = pl.program_id(1)
    sb = srcs[0].shape[1]
    for j, src in enumerate(srcs):
        x = src[...].astype(BF16)
        s = j % k
        if nb % k and s >= nb % k:
            x = jnp.where(f * k + s < nb, x, jnp.zeros_like(x))
        o_ref[:, j * sb:(j + 1) * sb] = x


def _prep_ffn_gu(w_gu, tf):
    nl, two, d, f2 = w_gu.shape
    ff = f2 // 2
    sb = FFN_SRC_BLOCK
    assert ff % sb == 0 and tf % sb == 0
    nb, k = ff // sb, tf // sb
    nf = -(-nb // k)

    def cols(off, s):
        return pl.BlockSpec((None, None, d, sb),
                            lambda lj, f: (lj // two, lj % two, 0, off + jnp.minimum(f * k + s, nb - 1)))

    return pl.pallas_call(
        functools.partial(_cast_gu_kernel, nb=nb, k=k), grid=(nl * two, nf),
        in_specs=[cols(off, s) for off in (0, nb) for s in range(k)],
        out_specs=pl.BlockSpec((None, None, None, d, 2 * tf), lambda lj, f: (lj // two, lj % two, f, 0, 0)),
        out_shape=jax.ShapeDtypeStruct((nl, two, nf, d, 2 * tf), BF16),
        compiler_params=_cparams("parallel", "parallel"), name="cast_gu",
    )(*([w_gu] * (2 * k)))


def kernel(x, p, positions, emb_ln_g, emb_ln_b, w_in, w_branch_gate, da_lambda_q1, da_lambda_k1, da_lambda_q2,
           da_lambda_k2, da_subln_g, gla_w_gate2, gla_b_gate, gla_norm_g, mla_q_norm_g, mla_w_qb, mla_kv_norm_g,
           mla_w_kvb, w_o_a, w_o_b, w_o_c, w_out, ffn_w_gu, ffn_w_down, ln_g, ln_b, ple_w_proj, ple_w_gate):
    b, t, d = x.shape
    n = b * t
    depth = w_in.shape[0]
    alpha = float(DEEPNORM_ALPHA)

    w_gu = _prep_ffn_gu(ffn_w_gu, FFN_CHUNK)
    ff_pad = w_gu.shape[2] * FFN_CHUNK - ffn_w_down.shape[2]
    w_down = jnp.pad(ffn_w_down.astype(BF16), ((0, 0), (0, 0), (0, ff_pad), (0, 0)))
    w_a, w_f = _prep_in_proj(w_in)
    w_q, w_qrot, w_kn, w_v = _prep_mla(mla_w_qb, mla_w_kvb)
    w_bg = w_branch_gate.astype(BF16)
    w_oa, w_ob, w_oc, w_o = (w.astype(BF16) for w in (w_o_a, w_o_b, w_o_c, w_out))
    w_pg, w_pp = ple_w_gate.astype(BF16), ple_w_proj.astype(BF16)
    w2_pad = jnp.pad(gla_w_gate2.astype(BF16), ((0, 0), (0, ZF_GLR_W - GLA_GATE_RANK), (0, 0)))
    p = p.reshape(depth, n, -1)

    cos_tab, sin_tab = _rope_tables(positions)
    slopes = jnp.asarray(_alibi_slopes(DA_HEADS))
    a_scale = np.ones((1, 2 * A_QK_W + A_V_W), np.float32)
    a_scale[0, :A_QK_W] = DA_Q_SCALE
    a_scale = jnp.asarray(a_scale)
    no_slopes = jnp.zeros((1,), F32)
    no_lam = jnp.zeros((4, DA_QK_DIM), F32)
    no_gain = jnp.ones((MLA_V_DIM, 1), F32)

    h, hb = _layer_norm(x.reshape(n, d), emb_ln_g, emb_ln_b)
    for i in range(depth):
        h, hb = _ffn_ln(h, hb, w_gu, w_down, i, 0, ln_g[i, 0], ln_b[i, 0], alpha)

        z_a = _matmul(hb, w_a, i, BF16, "in_proj_a", col_scale=a_scale).reshape(b, t, -1)
        z_f = _matmul(hb, w_f, i, F32, "in_proj_f")
        lam_init = 0.8 - 0.6 * math.exp(-0.3 * i)
        lam_params = jnp.stack([da_lambda_q1[i], da_lambda_k1[i], da_lambda_q2[i], da_lambda_k2[i]]).astype(F32)
        o_a = _attention(z_a, z_a, z_a, 0, DA_HEADS, 2 * DA_HEADS, 2 * DA_QK_DIM, DA_V_DIM, DA_HEADS,
                         diff=True, slopes=slopes, lam_params=lam_params,
                         gain_col=da_subln_g[i].reshape(-1, 1), lam_init=lam_init, name="diff_attn")
        o_b = _gla(z_f.reshape(b, t, ZF_W), w2_pad, gla_b_gate, gla_norm_g, i)
        q_c, k_c, v_c = _mla_proj(z_f, cos_tab, sin_tab, mla_q_norm_g, mla_kv_norm_g, w_q, w_qrot, w_kn, w_v, i)
        o_c = _attention(q_c.reshape(b, t, -1), k_c.reshape(b, t, -1), v_c.reshape(b, t, -1), 0, 0, 0,
                         MLA_HEAD_PAD, MLA_V_DIM, MLA_HEADS, diff=False,
                         slopes=no_slopes, lam_params=no_lam,
                         gain_col=no_gain, lam_init=0.0, name="mla_attn")
        merged = _merge(hb, o_a.reshape(n, -1), o_b.reshape(n, -1), o_c.reshape(n, -1), w_bg, w_oa, w_ob, w_oc, i)
        h, hb = _proj_ln(h, merged, w_o, i, ln_g[i, 1], ln_b[i, 1], alpha)

        resid = _ple_resid(hb, w_pg, p, w_pp, i, h, alpha)
        h, hb = _ffn_ln(resid, hb, w_gu, w_down, i, 1, ln_g[i, 2], ln_b[i, 2], 1.0)
    return h.reshape(b, t, d)
```
